```python
import jax, jax.numpy as jnp
from jax import lax
import numpy as np

D_MODEL = 1024
BATCH = 2
SEQ = 8192
DEPTH = 4
DEC_BATCH = 128
DEC_SEQ = 4
PAST_LEN = 8192
PAGE_SIZE = 128

N_META = 16
HEAD_DIM = 64
N_Q_HEADS = 8
N_KV_HEADS = 2
Q_PER_KV = N_Q_HEADS // N_KV_HEADS
ATTN_WIDTH = N_Q_HEADS * HEAD_DIM
KV_WIDTH = N_KV_HEADS * HEAD_DIM
CONV_CH = D_MODEL - ATTN_WIDTH
MIX_WIDTH = ATTN_WIDTH + CONV_CH
Q_END = ATTN_WIDTH
K_END = Q_END + KV_WIDTH
V_END = K_END + KV_WIDTH
IN_COLS = V_END + 2 * CONV_CH
CONV_WIDTH = 31
CONV_STATE = CONV_WIDTH - 1
WINDOW = 128
BLOCK = 128
ROPE_THETA = 10000.0
ATTN_SCALE = HEAD_DIM ** -0.5
D_FF_DENSE = 2816
N_EXPERTS = 8
TOP_K = 2
D_FF_EXPERT = 1024
N_DENSE = (DEPTH + 1) // 2
N_MOE = DEPTH // 2
ALPHA = (2 * DEPTH) ** 0.25
BETA = (8 * DEPTH) ** -0.25
LN_EPS = 1e-5

kernel_name = 'hymba_swa_conformer_deepnorm_step'


def layer_norm(x, g, b):
    xf = x.astype(jnp.float32)
    mu = jnp.mean(xf, -1, keepdims=True)
    var = jnp.mean(jnp.square(xf - mu), -1, keepdims=True)
    return ((xf - mu) * lax.rsqrt(var + LN_EPS)).astype(x.dtype) * g + b


def rope(x, pos):
    half = HEAD_DIM // 2
    inv = ROPE_THETA ** (-jnp.arange(half, dtype=jnp.float32) / half)
    ang = pos.astype(jnp.float32)[:, None] * inv[None, :]
    cos = jnp.cos(ang)[:, None, :]
    sin = jnp.sin(ang)[:, None, :]
    x1 = x[..., :half].astype(jnp.float32)
    x2 = x[..., half:].astype(jnp.float32)
    return jnp.concatenate([x1 * cos - x2 * sin, x2 * cos + x1 * sin], -1).astype(x.dtype)


def split_proj(proj):
    lead = proj.shape[:-1]
    q = proj[..., :Q_END].reshape(*lead, N_Q_HEADS, HEAD_DIM)
    k = proj[..., Q_END:K_END].reshape(*lead, N_KV_HEADS, HEAD_DIM)
    v = proj[..., K_END:V_END].reshape(*lead, N_KV_HEADS, HEAD_DIM)
    u = proj[..., V_END:V_END + CONV_CH] * jax.nn.sigmoid(proj[..., V_END + CONV_CH:])
    return q, k, v, u


def sink_softmax(s, mask, sinks):
    sk = sinks.astype(jnp.float32).reshape(N_KV_HEADS, Q_PER_KV, 1, 1)
    s = jnp.where(mask, s, -jnp.inf)
    m = jnp.maximum(jnp.max(s, -1, keepdims=True), sk)
    p = jnp.exp(s - m)
    return p / (jnp.sum(p, -1, keepdims=True) + jnp.exp(sk - m))


def prompt_attention(q, k, v, sinks):
    B, L = q.shape[:2]
    nb = -(-L // BLOCK)
    pad = nb * BLOCK - L
    qb = jnp.pad(q, ((0, 0), (0, pad), (0, 0), (0, 0))).reshape(B, nb, BLOCK, N_KV_HEADS, Q_PER_KV, HEAD_DIM)

    def band(a):
        ap = jnp.pad(a, ((0, 0), (BLOCK, pad), (0, 0), (0, 0))).reshape(B, nb + 1, BLOCK, N_KV_HEADS, HEAD_DIM)
        return jnp.concatenate([ap[:, :-1], ap[:, 1:]], axis=2)

    kb, vb = band(k), band(v)
    km, vm = k[:, :N_META], v[:, :N_META]
    s_meta = jnp.einsum('bnqkgd,bmkd->bnkgqm', qb, km, preferred_element_type=jnp.float32)
    s_band = jnp.einsum('bnqkgd,bnskd->bnkgqs', qb, kb, preferred_element_type=jnp.float32)
    s = jnp.concatenate([s_meta, s_band], -1) * ATTN_SCALE
    t = (jnp.arange(nb)[:, None] * BLOCK + jnp.arange(BLOCK)[None, :])[:, :, None]
    j = ((jnp.arange(nb)[:, None] - 1) * BLOCK + jnp.arange(2 * BLOCK)[None, :])[:, None, :]
    d = t - j
    band_mask = (d >= 0) & (d <= WINDOW) & (j >= N_META)
    meta_mask = jnp.broadcast_to(jnp.arange(N_META)[None, None, :] <= t, (nb, BLOCK, N_META))
    mask = jnp.concatenate([meta_mask, band_mask], -1)[None, :, None, None]
    p = sink_softmax(s, mask, sinks).astype(v.dtype)
    o = (jnp.einsum('bnkgqm,bmkd->bnqkgd', p[..., :N_META], vm)
         + jnp.einsum('bnkgqs,bnskd->bnqkgd', p[..., N_META:], vb))
    return o.reshape(B, nb * BLOCK, ATTN_WIDTH)[:, :L]


def sample_attention(q, k, v, mk, mv, kbuf, vbuf, sinks):
    DB, S = q.shape[:2]
    W = kbuf.shape[1]
    qpos = PAST_LEN + jnp.arange(S)
    kpos = PAST_LEN - W + jnp.arange(W + S)
    k_loc = jnp.concatenate([kbuf, k], 1)
    v_loc = jnp.concatenate([vbuf, v], 1)
    qg = q.reshape(DB, S, N_KV_HEADS, Q_PER_KV, HEAD_DIM)
    s_meta = jnp.einsum('bqkgd,bmkd->bkgqm', qg, mk, preferred_element_type=jnp.float32)
    s_loc = jnp.einsum('bqkgd,bskd->bkgqs', qg, k_loc, preferred_element_type=jnp.float32)
    s = jnp.concatenate([s_meta, s_loc], -1) * ATTN_SCALE
    d = qpos[:, None] - kpos[None, :]
    loc_mask = (d >= 0) & (d <= WINDOW) & (kpos[None, :] >= N_META)
    mask = jnp.concatenate([jnp.ones((S, N_META), bool), loc_mask], -1)
    p = sink_softmax(s, mask, sinks).astype(v.dtype)
    o = (jnp.einsum('bkgqm,bmkd->bqkgd', p[..., :N_META], mv)
         + jnp.einsum('bkgqs,bskd->bqkgd', p[..., N_META:], v_loc))
    return o.reshape(DB, S, ATTN_WIDTH), k_loc[:, -W:], v_loc[:, -W:]


def conformer_conv(u_ext, w_dw, b_dw, g, b):
    y = lax.conv_general_dilated(u_ext, w_dw[:, None, :], (1,), 'VALID',
                                 dimension_numbers=('NWC', 'WIO', 'NWC'),
                                 feature_group_count=CONV_CH) + b_dw
    return jax.nn.silu(layer_norm(y, g, b))


def swiglu(x, wg, wu, wd):
    return (jax.nn.silu(x @ wg) * (x @ wu)) @ wd


def moe_swiglu(x, rw, rb, eg, eu, ed):
    logits = (x @ rw).astype(jnp.float32) + rb.astype(jnp.float32)
    top_v, top_i = lax.top_k(logits, TOP_K)
    gates = jax.nn.softmax(top_v, -1)
    combine = jnp.sum(jax.nn.one_hot(top_i, N_EXPERTS, dtype=jnp.float32) * gates[..., None], -2)
    hid = jax.nn.silu(jnp.einsum('...d,edf->...ef', x, eg)) * jnp.einsum('...d,edf->...ef', x, eu)
    hid = hid * combine[..., None].astype(hid.dtype)
    return jnp.einsum('...ef,efd->...d', hid, ed)


def setup_inputs(seed: int = 0) -> dict:
    key = jax.random.key(seed)
    ks = iter(jax.random.split(key, 40))
    nrm = lambda shape, scale: jax.random.normal(next(ks), shape, jnp.float32) * scale
    win_buf = min(WINDOW, PAST_LEN)
    return {
        'x_prompt': nrm((BATCH, SEQ, D_MODEL), 1.0),
        'x_sample': nrm((DEC_BATCH, DEC_SEQ, D_MODEL), 1.0),
        'cache_meta_k': nrm((DEPTH, DEC_BATCH, N_META, N_KV_HEADS, HEAD_DIM), 1.0),
        'cache_meta_v': nrm((DEPTH, DEC_BATCH, N_META, N_KV_HEADS, HEAD_DIM), 1.0),
        'cache_k': nrm((DEPTH, DEC_BATCH, win_buf, N_KV_HEADS, HEAD_DIM), 1.0),
        'cache_v': nrm((DEPTH, DEC_BATCH, win_buf, N_KV_HEADS, HEAD_DIM), 1.0),
        'state_conv': nrm((DEPTH, DEC_BATCH, CONV_STATE, CONV_CH), 0.5),
        'meta_tokens': nrm((N_META, D_MODEL), 1.0),
        'ln_in_g': 1.0 + nrm((D_MODEL,), 0.02),
        'ln_in_b': nrm((D_MODEL,), 0.02),
        'w_in': nrm((DEPTH, D_MODEL, IN_COLS), D_MODEL ** -0.5),
        'w_dw': nrm((DEPTH, CONV_WIDTH, CONV_CH), CONV_WIDTH ** -0.5),
        'b_dw': nrm((DEPTH, CONV_CH), 0.02),
        'conv_ln_g': 1.0 + nrm((DEPTH, CONV_CH), 0.02),
        'conv_ln_b': nrm((DEPTH, CONV_CH), 0.02),
        'sinks': nrm((DEPTH, N_Q_HEADS), 0.5),
        'w_out': nrm((DEPTH, MIX_WIDTH, D_MODEL), MIX_WIDTH ** -0.5 * BETA),
        'ln1_g': 1.0 + nrm((DEPTH, D_MODEL), 0.02),
        'ln1_b': nrm((DEPTH, D_MODEL), 0.02),
        'ln2_g': 1.0 + nrm((DEPTH, D_MODEL), 0.02),
        'ln2_b': nrm((DEPTH, D_MODEL), 0.02),
        'ffd_w_gate': nrm((N_DENSE, D_MODEL, D_FF_DENSE), D_MODEL ** -0.5),
        'ffd_w_up': nrm((N_DENSE, D_MODEL, D_FF_DENSE), D_MODEL ** -0.5),
        'ffd_w_down': nrm((N_DENSE, D_FF_DENSE, D_MODEL), D_FF_DENSE ** -0.5 * BETA),
        'router_w': nrm((N_MOE, D_MODEL, N_EXPERTS), D_MODEL ** -0.5),
        'router_b': nrm((N_MOE, N_EXPERTS), 0.01),
        'exp_w_gate': nrm((N_MOE, N_EXPERTS, D_MODEL, D_FF_EXPERT), D_MODEL ** -0.5),
        'exp_w_up': nrm((N_MOE, N_EXPERTS, D_MODEL, D_FF_EXPERT), D_MODEL ** -0.5),
        'exp_w_down': nrm((N_MOE, N_EXPERTS, D_FF_EXPERT, D_MODEL), D_FF_EXPERT ** -0.5 * BETA),
    }


def reference(x_prompt, x_sample, cache_meta_k, cache_meta_v, cache_k, cache_v, state_conv,
              meta_tokens, ln_in_g, ln_in_b, w_in, w_dw, b_dw, conv_ln_g, conv_ln_b, sinks, w_out,
              ln1_g, ln1_b, ln2_g, ln2_b, ffd_w_gate, ffd_w_up, ffd_w_down,
              router_w, router_b, exp_w_gate, exp_w_up, exp_w_down):
    B = x_prompt.shape[0]
    meta = jnp.broadcast_to(meta_tokens[None].astype(x_prompt.dtype), (B, N_META, D_MODEL))
    hp = layer_norm(jnp.concatenate([meta, x_prompt], 1), ln_in_g, ln_in_b)
    hs = layer_norm(x_sample, ln_in_g, ln_in_b)
    L = hp.shape[1]
    pos_p = jnp.arange(L)
    pos_s = PAST_LEN + jnp.arange(hs.shape[1])

    pmk, pmv, pk, pv, pc, sk, sv, sc = [], [], [], [], [], [], [], []
    for i in range(DEPTH):
        q, k, v, u = split_proj(hp @ w_in[i])
        q, k = rope(q, pos_p), rope(k, pos_p)
        a = prompt_attention(q, k, v, sinks[i])
        c = conformer_conv(jnp.pad(u, ((0, 0), (CONV_STATE, 0), (0, 0))), w_dw[i], b_dw[i], conv_ln_g[i], conv_ln_b[i])
        mix_p = jnp.concatenate([a, c], -1) @ w_out[i]
        pmk.append(k[:, :N_META]); pmv.append(v[:, :N_META])
        pk.append(k[:, -WINDOW:]); pv.append(v[:, -WINDOW:])
        pc.append(u[:, -CONV_STATE:])

        q2, k2, v2, u2 = split_proj(hs @ w_in[i])
        q2, k2 = rope(q2, pos_s), rope(k2, pos_s)
        a2, nk, nv = sample_attention(q2, k2, v2, cache_meta_k[i], cache_meta_v[i], cache_k[i], cache_v[i], sinks[i])
        u_ext = jnp.concatenate([state_conv[i].astype(u2.dtype), u2], 1)
        c2 = conformer_conv(u_ext, w_dw[i], b_dw[i], conv_ln_g[i], conv_ln_b[i])
        mix_s = jnp.concatenate([a2, c2], -1) @ w_out[i]
        sk.append(nk); sv.append(nv); sc.append(u_ext[:, -CONV_STATE:])

        hp = layer_norm(ALPHA * hp + mix_p, ln1_g[i], ln1_b[i])
        hs = layer_norm(ALPHA * hs + mix_s, ln1_g[i], ln1_b[i])
        if i % 2 == 0:
            j = i // 2
            fp = swiglu(hp, ffd_w_gate[j], ffd_w_up[j], ffd_w_down[j])
            fs = swiglu(hs, ffd_w_gate[j], ffd_w_up[j], ffd_w_down[j])
        else:
            j = i // 2
            fp = moe_swiglu(hp, router_w[j], router_b[j], exp_w_gate[j], exp_w_up[j], exp_w_down[j])
            fs = moe_swiglu(hs, router_w[j], router_b[j], exp_w_gate[j], exp_w_up[j], exp_w_down[j])
        hp = layer_norm(ALPHA * hp + fp, ln2_g[i], ln2_b[i])
        hs = layer_norm(ALPHA * hs + fs, ln2_g[i], ln2_b[i])

    y_prompt = hp[:, N_META:]
    y_sample = hs
    return (y_prompt, y_sample, jnp.stack(pmk), jnp.stack(pmv), jnp.stack(pk), jnp.stack(pv), jnp.stack(pc),
            jnp.stack(sk), jnp.stack(sv), jnp.stack(sc))
```

```python
import functools

import jax
import jax.numpy as jnp
from jax import lax
from jax.experimental import pallas as pl
from jax.experimental.pallas import tpu as pltpu

D_MODEL = 1024
BATCH = 2
SEQ = 8192
DEPTH = 4
DEC_BATCH = 128
DEC_SEQ = 4
PAST_LEN = 8192
N_META = 16
HEAD_DIM = 64
N_Q_HEADS = 8
N_KV_HEADS = 2
ATTN_WIDTH = N_Q_HEADS * HEAD_DIM
KV_WIDTH = N_KV_HEADS * HEAD_DIM
CONV_CH = D_MODEL - ATTN_WIDTH
Q_END = ATTN_WIDTH
K_END = Q_END + KV_WIDTH
V_END = K_END + KV_WIDTH
IN_COLS = V_END + 2 * CONV_CH
CONV_WIDTH = 31
CONV_STATE = CONV_WIDTH - 1
WINDOW = 128
BLOCK = 128
ROPE_THETA = 10000.0
ATTN_SCALE = HEAD_DIM ** -0.5
D_FF_DENSE = 2816
N_EXPERTS = 8
D_FF_EXPERT = 1024
ALPHA = (2 * DEPTH) ** 0.25
LN_EPS = 1e-5

LANES = 128
L_REAL = N_META + SEQ
N_BLK = -(-L_REAL // BLOCK)
LP = N_BLK * BLOCK
NP = BATCH * LP
NS = DEC_BATCH * DEC_SEQ
TM = 512
TM_MOE = 1024
T_PAD = -(-(NP + NS) // TM_MOE) * TM_MOE
W_LOC = WINDOW + DEC_SEQ
CONV_TILE = 640
CONV_HALO = 32
CONV_CHUNK = 32
SAMPLE_BT = 16
VMEM_LIMIT = 56 * 1024 * 1024

BF = jnp.bfloat16
F32 = jnp.float32


def _cparams(n_axes):
    return pltpu.CompilerParams(dimension_semantics=("arbitrary",) * n_axes,
                                vmem_limit_bytes=VMEM_LIMIT)


def _ln_rows(x, g, b):
    mu = jnp.mean(x, -1, keepdims=True)
    xc = x - mu
    var = jnp.mean(xc * xc, -1, keepdims=True)
    return xc * lax.rsqrt(var + LN_EPS) * g + b


def _silu(x):
    return x * jax.nn.sigmoid(x)


def _ln_kernel(x_ref, g_ref, b_ref, o_ref):
    o_ref[...] = _ln_rows(x_ref[...], g_ref[...], b_ref[...])


def _ln_call(x, g, b):
    return pl.pallas_call(
        _ln_kernel,
        grid=(T_PAD // TM,),
        in_specs=[pl.BlockSpec((TM, D_MODEL), lambda i: (i, 0)),
                  pl.BlockSpec((1, D_MODEL), lambda i: (0, 0)),
                  pl.BlockSpec((1, D_MODEL), lambda i: (0, 0))],
        out_specs=pl.BlockSpec((TM, D_MODEL), lambda i: (i, 0)),
        out_shape=jax.ShapeDtypeStruct((T_PAD, D_MODEL), F32),
        compiler_params=_cparams(1),
        name="ln_in",
    )(x, g.reshape(1, D_MODEL), b.reshape(1, D_MODEL))


def _inproj_kernel(h_ref, w_ref, cos_ref, sin_ref, q_ref, k_ref, v_ref, u_ref):
    hb = h_ref[...].astype(BF)
    cos = cos_ref[...]
    sin = sin_ref[...]
    lane = lax.broadcasted_iota(jnp.int32, (TM, LANES), 1)
    first_half = (lane % HEAD_DIM) < (HEAD_DIM // 2)

    def rope(x):
        rot = jnp.where(first_half, pltpu.roll(x, LANES - HEAD_DIM // 2, 1),
                        pltpu.roll(x, HEAD_DIM // 2, 1))
        return x * cos + rot * sin

    qk = jnp.dot(hb, w_ref[:, :K_END], preferred_element_type=F32)
    for c in range(Q_END // LANES):
        q_ref[:, c * LANES:(c + 1) * LANES] = (
            rope(qk[:, c * LANES:(c + 1) * LANES]) * ATTN_SCALE).astype(BF)
    k_ref[...] = rope(qk[:, Q_END:K_END])
    v_ref[...] = jnp.dot(hb, w_ref[:, K_END:V_END], preferred_element_type=F32)
    val = jnp.dot(hb, w_ref[:, V_END:V_END + CONV_CH], preferred_element_type=F32)
    gate = jnp.dot(hb, w_ref[:, V_END + CONV_CH:], preferred_element_type=F32)
    u_ref[...] = val * jax.nn.sigmoid(gate)


def _inproj_call(h, w_bf, cos_t, sin_t):
    row = lambda i: (i, 0)
    fixed = lambda i: (0, 0)
    return pl.pallas_call(
        _inproj_kernel,
        grid=(T_PAD // TM,),
        in_specs=[pl.BlockSpec((TM, D_MODEL), row),
                  pl.BlockSpec((D_MODEL, IN_COLS), fixed),
                  pl.BlockSpec((TM, LANES), row),
                  pl.BlockSpec((TM, LANES), row)],
        out_specs=[pl.BlockSpec((TM, ATTN_WIDTH), row),
                   pl.BlockSpec((TM, KV_WIDTH), row),
                   pl.BlockSpec((TM, KV_WIDTH), row),
                   pl.BlockSpec((TM, CONV_CH), row)],
        out_shape=[jax.ShapeDtypeStruct((T_PAD, ATTN_WIDTH), BF),
                   jax.ShapeDtypeStruct((T_PAD, KV_WIDTH), F32),
                   jax.ShapeDtypeStruct((T_PAD, KV_WIDTH), F32),
                   jax.ShapeDtypeStruct((T_PAD, CONV_CH), F32)],
        compiler_params=_cparams(1),
        name="inproj",
    )(h, w_bf, cos_t, sin_t)


N_KEYS = 2 * BLOCK + N_META


def _head_variants(x):
    lane = lax.broadcasted_iota(jnp.int32, x.shape, x.ndim - 1)
    lo = lane < HEAD_DIM
    xr = pltpu.roll(x, HEAD_DIM, x.ndim - 1)
    zero = jnp.zeros_like(x)
    return [[jnp.where(lo, x, zero).astype(BF), jnp.where(lo, zero, xr).astype(BF)],
            [jnp.where(lo, xr, zero).astype(BF), jnp.where(lo, zero, x).astype(BF)]]


def _attn_kernel(sinks_ref, q_ref, kp_ref, kc_ref, km_ref, vp_ref, vc_ref, vm_ref, o_ref):
    n = pl.program_id(1)
    kvar = _head_variants(jnp.concatenate([kp_ref[...], kc_ref[...], km_ref[...]], axis=0))
    vvar = _head_variants(jnp.concatenate([vp_ref[...], vc_ref[...], vm_ref[...]], axis=0))

    row = lax.broadcasted_iota(jnp.int32, (2 * BLOCK, N_KEYS), 0)
    col = lax.broadcasted_iota(jnp.int32, (2 * BLOCK, N_KEYS), 1)
    qi = row % BLOCK
    d = BLOCK + qi - col
    j = (n - 1) * BLOCK + col
    is_band = col < 2 * BLOCK
    band_ok = is_band & (d >= 0) & (d <= WINDOW) & (j >= N_META)
    meta_ok = (~is_band) & ((col - 2 * BLOCK) <= n * BLOCK + qi)
    mask = band_ok | meta_ok
    upper = lax.broadcasted_iota(jnp.int32, (2 * BLOCK, 1), 0) < BLOCK

    for g in range(N_KV_HEADS):
        qg = jnp.concatenate([q_ref[:, (2 * g) * LANES:(2 * g + 1) * LANES],
                              q_ref[:, (2 * g + 1) * LANES:(2 * g + 2) * LANES]], axis=0)
        acc = jnp.zeros((2 * BLOCK, LANES), F32)
        for e in range(2):
            s = lax.dot_general(qg, kvar[g][e], (((1,), (1,)), ((), ())),
                                preferred_element_type=F32)
            s = jnp.where(mask, s, -jnp.inf)
            sk = jnp.where(upper, sinks_ref[4 * g + e], sinks_ref[4 * g + 2 + e])
            m = jnp.maximum(jnp.max(s, -1, keepdims=True), sk)
            p = jnp.exp(s - m)
            den = jnp.sum(p, -1, keepdims=True) + jnp.exp(sk - m)
            pn = (p * (1.0 / den)).astype(BF)
            acc = acc + jnp.dot(pn, vvar[g][e], preferred_element_type=F32)
        o_ref[:, (2 * g) * LANES:(2 * g + 1) * LANES] = acc[:BLOCK].astype(BF)
        o_ref[:, (2 * g + 1) * LANES:(2 * g + 2) * LANES] = acc[BLOCK:].astype(BF)


def _attn_call(sinks, q, k, v):
    cur = lambda b, n: (b * N_BLK + n, 0)
    prev = lambda b, n: (b * N_BLK + jnp.maximum(n - 1, 0), 0)
    meta = lambda b, n: (b * (LP // N_META), 0)
    return pl.pallas_call(
        _attn_kernel,
        grid=(BATCH, N_BLK),
        in_specs=[pl.BlockSpec(memory_space=pltpu.SMEM),
                  pl.BlockSpec((BLOCK, ATTN_WIDTH), cur),
                  pl.BlockSpec((BLOCK, KV_WIDTH), prev),
                  pl.BlockSpec((BLOCK, KV_WIDTH), cur),
                  pl.BlockSpec((N_META, KV_WIDTH), meta),
                  pl.BlockSpec((BLOCK, KV_WIDTH), prev),
                  pl.BlockSpec((BLOCK, KV_WIDTH), cur),
                  pl.BlockSpec((N_META, KV_WIDTH), meta)],
        out_specs=pl.BlockSpec((BLOCK, ATTN_WIDTH), cur),
        out_shape=jax.ShapeDtypeStruct((NP, ATTN_WIDTH), BF),
        compiler_params=_cparams(2),
        name="prompt_attn",
    )(sinks, q, k, k, k, v, v, v)


def _sattn_kernel(sk_ref, q_ref, kl_ref, km_ref, vl_ref, vm_ref, o_ref):
    q = q_ref[...]
    nt = (((2,), (2,)), ((0,), (0,)))
    s_loc = lax.dot_general(q, kl_ref[...].astype(BF), nt, preferred_element_type=F32)
    s_meta = lax.dot_general(q, km_ref[...].astype(BF), nt, preferred_element_type=F32)
    rows = 2 * DEC_SEQ * 4
    r = lax.broadcasted_iota(jnp.int32, (1, rows, W_LOC), 1)
    idx = lax.broadcasted_iota(jnp.int32, (1, rows, W_LOC), 2)
    sq = (r % (DEC_SEQ * 4)) // 4
    ok = (idx >= sq) & (idx <= WINDOW + sq)
    s_loc = jnp.where(ok, s_loc, -jnp.inf)
    sk = sk_ref[...][None, :, :1]
    m = jnp.maximum(jnp.maximum(jnp.max(s_loc, -1, keepdims=True),
                                jnp.max(s_meta, -1, keepdims=True)), sk)
    p_loc = jnp.exp(s_loc - m)
    p_meta = jnp.exp(s_meta - m)
    den = (jnp.sum(p_loc, -1, keepdims=True) + jnp.sum(p_meta, -1, keepdims=True)
           + jnp.exp(sk - m))
    inv = 1.0 / den
    nn = (((2,), (1,)), ((0,), (0,)))
    o_ref[...] = (
        lax.dot_general((p_loc * inv).astype(BF), vl_ref[...].astype(BF), nn,
                        preferred_element_type=F32)
        + lax.dot_general((p_meta * inv).astype(BF), vm_ref[...].astype(BF), nn,
                          preferred_element_type=F32))


def _sattn_call(sink_rows, qz, k_loc, mk, v_loc, mv):
    rows = 2 * DEC_SEQ * 4
    b3 = lambda i: (i, 0, 0)
    return pl.pallas_call(
        _sattn_kernel,
        grid=(DEC_BATCH // SAMPLE_BT,),
        in_specs=[pl.BlockSpec((rows, LANES), lambda i: (0, 0)),
                  pl.BlockSpec((SAMPLE_BT, rows, LANES), b3),
                  pl.BlockSpec((SAMPLE_BT, W_LOC, KV_WIDTH), b3),
                  pl.BlockSpec((SAMPLE_BT, N_META, KV_WIDTH), b3),
                  pl.BlockSpec((SAMPLE_BT, W_LOC, KV_WIDTH), b3),
                  pl.BlockSpec((SAMPLE_BT, N_META, KV_WIDTH), b3)],
        out_specs=pl.BlockSpec((SAMPLE_BT, rows, LANES), b3),
        out_shape=jax.ShapeDtypeStruct((DEC_BATCH, rows, LANES), F32),
        compiler_params=_cparams(1),
        name="sample_attn",
    )(sink_rows, qz, k_loc, mk, v_loc, mv)


def _conv_tail(y, g, b):
    return _silu(_ln_rows(y, g, b))


def _pconv_kernel(prev_ref, cur_ref, w_ref, bdw_ref, g_ref, b_ref, o_ref, ext_ref):
    t = pl.program_id(1)
    ext_ref[:CONV_HALO] = jnp.where(t > 0, prev_ref[...], 0.0)
    ext_ref[CONV_HALO:] = cur_ref[...]
    off = CONV_HALO - CONV_STATE
    for r0 in range(0, CONV_TILE, CONV_CHUNK):
        acc = jnp.broadcast_to(bdw_ref[...], (CONV_CHUNK, CONV_CH))
        for j in range(CONV_WIDTH):
            acc = acc + ext_ref[r0 + off + j:r0 + off + j + CONV_CHUNK, :] * w_ref[j:j + 1, :]
        o_ref[r0:r0 + CONV_CHUNK, :] = _conv_tail(acc, g_ref[...], b_ref[...]).astype(BF)


def _pconv_call(u, w_dw, b_dw, g, b):
    tiles = LP // CONV_TILE
    per_tile = CONV_TILE // CONV_HALO
    cur = lambda bb, t: (bb * tiles + t, 0)
    prev = lambda bb, t: (bb * tiles * per_tile + jnp.maximum(t * per_tile - 1, 0), 0)
    fixed = lambda bb, t: (0, 0)
    return pl.pallas_call(
        _pconv_kernel,
        grid=(BATCH, tiles),
        in_specs=[pl.BlockSpec((CONV_HALO, CONV_CH), prev),
                  pl.BlockSpec((CONV_TILE, CONV_CH), cur),
                  pl.BlockSpec((CONV_WIDTH, CONV_CH), fixed),
                  pl.BlockSpec((1, CONV_CH), fixed),
                  pl.BlockSpec((1, CONV_CH), fixed),
                  pl.BlockSpec((1, CONV_CH), fixed)],
        out_specs=pl.BlockSpec((CONV_TILE, CONV_CH), cur),
        out_shape=jax.ShapeDtypeStruct((NP, CONV_CH), BF),
        scratch_shapes=[pltpu.VMEM((CONV_HALO + CONV_TILE, CONV_CH), F32)],
        compiler_params=_cparams(2),
        name="prompt_conv",
    )(u, u, w_dw, b_dw.reshape(1, CONV_CH), g.reshape(1, CONV_CH), b.reshape(1, CONV_CH))


def _sconv_kernel(st_ref, u_ref, w_ref, bdw_ref, g_ref, b_ref, o_ref, y_ref):
    l = pl.program_id(0)

    def ext_row(r):
        if r < CONV_STATE:
            return st_ref[pl.ds(r, DEC_BATCH, stride=CONV_STATE), :]
        return u_ref[pl.ds(r - CONV_STATE, DEC_BATCH, stride=DEC_SEQ), :]

    rows = [ext_row(r) for r in range(CONV_STATE + DEC_SEQ)]
    for s in range(DEC_SEQ):
        acc = jnp.broadcast_to(bdw_ref[...], (DEC_BATCH, LANES))
        for j in range(CONV_WIDTH):
            acc = acc + rows[s + j] * w_ref[j:j + 1, :]
        y_ref[l, s] = acc

    @pl.when(l == CONV_CH // LANES - 1)
    def _():
        for s in range(DEC_SEQ):
            y = jnp.concatenate([y_ref[c, s] for c in range(CONV_CH // LANES)], axis=-1)
            o_ref[s] = _conv_tail(y, g_ref[...], b_ref[...]).astype(BF)


def _sconv_call(state, u2, w_dw, b_dw, g, b):
    grp = lambda rows: pl.BlockSpec((rows, LANES), lambda l: (0, l))
    full = lambda shape: pl.BlockSpec(shape, lambda l: (0,) * len(shape))
    return pl.pallas_call(
        _sconv_kernel,
        grid=(CONV_CH // LANES,),
        in_specs=[grp(DEC_BATCH * CONV_STATE), grp(NS), grp(CONV_WIDTH), grp(1),
                  full((1, CONV_CH)), full((1, CONV_CH))],
        out_specs=full((DEC_SEQ, DEC_BATCH, CONV_CH)),
        out_shape=jax.ShapeDtypeStruct((DEC_SEQ, DEC_BATCH, CONV_CH), BF),
        scratch_shapes=[pltpu.VMEM((CONV_CH // LANES, DEC_SEQ, DEC_BATCH, LANES), F32)],
        compiler_params=_cparams(1),
        name="sample_conv",
    )(state, u2, w_dw, b_dw.reshape(1, CONV_CH), g.reshape(1, CONV_CH), b.reshape(1, CONV_CH))


def _mix(a_ref, c_ref, h_ref, wo_ref, g1_ref, b1_ref):
    mix = (jnp.dot(a_ref[...], wo_ref[:ATTN_WIDTH, :], preferred_element_type=F32)
           + jnp.dot(c_ref[...], wo_ref[ATTN_WIDTH:, :], preferred_element_type=F32))
    return _ln_rows(ALPHA * h_ref[...] + mix, g1_ref[...], b1_ref[...])


def _dense_kernel(a_ref, c_ref, h_ref, wo_ref, g1_ref, b1_ref, wg_ref, wu_ref, wd_ref,
                  g2_ref, b2_ref, o_ref):
    h1 = _mix(a_ref, c_ref, h_ref, wo_ref, g1_ref, b1_ref)
    hb = h1.astype(BF)
    gate = jnp.dot(hb, wg_ref[...], preferred_element_type=F32)
    up = jnp.dot(hb, wu_ref[...], preferred_element_type=F32)
    f = jnp.dot((_silu(gate) * up).astype(BF), wd_ref[...], preferred_element_type=F32)
    o_ref[...] = _ln_rows(ALPHA * h1 + f, g2_ref[...], b2_ref[...])


def _dense_call(a, c, h, wo, g1, b1, wg, wu, wd, g2, b2):
    row = lambda i: (i, 0)
    fixed = lambda i: (0, 0)
    vec = pl.BlockSpec((1, D_MODEL), fixed)
    once = dict(pipeline_mode=pl.Buffered(1))
    return pl.pallas_call(
        _dense_kernel,
        grid=(T_PAD // TM,),
        in_specs=[pl.BlockSpec((TM, ATTN_WIDTH), row),
                  pl.BlockSpec((TM, CONV_CH), row),
                  pl.BlockSpec((TM, D_MODEL), row),
                  pl.BlockSpec((D_MODEL, D_MODEL), fixed, **once),
                  vec, vec,
                  pl.BlockSpec((D_MODEL, D_FF_DENSE), fixed, **once),
                  pl.BlockSpec((D_MODEL, D_FF_DENSE), fixed, **once),
                  pl.BlockSpec((D_FF_DENSE, D_MODEL), fixed, **once),
                  vec, vec],
        out_specs=pl.BlockSpec((TM, D_MODEL), row),
        out_shape=jax.ShapeDtypeStruct((T_PAD, D_MODEL), F32),
        compiler_params=_cparams(1),
        name="mix_dense_ffn",
    )(a, c, h, wo, g1.reshape(1, -1), b1.reshape(1, -1), wg, wu, wd,
      g2.reshape(1, -1), b2.reshape(1, -1))


def _mixln_kernel(a_ref, c_ref, h_ref, wo_ref, g1_ref, b1_ref, o_ref):
    o_ref[...] = _mix(a_ref, c_ref, h_ref, wo_ref, g1_ref, b1_ref)


def _mixln_call(a, c, h, wo, g1, b1):
    row = lambda i: (i, 0)
    fixed = lambda i: (0, 0)
    vec = pl.BlockSpec((1, D_MODEL), fixed)
    return pl.pallas_call(
        _mixln_kernel,
        grid=(T_PAD // TM,),
        in_specs=[pl.BlockSpec((TM, ATTN_WIDTH), row),
                  pl.BlockSpec((TM, CONV_CH), row),
                  pl.BlockSpec((TM, D_MODEL), row),
                  pl.BlockSpec((D_MODEL, D_MODEL), fixed),
                  vec, vec],
        out_specs=pl.BlockSpec((TM, D_MODEL), row),
        out_shape=jax.ShapeDtypeStruct((T_PAD, D_MODEL), F32),
        compiler_params=_cparams(1),
        name="mix_ln",
    )(a, c, h, wo, g1.reshape(1, -1), b1.reshape(1, -1))


def _moe_kernel(x_ref, rw_ref, rb_ref, eg_ref, eu_ref, ed_ref, g2_ref, b2_ref, o_ref,
                comb_ref, acc_ref):
    e = pl.program_id(1)
    lane = lax.broadcasted_iota(jnp.int32, (TM_MOE, LANES), 1)

    @pl.when(e == 0)
    def _():
        logits = jnp.dot(x_ref[...], rw_ref[...], preferred_element_type=F32,
                         precision=lax.Precision.HIGHEST) + rb_ref[...]
        logits = jnp.where(lane < N_EXPERTS, logits, -jnp.inf)
        v1 = jnp.max(logits, -1, keepdims=True)
        i1 = jnp.min(jnp.where(logits == v1, lane, LANES), -1, keepdims=True)
        rest = jnp.where(lane == i1, -jnp.inf, logits)
        v2 = jnp.max(rest, -1, keepdims=True)
        i2 = jnp.min(jnp.where(rest == v2, lane, LANES), -1, keepdims=True)
        e2 = jnp.exp(v2 - v1)
        den = 1.0 + e2
        comb_ref[...] = jnp.where(lane == i1, 1.0 / den, jnp.where(lane == i2, e2 / den, 0.0))
        acc_ref[...] = jnp.zeros_like(acc_ref)

    xb = x_ref[...].astype(BF)
    gate = jnp.dot(xb, eg_ref[0], preferred_element_type=F32)
    up = jnp.dot(xb, eu_ref[0], preferred_element_type=F32)
    w = jnp.sum(jnp.where(lane == e, comb_ref[...], 0.0), -1, keepdims=True)
    hid = (_silu(gate) * up * w).astype(BF)
    acc_ref[...] += jnp.dot(hid, ed_ref[0], preferred_element_type=F32)

    @pl.when(e == N_EXPERTS - 1)
    def _():
        o_ref[...] = _ln_rows(ALPHA * x_ref[...] + acc_ref[...], g2_ref[...], b2_ref[...])


def _moe_call(x, rw, rb, eg, eu, ed, g2, b2):
    row = lambda i, e: (i, 0)
    fixed = lambda i, e: (0, 0)
    exp = lambda i, e: (e, 0, 0)
    vec = pl.BlockSpec((1, D_MODEL), fixed)
    return pl.pallas_call(
        _moe_kernel,
        grid=(T_PAD // TM_MOE, N_EXPERTS),
        in_specs=[pl.BlockSpec((TM_MOE, D_MODEL), row),
                  pl.BlockSpec((D_MODEL, LANES), fixed),
                  pl.BlockSpec((1, LANES), fixed),
                  pl.BlockSpec((1, D_MODEL, D_FF_EXPERT), exp),
                  pl.BlockSpec((1, D_MODEL, D_FF_EXPERT), exp),
                  pl.BlockSpec((1, D_FF_EXPERT, D_MODEL), exp),
                  vec, vec],
        out_specs=pl.BlockSpec((TM_MOE, D_MODEL), row),
        out_shape=jax.ShapeDtypeStruct((T_PAD, D_MODEL), F32),
        scratch_shapes=[pltpu.VMEM((TM_MOE, LANES), F32),
                        pltpu.VMEM((TM_MOE, D_MODEL), F32)],
        compiler_params=_cparams(2),
        name="moe_ffn",
    )(x, rw, rb, eg, eu, ed, g2.reshape(1, -1), b2.reshape(1, -1))


def _rope_tables():
    half = HEAD_DIM // 2
    pos_p = jnp.tile(jnp.arange(LP), BATCH)
    pos_s = PAST_LEN + jnp.tile(jnp.arange(DEC_SEQ), DEC_BATCH)
    pos = jnp.concatenate([pos_p, pos_s, jnp.zeros((T_PAD - NP - NS,), pos_p.dtype)])
    inv = ROPE_THETA ** (-jnp.arange(half, dtype=F32) / half)
    ang = pos.astype(F32)[:, None] * inv[None, :]
    cos = jnp.cos(ang)
    sin = jnp.sin(ang)
    return jnp.tile(cos, (1, 4)), jnp.tile(jnp.concatenate([-sin, sin], -1), (1, 2))


def kernel(x_prompt, x_sample, cache_meta_k, cache_meta_v, cache_k, cache_v, state_conv,
           meta_tokens, ln_in_g, ln_in_b, w_in, w_dw, b_dw, conv_ln_g, conv_ln_b, sinks, w_out,
           ln1_g, ln1_b, ln2_g, ln2_b, ffd_w_gate, ffd_w_up, ffd_w_down,
           router_w, router_b, exp_w_gate, exp_w_up, exp_w_down):
    meta = jnp.broadcast_to(meta_tokens[None], (BATCH, N_META, D_MODEL))
    xp = jnp.concatenate([meta, x_prompt, jnp.zeros((BATCH, LP - L_REAL, D_MODEL), F32)], 1)
    x = jnp.concatenate([xp.reshape(NP, D_MODEL), x_sample.reshape(NS, D_MODEL),
                         jnp.zeros((T_PAD - NP - NS, D_MODEL), F32)], 0)
    cos_t, sin_t = _rope_tables()
    h = _ln_call(x, ln_in_g, ln_in_b)

    head_of_row = (jnp.arange(32) // 16) * 4 + jnp.arange(32) % 4
    tail = jnp.zeros((T_PAD - NP - NS, ATTN_WIDTH), BF)

    pmk, pmv, pk, pv, pc, sk, sv, sc = [], [], [], [], [], [], [], []
    for i in range(DEPTH):
        q, k, v, u = _inproj_call(h, w_in[i].astype(BF), cos_t, sin_t)

        a_p = _attn_call(sinks[i], q, k, v)
        c_p = _pconv_call(u, w_dw[i], b_dw[i], conv_ln_g[i], conv_ln_b[i])
        kp = k[:NP].reshape(BATCH, LP, N_KV_HEADS, HEAD_DIM)
        vp = v[:NP].reshape(BATCH, LP, N_KV_HEADS, HEAD_DIM)
        pmk.append(kp[:, :N_META]); pmv.append(vp[:, :N_META])
        pk.append(kp[:, L_REAL - WINDOW:L_REAL]); pv.append(vp[:, L_REAL - WINDOW:L_REAL])
        pc.append(u[:NP].reshape(BATCH, LP, CONV_CH)[:, L_REAL - CONV_STATE:L_REAL])

        k2 = k[NP:NP + NS].reshape(DEC_BATCH, DEC_SEQ, KV_WIDTH)
        v2 = v[NP:NP + NS].reshape(DEC_BATCH, DEC_SEQ, KV_WIDTH)
        u2 = u[NP:NP + NS]
        k_loc = jnp.concatenate([cache_k[i].reshape(DEC_BATCH, WINDOW, KV_WIDTH), k2], 1)
        v_loc = jnp.concatenate([cache_v[i].reshape(DEC_BATCH, WINDOW, KV_WIDTH), v2], 1)
        q2 = q[NP:NP + NS].reshape(DEC_BATCH, DEC_SEQ, N_KV_HEADS, 4, HEAD_DIM)
        q2 = q2.transpose(0, 2, 1, 3, 4).reshape(DEC_BATCH, N_KV_HEADS, 16, HEAD_DIM)
        zq = jnp.zeros_like(q2[:, 0])
        qz = jnp.concatenate([jnp.concatenate([q2[:, 0], zq], -1),
                              jnp.concatenate([zq, q2[:, 1]], -1)], 1)
        sink_rows = jnp.broadcast_to(sinks[i][head_of_row][:, None], (32, LANES))
        o2 = _sattn_call(sink_rows, qz, k_loc,
                         cache_meta_k[i].reshape(DEC_BATCH, N_META, KV_WIDTH), v_loc,
                         cache_meta_v[i].reshape(DEC_BATCH, N_META, KV_WIDTH))
        o2 = jnp.stack([o2[:, :16, :HEAD_DIM], o2[:, 16:, HEAD_DIM:]], 1)
        a_s = o2.reshape(DEC_BATCH, N_KV_HEADS, DEC_SEQ, 4, HEAD_DIM).transpose(0, 2, 1, 3, 4)
        a_s = a_s.reshape(NS, ATTN_WIDTH).astype(BF)
        c_s = _sconv_call(state_conv[i].reshape(DEC_BATCH * CONV_STATE, CONV_CH), u2,
                          w_dw[i], b_dw[i], conv_ln_g[i], conv_ln_b[i])
        c_s = c_s.transpose(1, 0, 2).reshape(NS, CONV_CH)
        sk.append(k_loc[:, DEC_SEQ:].reshape(DEC_BATCH, WINDOW, N_KV_HEADS, HEAD_DIM))
        sv.append(v_loc[:, DEC_SEQ:].reshape(DEC_BATCH, WINDOW, N_KV_HEADS, HEAD_DIM))
        sc.append(jnp.concatenate([state_conv[i][:, DEC_SEQ:],
                                   u2.reshape(DEC_BATCH, DEC_SEQ, CONV_CH)], 1))

        a = jnp.concatenate([a_p, a_s, tail], 0)
        c = jnp.concatenate([c_p, c_s, tail], 0)
        wo = w_out[i].astype(BF)
        j = i // 2
        if i % 2 == 0:
            h = _dense_call(a, c, h, wo, ln1_g[i], ln1_b[i], ffd_w_gate[j].astype(BF),
                            ffd_w_up[j].astype(BF), ffd_w_down[j].astype(BF), ln2_g[i], ln2_b[i])
        else:
            h1 = _mixln_call(a, c, h, wo, ln1_g[i], ln1_b[i])
            rw = jnp.pad(router_w[j], ((0, 0), (0, LANES - N_EXPERTS)))
            rb = jnp.pad(router_b[j], (0, LANES - N_EXPERTS)).reshape(1, LANES)
            h = _moe_call(h1, rw, rb, exp_w_gate[j].astype(BF), exp_w_up[j].astype(BF),
                          exp_w_down[j].astype(BF), ln2_g[i], ln2_b[i])

    y_prompt = h[:NP].reshape(BATCH, LP, D_MODEL)[:, N_META:L_REAL]
    y_sample = h[NP:NP + NS].reshape(DEC_BATCH, DEC_SEQ, D_MODEL)
    return (y_prompt, y_sample, jnp.stack(pmk), jnp.stack(pmv), jnp.stack(pk), jnp.stack(pv),
            jnp.stack(pc), jnp.stack(sk), jnp.stack(sv), jnp.stack(sc))
```

```python
import functools

import numpy as np
import jax
import jax.numpy as jnp
from jax import lax
from jax.experimental import pallas as pl
from jax.experimental.pallas import tpu as pltpu

D_MODEL = 1024
BATCH = 2
SEQ = 8192
DEPTH = 4
DEC_BATCH = 128
DEC_SEQ = 4
PAST_LEN = 8192
N_META = 16
HEAD_DIM = 64
N_Q_HEADS = 8
N_KV_HEADS = 2
ATTN_WIDTH = N_Q_HEADS * HEAD_DIM
KV_WIDTH = N_KV_HEADS * HEAD_DIM
CONV_CH = D_MODEL - ATTN_WIDTH
Q_END = ATTN_WIDTH
K_END = Q_END + KV_WIDTH
V_END = K_END + KV_WIDTH
IN_COLS = V_END + 2 * CONV_CH
CONV_WIDTH = 31
CONV_STATE = CONV_WIDTH - 1
WINDOW = 128
BLOCK = 128
ROPE_THETA = 10000.0
ATTN_SCALE = HEAD_DIM ** -0.5
D_FF_DENSE = 2816
N_EXPERTS = 8
D_FF_EXPERT = 1024
ALPHA = (2 * DEPTH) ** 0.25
LN_EPS = 1e-5

LANES = 128
SUBLANES = 8
TOK = BATCH * SEQ
MET0 = TOK
SMP0 = MET0 + BATCH * BLOCK
NS = DEC_BATCH * DEC_SEQ
PAD0 = SMP0 + NS
TM = 512
TM_MOE = 1024
T = -(-PAD0 // TM_MOE) * TM_MOE
TAIL = T - TOK
ATT_Q = 2 * BLOCK
ATT_STEPS = SEQ // ATT_Q
W_LOC = WINDOW + DEC_SEQ
N_KEYS = 2 * BLOCK + N_META
CONV_TILE = 512
CONV_HALO = 32
CONV_CHUNK = 64
SAMPLE_BT = 16
MOE_CHUNK = 128
VMEM_LIMIT = 56 * 1024 * 1024

BF = jnp.bfloat16
F32 = jnp.float32


def _cparams(n_axes):
    return pltpu.CompilerParams(dimension_semantics=("arbitrary",) * n_axes,
                                vmem_limit_bytes=VMEM_LIMIT)


def _ln_rows(x, g, b):
    mu = jnp.mean(x, -1, keepdims=True)
    xc = x - mu
    var = jnp.mean(xc * xc, -1, keepdims=True)
    return xc * lax.rsqrt(var + LN_EPS) * g + b


def _silu(x):
    return x * jax.nn.sigmoid(x)


def _vec(n):
    return pl.BlockSpec((1, n), lambda *_: (0, 0))


def _ln_kernel(xt_ref, xr_ref, g_ref, b_ref, o_ref):
    i = pl.program_id(0)

    @pl.when(i < TOK // TM)
    def _():
        o_ref[...] = _ln_rows(xt_ref[...], g_ref[...], b_ref[...])

    @pl.when(i >= TOK // TM)
    def _():
        o_ref[...] = _ln_rows(xr_ref[...], g_ref[...], b_ref[...])


def _ln_call(x_tok, x_tail, g, b):
    n_tok = TOK // TM
    return pl.pallas_call(
        _ln_kernel,
        grid=(T // TM,),
        in_specs=[pl.BlockSpec((TM, D_MODEL), lambda i: (jnp.minimum(i, n_tok - 1), 0)),
                  pl.BlockSpec((TM, D_MODEL), lambda i: (jnp.maximum(i - n_tok, 0), 0)),
                  _vec(D_MODEL), _vec(D_MODEL)],
        out_specs=pl.BlockSpec((TM, D_MODEL), lambda i: (i, 0)),
        out_shape=jax.ShapeDtypeStruct((T, D_MODEL), F32),
        compiler_params=_cparams(1),
        name="ln_in",
    )(x_tok, x_tail, g.reshape(1, D_MODEL), b.reshape(1, D_MODEL))


def _inproj_kernel(h_ref, w_ref, cos_ref, sin_ref, q_ref, k_ref, v_ref, u_ref):
    hb = h_ref[...].astype(BF)
    cos = cos_ref[...]
    sin = sin_ref[...]
    lane = lax.broadcasted_iota(jnp.int32, (TM, LANES), 1)
    first_half = (lane % HEAD_DIM) < (HEAD_DIM // 2)

    def rope(x):
        rot = jnp.where(first_half, pltpu.roll(x, LANES - HEAD_DIM // 2, 1),
                        pltpu.roll(x, HEAD_DIM // 2, 1))
        return x * cos + rot * sin

    qk = jnp.dot(hb, w_ref[:, :K_END], preferred_element_type=F32)
    for c in range(Q_END // LANES):
        q_ref[:, c * LANES:(c + 1) * LANES] = (
            rope(qk[:, c * LANES:(c + 1) * LANES]) * ATTN_SCALE).astype(BF)
    k_ref[...] = rope(qk[:, Q_END:K_END])
    v_ref[...] = jnp.dot(hb, w_ref[:, K_END:V_END], preferred_element_type=F32)
    val = jnp.dot(hb, w_ref[:, V_END:V_END + CONV_CH], preferred_element_type=F32)
    gate = jnp.dot(hb, w_ref[:, V_END + CONV_CH:], preferred_element_type=F32)
    u_ref[...] = val * jax.nn.sigmoid(gate)


def _rope_tables():
    half = HEAD_DIM // 2
    pos = np.zeros((SEQ + TAIL,), np.float64)
    pos[:SEQ] = N_META + np.arange(SEQ)
    for b in range(BATCH):
        pos[SEQ + b * BLOCK:SEQ + b * BLOCK + N_META] = np.arange(N_META)
    s0 = SEQ + SMP0 - MET0
    pos[s0:s0 + NS] = PAST_LEN + np.tile(np.arange(DEC_SEQ), DEC_BATCH)
    inv = ROPE_THETA ** (-np.arange(half, dtype=np.float64) / half)
    ang = pos[:, None] * inv[None, :]
    cos = np.cos(ang).astype(np.float32)
    sin = np.sin(ang).astype(np.float32)
    return np.tile(cos, (1, 4)), np.tile(np.concatenate([-sin, sin], -1), (1, 2))


def _inproj_call(h, w_bf, cos_t, sin_t):
    row = lambda i: (i, 0)
    per_batch = SEQ // TM
    tab = lambda i: (jnp.where(i < 2 * per_batch, i % per_batch, i - per_batch), 0)
    return pl.pallas_call(
        _inproj_kernel,
        grid=(T // TM,),
        in_specs=[pl.BlockSpec((TM, D_MODEL), row),
                  pl.BlockSpec((D_MODEL, IN_COLS), lambda i: (0, 0)),
                  pl.BlockSpec((TM, LANES), tab),
                  pl.BlockSpec((TM, LANES), tab)],
        out_specs=[pl.BlockSpec((TM, ATTN_WIDTH), row),
                   pl.BlockSpec((TM, KV_WIDTH), row),
                   pl.BlockSpec((TM, KV_WIDTH), row),
                   pl.BlockSpec((TM, CONV_CH), row)],
        out_shape=[jax.ShapeDtypeStruct((T, ATTN_WIDTH), BF),
                   jax.ShapeDtypeStruct((T, KV_WIDTH), F32),
                   jax.ShapeDtypeStruct((T, KV_WIDTH), F32),
                   jax.ShapeDtypeStruct((T, CONV_CH), F32)],
        compiler_params=_cparams(1),
        name="inproj",
    )(h, w_bf, cos_t, sin_t)


def _head_variants(x):
    lane = lax.broadcasted_iota(jnp.int32, x.shape, x.ndim - 1)
    lo = lane < HEAD_DIM
    xr = pltpu.roll(x, HEAD_DIM, x.ndim - 1)
    zero = jnp.zeros_like(x)
    return [[jnp.where(lo, x, zero).astype(BF), jnp.where(lo, zero, xr).astype(BF)],
            [jnp.where(lo, xr, zero).astype(BF), jnp.where(lo, zero, x).astype(BF)]]


def _attn_bias():
    qi = np.arange(2 * BLOCK)[:, None] % BLOCK
    col = np.arange(N_KEYS)[None, :]
    d = BLOCK + qi - col
    band = (col < 2 * BLOCK) & (d >= 0) & (d <= WINDOW)
    is_meta = col >= 2 * BLOCK
    normal = band | is_meta
    first = (band & (col >= BLOCK)) | is_meta
    meta_q = is_meta & ((col - 2 * BLOCK) <= qi)
    return np.where(np.stack([normal, first, meta_q]), 0.0, -np.inf).astype(np.float32)


def _attn_kernel(sinks_ref, bias_ref, q_ref, kp_ref, kc_ref, km0_ref, km1_ref,
                 vp_ref, vc_ref, vm0_ref, vm1_ref, o_ref):
    s_id = pl.program_id(0)
    n_tok = BATCH * ATT_STEPS

    @pl.when(s_id > n_tok)
    def _():
        o_ref[...] = jnp.zeros_like(o_ref)

    @pl.when(s_id <= n_tok)
    def _():
        is_meta = s_id == n_tok
        first = (s_id % ATT_STEPS) == 0
        variant = [jnp.where(is_meta, 2, jnp.where(first, 1, 0)), jnp.where(is_meta, 2, 0)]
        kc = kc_ref[...]
        vc = vc_ref[...]
        keys = [jnp.concatenate([kp_ref[...], kc[:BLOCK], km0_ref[...]], axis=0),
                jnp.concatenate([kc[:BLOCK], kc[BLOCK:], km1_ref[...]], axis=0)]
        vals = [jnp.concatenate([vp_ref[...], vc[:BLOCK], vm0_ref[...]], axis=0),
                jnp.concatenate([vc[:BLOCK], vc[BLOCK:], vm1_ref[...]], axis=0)]
        upper = lax.broadcasted_iota(jnp.int32, (2 * BLOCK, 1), 0) < BLOCK

        for h in range(2):
            rows = slice(h * BLOCK, (h + 1) * BLOCK)
            kvar = _head_variants(keys[h])
            vvar = _head_variants(vals[h])
            bias = bias_ref[variant[h]]
            for g in range(N_KV_HEADS):
                qg = jnp.concatenate([q_ref[rows, (2 * g) * LANES:(2 * g + 1) * LANES],
                                      q_ref[rows, (2 * g + 1) * LANES:(2 * g + 2) * LANES]],
                                     axis=0)
                acc = jnp.zeros((2 * BLOCK, LANES), F32)
                for e in range(2):
                    s = lax.dot_general(qg, kvar[g][e], (((1,), (1,)), ((), ())),
                                        preferred_element_type=F32) + bias
                    sk = jnp.where(upper, sinks_ref[4 * g + e], sinks_ref[4 * g + 2 + e])
                    m = jnp.maximum(jnp.max(s, -1, keepdims=True), sk)
                    p = jnp.exp(s - m)
                    den = jnp.sum(p, -1, keepdims=True) + jnp.exp(sk - m)
                    pn = (p * (1.0 / den)).astype(BF)
                    acc = acc + jnp.dot(pn, vvar[g][e], preferred_element_type=F32)
                o_ref[rows, (2 * g) * LANES:(2 * g + 1) * LANES] = acc[:BLOCK].astype(BF)
                o_ref[rows, (2 * g + 1) * LANES:(2 * g + 2) * LANES] = acc[BLOCK:].astype(BF)


def _attn_call(sinks, bias, q, k, v):
    n_tok = BATCH * ATT_STEPS
    cur = lambda s: (jnp.minimum(s, n_tok), 0)

    def prev(s):
        return (jnp.minimum(jnp.where(s % ATT_STEPS == 0, 2 * s, 2 * s - 1), 2 * n_tok), 0)

    def meta(h):
        def index(s):
            b = jnp.where(s < n_tok, s // ATT_STEPS, jnp.where(s == n_tok, h, 0))
            return ((MET0 + b * BLOCK) // N_META, 0)
        return index

    return pl.pallas_call(
        _attn_kernel,
        grid=(T // ATT_Q,),
        in_specs=[pl.BlockSpec(memory_space=pltpu.SMEM),
                  pl.BlockSpec((3, ATT_Q, N_KEYS), lambda s: (0, 0, 0)),
                  pl.BlockSpec((ATT_Q, ATTN_WIDTH), cur),
                  pl.BlockSpec((BLOCK, KV_WIDTH), prev),
                  pl.BlockSpec((ATT_Q, KV_WIDTH), cur),
                  pl.BlockSpec((N_META, KV_WIDTH), meta(0)),
                  pl.BlockSpec((N_META, KV_WIDTH), meta(1)),
                  pl.BlockSpec((BLOCK, KV_WIDTH), prev),
                  pl.BlockSpec((ATT_Q, KV_WIDTH), cur),
                  pl.BlockSpec((N_META, KV_WIDTH), meta(0)),
                  pl.BlockSpec((N_META, KV_WIDTH), meta(1))],
        out_specs=pl.BlockSpec((ATT_Q, ATTN_WIDTH), lambda s: (s, 0)),
        out_shape=jax.ShapeDtypeStruct((T, ATTN_WIDTH), BF),
        compiler_params=_cparams(1),
        name="prompt_attn",
    )(sinks, bias, q, k, k, k, k, v, v, v, v)


SAMPLE_ROWS = N_KV_HEADS * DEC_SEQ * (N_Q_HEADS // N_KV_HEADS)


def _sattn_kernel(sk_ref, q_ref, kl_ref, km_ref, vl_ref, vm_ref, o_ref):
    q = q_ref[...]
    nt = (((2,), (2,)), ((0,), (0,)))
    s_loc = lax.dot_general(q, kl_ref[...].astype(BF), nt, preferred_element_type=F32)
    s_meta = lax.dot_general(q, km_ref[...].astype(BF), nt, preferred_element_type=F32)
    r = lax.broadcasted_iota(jnp.int32, (1, SAMPLE_ROWS, W_LOC), 1)
    idx = lax.broadcasted_iota(jnp.int32, (1, SAMPLE_ROWS, W_LOC), 2)
    sq = (r % (SAMPLE_ROWS // N_KV_HEADS)) // (N_Q_HEADS // N_KV_HEADS)
    ok = (idx >= sq) & (idx <= WINDOW + sq)
    s_loc = jnp.where(ok, s_loc, -jnp.inf)
    sk = sk_ref[...][None, :, :1]
    m = jnp.maximum(jnp.maximum(jnp.max(s_loc, -1, keepdims=True),
                                jnp.max(s_meta, -1, keepdims=True)), sk)
    p_loc = jnp.exp(s_loc - m)
    p_meta = jnp.exp(s_meta - m)
    den = (jnp.sum(p_loc, -1, keepdims=True) + jnp.sum(p_meta, -1, keepdims=True)
           + jnp.exp(sk - m))
    inv = 1.0 / den
    nn = (((2,), (1,)), ((0,), (0,)))
    o_ref[...] = (
        lax.dot_general((p_loc * inv).astype(BF), vl_ref[...].astype(BF), nn,
                        preferred_element_type=F32)
        + lax.dot_general((p_meta * inv).astype(BF), vm_ref[...].astype(BF), nn,
                          preferred_element_type=F32))


def _sattn_call(sink_rows, qz, k_loc, mk, v_loc, mv):
    b3 = lambda i: (i, 0, 0)
    return pl.pallas_call(
        _sattn_kernel,
        grid=(DEC_BATCH // SAMPLE_BT,),
        in_specs=[pl.BlockSpec((SAMPLE_ROWS, LANES), lambda i: (0, 0)),
                  pl.BlockSpec((SAMPLE_BT, SAMPLE_ROWS, LANES), b3),
                  pl.BlockSpec((SAMPLE_BT, W_LOC, KV_WIDTH), b3),
                  pl.BlockSpec((SAMPLE_BT, N_META, KV_WIDTH), b3),
                  pl.BlockSpec((SAMPLE_BT, W_LOC, KV_WIDTH), b3),
                  pl.BlockSpec((SAMPLE_BT, N_META, KV_WIDTH), b3)],
        out_specs=pl.BlockSpec((SAMPLE_BT, SAMPLE_ROWS, LANES), b3),
        out_shape=jax.ShapeDtypeStruct((DEC_BATCH, SAMPLE_ROWS, LANES), F32),
        compiler_params=_cparams(1),
        name="sample_attn",
    )(sink_rows, qz, k_loc, mk, v_loc, mv)


def _conv_tail(y, g, b):
    return _silu(_ln_rows(y, g, b))


def _conv_rows(ext_ref, sh_ref, y_ref, w_ref, bdw_ref, g_ref, b_ref, o_ref, n_rows):
    off = CONV_HALO - CONV_STATE
    span = n_rows + CONV_HALO - SUBLANES
    for s in range(1, SUBLANES):
        sh_ref[s - 1, :span, :] = ext_ref[s:s + span, :]
    for lg in range(CONV_CH // LANES):
        lanes = slice(lg * LANES, (lg + 1) * LANES)
        for r0 in range(0, n_rows, CONV_CHUNK):
            acc = jnp.broadcast_to(bdw_ref[:, lanes], (CONV_CHUNK, LANES))
            for j in range(CONV_WIDTH):
                a, s = divmod(off + j, SUBLANES)
                src = ext_ref if s == 0 else sh_ref.at[s - 1]
                x = src[r0 + a * SUBLANES:r0 + a * SUBLANES + CONV_CHUNK, lanes]
                acc = acc + x * w_ref[j:j + 1, lanes]
            y_ref[r0:r0 + CONV_CHUNK, lanes] = acc
    for r0 in range(0, n_rows, CONV_CHUNK):
        y = y_ref[r0:r0 + CONV_CHUNK, :]
        o_ref[r0:r0 + CONV_CHUNK, :] = _conv_tail(y, g_ref[...], b_ref[...]).astype(BF)


def _conv_scratch(n_rows):
    return [pltpu.VMEM((CONV_HALO + n_rows, CONV_CH), F32),
            pltpu.VMEM((SUBLANES - 1, CONV_HALO + n_rows - SUBLANES, CONV_CH), F32),
            pltpu.VMEM((n_rows, CONV_CH), F32)]


def _pconv_kernel(prev_ref, cur_ref, w_ref, bdw_ref, g_ref, b_ref, o_ref, ext_ref, sh_ref,
                  y_ref):
    t = pl.program_id(1)
    n_tok = SEQ // CONV_TILE

    @pl.when(t == n_tok)
    def _():
        o_ref[...] = jnp.zeros_like(o_ref)

    @pl.when(t < n_tok)
    def _():
        half = CONV_HALO // 2
        ext_ref[:half] = jnp.where(t > 0, prev_ref[:half], 0.0)
        ext_ref[half:CONV_HALO] = jnp.where(t > 0, prev_ref[half:], prev_ref[:half])
        ext_ref[CONV_HALO:] = cur_ref[...]
        _conv_rows(ext_ref, sh_ref, y_ref, w_ref, bdw_ref, g_ref, b_ref, o_ref, CONV_TILE)


def _pconv_call(u, w_dw, b_dw, g, b):
    n_tok = SEQ // CONV_TILE
    per_tile = CONV_TILE // CONV_HALO

    def cur(bb, t):
        return (jnp.where(t < n_tok, bb * n_tok + t, MET0 // CONV_TILE + bb), 0)

    def prev(bb, t):
        tok = (bb * n_tok + jnp.minimum(t, n_tok - 1)) * per_tile - 1
        return (jnp.where(t > 0, tok, (MET0 + bb * BLOCK) // CONV_HALO), 0)

    fixed = lambda bb, t: (0, 0)
    return pl.pallas_call(
        _pconv_kernel,
        grid=(BATCH, n_tok + 1),
        in_specs=[pl.BlockSpec((CONV_HALO, CONV_CH), prev),
                  pl.BlockSpec((CONV_TILE, CONV_CH), cur),
                  pl.BlockSpec((CONV_WIDTH, CONV_CH), fixed),
                  _vec(CONV_CH), _vec(CONV_CH), _vec(CONV_CH)],
        out_specs=pl.BlockSpec((CONV_TILE, CONV_CH), cur),
        out_shape=jax.ShapeDtypeStruct((T, CONV_CH), BF),
        scratch_shapes=_conv_scratch(CONV_TILE),
        compiler_params=_cparams(2),
        name="prompt_conv",
    )(u, u, w_dw, b_dw.reshape(1, CONV_CH), g.reshape(1, CONV_CH), b.reshape(1, CONV_CH))


def _mconv_kernel(cur_ref, w_ref, bdw_ref, g_ref, b_ref, o_ref, ext_ref, sh_ref, y_ref):
    ext_ref[:CONV_HALO] = jnp.zeros((CONV_HALO, CONV_CH), F32)
    ext_ref[CONV_HALO:] = cur_ref[...]
    _conv_rows(ext_ref, sh_ref, y_ref, w_ref, bdw_ref, g_ref, b_ref, o_ref, BLOCK)


def _mconv_call(u, w_dw, b_dw, g, b):
    fixed = lambda bb: (0, 0)
    return pl.pallas_call(
        _mconv_kernel,
        grid=(BATCH,),
        in_specs=[pl.BlockSpec((BLOCK, CONV_CH), lambda bb: (MET0 // BLOCK + bb, 0)),
                  pl.BlockSpec((CONV_WIDTH, CONV_CH), fixed),
                  _vec(CONV_CH), _vec(CONV_CH), _vec(CONV_CH)],
        out_specs=pl.BlockSpec((BLOCK, CONV_CH), lambda bb: (bb, 0)),
        out_shape=jax.ShapeDtypeStruct((BATCH * BLOCK, CONV_CH), BF),
        scratch_shapes=_conv_scratch(BLOCK),
        compiler_params=_cparams(1),
        name="meta_conv",
    )(u, w_dw, b_dw.reshape(1, CONV_CH), g.reshape(1, CONV_CH), b.reshape(1, CONV_CH))


def _sconv_kernel(st_ref, u_ref, w_ref, bdw_ref, g_ref, b_ref, o_ref, y_ref):
    l = pl.program_id(0)

    def ext_row(r):
        if r < CONV_STATE:
            return st_ref[pl.ds(r, DEC_BATCH, stride=CONV_STATE), :]
        return u_ref[pl.ds(r - CONV_STATE, DEC_BATCH, stride=DEC_SEQ), :]

    rows = [ext_row(r) for r in range(CONV_STATE + DEC_SEQ)]
    for s in range(DEC_SEQ):
        acc = jnp.broadcast_to(bdw_ref[...], (DEC_BATCH, LANES))
        for j in range(CONV_WIDTH):
            acc = acc + rows[s + j] * w_ref[j:j + 1, :]
        y_ref[l, s] = acc

    @pl.when(l == CONV_CH // LANES - 1)
    def _():
        for s in range(DEC_SEQ):
            y = jnp.concatenate([y_ref[c, s] for c in range(CONV_CH // LANES)], axis=-1)
            o_ref[s] = _conv_tail(y, g_ref[...], b_ref[...]).astype(BF)


def _sconv_call(state, u, w_dw, b_dw, g, b):
    grp = lambda rows: pl.BlockSpec((rows, LANES), lambda l: (0, l))
    full = lambda shape: pl.BlockSpec(shape, lambda l: (0,) * len(shape))
    return pl.pallas_call(
        _sconv_kernel,
        grid=(CONV_CH // LANES,),
        in_specs=[grp(DEC_BATCH * CONV_STATE), grp(NS), grp(CONV_WIDTH), grp(1),
                  full((1, CONV_CH)), full((1, CONV_CH))],
        out_specs=full((DEC_SEQ, DEC_BATCH, CONV_CH)),
        out_shape=jax.ShapeDtypeStruct((DEC_SEQ, DEC_BATCH, CONV_CH), BF),
        scratch_shapes=[pltpu.VMEM((CONV_CH // LANES, DEC_SEQ, DEC_BATCH, LANES), F32)],
        compiler_params=_cparams(1),
        name="sample_conv",
    )(state, u, w_dw, b_dw.reshape(1, CONV_CH), g.reshape(1, CONV_CH), b.reshape(1, CONV_CH))


def _mix(a_ref, c_ref, h_ref, wo_ref, g1_ref, b1_ref):
    mix = (jnp.dot(a_ref[...], wo_ref[:ATTN_WIDTH, :], preferred_element_type=F32)
           + jnp.dot(c_ref[...], wo_ref[ATTN_WIDTH:, :], preferred_element_type=F32))
    return _ln_rows(ALPHA * h_ref[...] + mix, g1_ref[...], b1_ref[...])


def _dense_kernel(a_ref, c_ref, h_ref, wo_ref, g1_ref, b1_ref, wg_ref, wu_ref, wd_ref,
                  g2_ref, b2_ref, o_ref):
    h1 = _mix(a_ref, c_ref, h_ref, wo_ref, g1_ref, b1_ref)
    hb = h1.astype(BF)
    gate = jnp.dot(hb, wg_ref[...], preferred_element_type=F32)
    up = jnp.dot(hb, wu_ref[...], preferred_element_type=F32)
    f = jnp.dot((_silu(gate) * up).astype(BF), wd_ref[...], preferred_element_type=F32)
    o_ref[...] = _ln_rows(ALPHA * h1 + f, g2_ref[...], b2_ref[...])


def _dense_call(a, c, h, wo, g1, b1, wg, wu, wd, g2, b2):
    row = lambda i: (i, 0)
    fixed = lambda i: (0, 0)
    once = dict(pipeline_mode=pl.Buffered(1))
    return pl.pallas_call(
        _dense_kernel,
        grid=(T // TM,),
        in_specs=[pl.BlockSpec((TM, ATTN_WIDTH), row),
                  pl.BlockSpec((TM, CONV_CH), row),
                  pl.BlockSpec((TM, D_MODEL), row),
                  pl.BlockSpec((D_MODEL, D_MODEL), fixed, **once),
                  _vec(D_MODEL), _vec(D_MODEL),
                  pl.BlockSpec((D_MODEL, D_FF_DENSE), fixed, **once),
                  pl.BlockSpec((D_MODEL, D_FF_DENSE), fixed, **once),
                  pl.BlockSpec((D_FF_DENSE, D_MODEL), fixed, **once),
                  _vec(D_MODEL), _vec(D_MODEL)],
        out_specs=pl.BlockSpec((TM, D_MODEL), row),
        out_shape=jax.ShapeDtypeStruct((T, D_MODEL), F32),
        compiler_params=_cparams(1),
        name="mix_dense_ffn",
    )(a, c, h, wo, g1.reshape(1, -1), b1.reshape(1, -1), wg, wu, wd,
      g2.reshape(1, -1), b2.reshape(1, -1))


def _router_kernel(a_ref, c_ref, h_ref, wo_ref, g1_ref, b1_ref, rwh_ref, rwl_ref, rb_ref,
                   h1_ref, comb_ref, sel_ref):
    h1 = _mix(a_ref, c_ref, h_ref, wo_ref, g1_ref, b1_ref)
    h1_ref[...] = h1
    xh = h1.astype(BF)
    xl = (h1 - xh.astype(F32)).astype(BF)
    logits = (jnp.dot(xh, rwh_ref[...], preferred_element_type=F32)
              + jnp.dot(xl, rwh_ref[...], preferred_element_type=F32)
              + jnp.dot(xh, rwl_ref[...], preferred_element_type=F32)) + rb_ref[...]
    lane = lax.broadcasted_iota(jnp.int32, (TM, LANES), 1)
    logits = jnp.where(lane < N_EXPERTS, logits, -jnp.inf)
    v1 = jnp.max(logits, -1, keepdims=True)
    i1 = jnp.min(jnp.where(logits == v1, lane, LANES), -1, keepdims=True)
    rest = jnp.where(lane == i1, -jnp.inf, logits)
    v2 = jnp.max(rest, -1, keepdims=True)
    i2 = jnp.min(jnp.where(rest == v2, lane, LANES), -1, keepdims=True)
    e2 = jnp.exp(v2 - v1)
    den = 1.0 + e2
    comb_ref[...] = jnp.where(lane == i1, 1.0 / den, jnp.where(lane == i2, e2 / den, 0.0))
    sel_ref[...] = jnp.where((lane == i1) | (lane == i2), 1.0, 0.0).astype(BF)


def _router_call(a, c, h, wo, g1, b1, rwh, rwl, rb):
    row = lambda i: (i, 0)
    fixed = lambda i: (0, 0)
    return pl.pallas_call(
        _router_kernel,
        grid=(T // TM,),
        in_specs=[pl.BlockSpec((TM, ATTN_WIDTH), row),
                  pl.BlockSpec((TM, CONV_CH), row),
                  pl.BlockSpec((TM, D_MODEL), row),
                  pl.BlockSpec((D_MODEL, D_MODEL), fixed),
                  _vec(D_MODEL), _vec(D_MODEL),
                  pl.BlockSpec((D_MODEL, LANES), fixed),
                  pl.BlockSpec((D_MODEL, LANES), fixed),
                  _vec(LANES)],
        out_specs=[pl.BlockSpec((TM, D_MODEL), row),
                   pl.BlockSpec((TM, LANES), row),
                   pl.BlockSpec((TM, LANES), row)],
        out_shape=[jax.ShapeDtypeStruct((T, D_MODEL), F32),
                   jax.ShapeDtypeStruct((T, LANES), F32),
                   jax.ShapeDtypeStruct((T, LANES), BF)],
        compiler_params=_cparams(1),
        name="mix_router",
    )(a, c, h, wo, g1.reshape(1, -1), b1.reshape(1, -1), rwh, rwl, rb)


def _moe_kernel(x_ref, comb_ref, sel_ref, tri_ref, eg_ref, eu_ref, ed_ref, g2_ref, b2_ref,
                o_ref, xb_ref, rankc_ref, rankr_ref, selr_ref, acc_ref):
    e = pl.program_id(1)
    lane = lax.broadcasted_iota(jnp.int32, (TM_MOE, LANES), 1)

    @pl.when(e == 0)
    def _():
        xb_ref[...] = x_ref[...].astype(BF)
        sel = sel_ref[...]
        rank = jnp.dot(tri_ref[...], sel, preferred_element_type=F32)
        rankc_ref[...] = rank
        rankr_ref[...] = rank.T
        selr_ref[...] = sel.astype(F32).T
        acc_ref[...] = jnp.zeros_like(acc_ref)

    onlane = lane == e
    sel_col = jnp.sum(jnp.where(onlane, sel_ref[...].astype(F32), 0.0), -1, keepdims=True)
    rank_col = jnp.sum(jnp.where(onlane, rankc_ref[...], 0.0), -1, keepdims=True)
    comb_col = jnp.sum(jnp.where(onlane, comb_ref[...], 0.0), -1, keepdims=True)
    sel_row = selr_ref[pl.ds(e, 1), :]
    rank_row = rankr_ref[pl.ds(e, 1), :]
    count = jnp.sum(sel_row).astype(jnp.int32)
    slot_r = lax.broadcasted_iota(jnp.int32, (MOE_CHUNK, TM_MOE), 0).astype(F32)
    slot_c = lax.broadcasted_iota(jnp.int32, (TM_MOE, MOE_CHUNK), 1).astype(F32)

    def chunk(ci, carry):
        base = (ci * MOE_CHUNK).astype(F32)
        take = jnp.where((rank_row - base == slot_r) & (sel_row > 0.0), 1.0, 0.0).astype(BF)
        put = jnp.where((rank_col - base == slot_c) & (sel_col > 0.0), 1.0, 0.0).astype(BF)
        xc = jnp.dot(take, xb_ref[...], preferred_element_type=F32).astype(BF)
        gate = jnp.dot(xc, eg_ref[0], preferred_element_type=F32)
        up = jnp.dot(xc, eu_ref[0], preferred_element_type=F32)
        out = jnp.dot((_silu(gate) * up).astype(BF), ed_ref[0], preferred_element_type=F32)
        acc_ref[...] += comb_col * jnp.dot(put, out.astype(BF), preferred_element_type=F32)
        return carry

    lax.fori_loop(0, (count + MOE_CHUNK - 1) // MOE_CHUNK, chunk, 0)

    @pl.when(e == N_EXPERTS - 1)
    def _():
        o_ref[...] = _ln_rows(ALPHA * x_ref[...] + acc_ref[...], g2_ref[...], b2_ref[...])


def _moe_call(x, comb, sel, tri, eg, eu, ed, g2, b2, tile0, n_tiles):
    row = lambda i, e: (tile0 + i, 0)
    fixed = lambda i, e: (0, 0)
    exp = lambda i, e: (e, 0, 0)
    return pl.pallas_call(
        _moe_kernel,
        grid=(n_tiles, N_EXPERTS),
        in_specs=[pl.BlockSpec((TM_MOE, D_MODEL), row),
                  pl.BlockSpec((TM_MOE, LANES), row),
                  pl.BlockSpec((TM_MOE, LANES), row),
                  pl.BlockSpec((TM_MOE, TM_MOE), fixed),
                  pl.BlockSpec((1, D_MODEL, D_FF_EXPERT), exp),
                  pl.BlockSpec((1, D_MODEL, D_FF_EXPERT), exp),
                  pl.BlockSpec((1, D_FF_EXPERT, D_MODEL), exp),
                  _vec(D_MODEL), _vec(D_MODEL)],
        out_specs=pl.BlockSpec((TM_MOE, D_MODEL), lambda i, e: (i, 0)),
        out_shape=jax.ShapeDtypeStruct((n_tiles * TM_MOE, D_MODEL), F32),
        scratch_shapes=[pltpu.VMEM((TM_MOE, D_MODEL), BF),
                        pltpu.VMEM((TM_MOE, LANES), F32),
                        pltpu.VMEM((LANES, TM_MOE), F32),
                        pltpu.VMEM((LANES, TM_MOE), F32),
                        pltpu.VMEM((TM_MOE, D_MODEL), F32)],
        compiler_params=_cparams(2),
        name="moe_ffn",
    )(x, comb, sel, tri, eg, eu, ed, g2.reshape(1, -1), b2.reshape(1, -1))


def kernel(x_prompt, x_sample, cache_meta_k, cache_meta_v, cache_k, cache_v, state_conv,
           meta_tokens, ln_in_g, ln_in_b, w_in, w_dw, b_dw, conv_ln_g, conv_ln_b, sinks, w_out,
           ln1_g, ln1_b, ln2_g, ln2_b, ffd_w_gate, ffd_w_up, ffd_w_down,
           router_w, router_b, exp_w_gate, exp_w_up, exp_w_down):
    meta_blk = jnp.concatenate([meta_tokens, jnp.zeros((BLOCK - N_META, D_MODEL), F32)], 0)
    x_tail = jnp.concatenate([meta_blk] * BATCH + [x_sample.reshape(NS, D_MODEL),
                                                   jnp.zeros((T - PAD0, D_MODEL), F32)], 0)
    h = _ln_call(x_prompt.reshape(TOK, D_MODEL), x_tail, ln_in_g, ln_in_b)
    cos_np, sin_np = _rope_tables()
    cos_t, sin_t = jnp.asarray(cos_np), jnp.asarray(sin_np)
    tri = jnp.asarray(np.tril(np.ones((TM_MOE, TM_MOE), np.float32), -1), dtype=BF)
    bias = jnp.asarray(_attn_bias())

    per_kv = SAMPLE_ROWS // N_KV_HEADS
    head_of_row = (jnp.arange(SAMPLE_ROWS) // per_kv) * 4 + jnp.arange(SAMPLE_ROWS) % 4
    met_rows = lambda x, b: x[MET0 + b * BLOCK:MET0 + b * BLOCK + N_META]
    kv5 = lambda rows: jnp.stack(rows).reshape(BATCH, -1, N_KV_HEADS, HEAD_DIM)

    pmk, pmv, pk, pv, pc, sk, sv, sc = [], [], [], [], [], [], [], []
    y_tok = y_tail = None
    for i in range(DEPTH):
        q, k, v, u = _inproj_call(h, w_in[i].astype(BF), cos_t, sin_t)

        a = _attn_call(sinks[i], bias, q, k, v)
        c = _pconv_call(u, w_dw[i], b_dw[i], conv_ln_g[i], conv_ln_b[i])
        c_m = _mconv_call(u, w_dw[i], b_dw[i], conv_ln_g[i], conv_ln_b[i])
        pmk.append(kv5([met_rows(k, b) for b in range(BATCH)]))
        pmv.append(kv5([met_rows(v, b) for b in range(BATCH)]))
        pk.append(kv5([k[(b + 1) * SEQ - WINDOW:(b + 1) * SEQ] for b in range(BATCH)]))
        pv.append(kv5([v[(b + 1) * SEQ - WINDOW:(b + 1) * SEQ] for b in range(BATCH)]))
        pc.append(jnp.stack([u[(b + 1) * SEQ - CONV_STATE:(b + 1) * SEQ] for b in range(BATCH)]))

        k2 = k[SMP0:PAD0].reshape(DEC_BATCH, DEC_SEQ, KV_WIDTH)
        v2 = v[SMP0:PAD0].reshape(DEC_BATCH, DEC_SEQ, KV_WIDTH)
        k_loc = jnp.concatenate([cache_k[i].reshape(DEC_BATCH, WINDOW, KV_WIDTH), k2], 1)
        v_loc = jnp.concatenate([cache_v[i].reshape(DEC_BATCH, WINDOW, KV_WIDTH), v2], 1)
        q2 = q[SMP0:PAD0].reshape(DEC_BATCH, DEC_SEQ, N_KV_HEADS, 4, HEAD_DIM)
        q2 = q2.transpose(0, 2, 1, 3, 4).reshape(DEC_BATCH, N_KV_HEADS, per_kv, HEAD_DIM)
        zq = jnp.zeros_like(q2[:, 0])
        qz = jnp.concatenate([jnp.concatenate([q2[:, 0], zq], -1),
                              jnp.concatenate([zq, q2[:, 1]], -1)], 1)
        sink_rows = jnp.broadcast_to(sinks[i][head_of_row][:, None], (SAMPLE_ROWS, LANES))
        o2 = _sattn_call(sink_rows, qz, k_loc,
                         cache_meta_k[i].reshape(DEC_BATCH, N_META, KV_WIDTH), v_loc,
                         cache_meta_v[i].reshape(DEC_BATCH, N_META, KV_WIDTH))
        o2 = jnp.stack([o2[:, :per_kv, :HEAD_DIM], o2[:, per_kv:, HEAD_DIM:]], 1)
        a_s = o2.reshape(DEC_BATCH, N_KV_HEADS, DEC_SEQ, 4, HEAD_DIM).transpose(0, 2, 1, 3, 4)
        a_s = a_s.reshape(NS, ATTN_WIDTH).astype(BF)
        u2 = u[SMP0:PAD0]
        c_s = _sconv_call(state_conv[i].reshape(DEC_BATCH * CONV_STATE, CONV_CH), u2,
                          w_dw[i], b_dw[i], conv_ln_g[i], conv_ln_b[i])
        c_s = c_s.transpose(1, 0, 2).reshape(NS, CONV_CH)
        sk.append(k_loc[:, DEC_SEQ:].reshape(DEC_BATCH, WINDOW, N_KV_HEADS, HEAD_DIM))
        sv.append(v_loc[:, DEC_SEQ:].reshape(DEC_BATCH, WINDOW, N_KV_HEADS, HEAD_DIM))
        sc.append(jnp.concatenate([state_conv[i][:, DEC_SEQ:],
                                   u2.reshape(DEC_BATCH, DEC_SEQ, CONV_CH)], 1))

        a = lax.dynamic_update_slice(a, a_s, (SMP0, 0))
        c = lax.dynamic_update_slice(c, jnp.concatenate([c_m, c_s], 0), (MET0, 0))
        wo = w_out[i].astype(BF)
        j = i // 2
        if i % 2 == 0:
            h = _dense_call(a, c, h, wo, ln1_g[i], ln1_b[i], ffd_w_gate[j].astype(BF),
                            ffd_w_up[j].astype(BF), ffd_w_down[j].astype(BF), ln2_g[i], ln2_b[i])
        else:
            rw = jnp.pad(router_w[j], ((0, 0), (0, LANES - N_EXPERTS)))
            rwh = rw.astype(BF)
            rwl = (rw - rwh.astype(F32)).astype(BF)
            rb = jnp.pad(router_b[j], (0, LANES - N_EXPERTS)).reshape(1, LANES)
            h1, comb, sel = _router_call(a, c, h, wo, ln1_g[i], ln1_b[i], rwh, rwl, rb)
            moe = functools.partial(_moe_call, h1, comb, sel, tri, exp_w_gate[j].astype(BF),
                                    exp_w_up[j].astype(BF), exp_w_down[j].astype(BF),
                                    ln2_g[i], ln2_b[i])
            if i < DEPTH - 1:
                h = moe(0, T // TM_MOE)
            else:
                y_tok = moe(0, TOK // TM_MOE)
                y_tail = moe(TOK // TM_MOE, TAIL // TM_MOE)

    y_prompt = y_tok.reshape(BATCH, SEQ, D_MODEL)
    y_sample = y_tail[SMP0 - TOK:PAD0 - TOK].reshape(DEC_BATCH, DEC_SEQ, D_MODEL)
    return (y_prompt, y_sample, jnp.stack(pmk), jnp.stack(pmv), jnp.stack(pk), jnp.stack(pv),
            jnp.stack(pc), jnp.stack(sk), jnp.stack(sv), jnp.stack(sc))
```

```python
import functools

import numpy as np
import jax
import jax.numpy as jnp
from jax import lax
from jax.experimental import pallas as pl
from jax.experimental.pallas import tpu as pltpu

D_MODEL = 1024
BATCH = 2
SEQ = 8192
DEPTH = 4
DEC_BATCH = 128
DEC_SEQ = 4
PAST_LEN = 8192
N_META = 16
HEAD_DIM = 64
N_Q_HEADS = 8
N_KV_HEADS = 2
ATTN_WIDTH = N_Q_HEADS * HEAD_DIM
KV_WIDTH = N_KV_HEADS * HEAD_DIM
CONV_CH = D_MODEL - ATTN_WIDTH
Q_END = ATTN_WIDTH
K_END = Q_END + KV_WIDTH
V_END = K_END + KV_WIDTH
IN_COLS = V_END + 2 * CONV_CH
CONV_WIDTH = 31
CONV_STATE = CONV_WIDTH - 1
WINDOW = 128
BLOCK = 128
ROPE_THETA = 10000.0
ATTN_SCALE = HEAD_DIM ** -0.5
D_FF_DENSE = 2816
N_EXPERTS = 8
D_FF_EXPERT = 1024
ALPHA = (2 * DEPTH) ** 0.25
LN_EPS = 1e-5

LANES = 128
SUBLANES = 8
TOK = BATCH * SEQ
MET0 = TOK
SMP0 = MET0 + BATCH * BLOCK
NS = DEC_BATCH * DEC_SEQ
PAD0 = SMP0 + NS
TM = 512
TM_MOE = 1024
T = -(-PAD0 // TM_MOE) * TM_MOE
TAIL = T - TOK
ATT_Q = 2 * BLOCK
ATT_STEPS = SEQ // ATT_Q
W_LOC = WINDOW + DEC_SEQ
N_KEYS = 2 * BLOCK + N_META
CONV_TILE = 512
CONV_HALO = 32
CONV_CHUNK = 64
SAMPLE_BT = 16
MOE_CHUNK = 128
VMEM_LIMIT = 56 * 1024 * 1024

BF = jnp.bfloat16
F32 = jnp.float32


def _cparams(n_axes):
    return pltpu.CompilerParams(dimension_semantics=("arbitrary",) * n_axes,
                                vmem_limit_bytes=VMEM_LIMIT)


def _ln_rows(x, g, b):
    mu = jnp.mean(x, -1, keepdims=True)
    xc = x - mu
    var = jnp.mean(xc * xc, -1, keepdims=True)
    return xc * lax.rsqrt(var + LN_EPS) * g + b


def _silu(x):
    return x * jax.nn.sigmoid(x)


def _vec(n):
    return pl.BlockSpec((1, n), lambda *_: (0, 0))


def _ln_kernel(xt_ref, xr_ref, g_ref, b_ref, o_ref):
    i = pl.program_id(0)

    @pl.when(i < TOK // TM)
    def _():
        o_ref[...] = _ln_rows(xt_ref[...], g_ref[...], b_ref[...])

    @pl.when(i >= TOK // TM)
    def _():
        o_ref[...] = _ln_rows(xr_ref[...], g_ref[...], b_ref[...])


def _project(h_ref, w_ref, cos_ref, sin_ref, q_ref, k_ref, v_ref, u_ref):
    hb = h_ref[...].astype(BF)
    cos = cos_ref[...]
    sin = sin_ref[...]
    lane = lax.broadcasted_iota(jnp.int32, (TM, LANES), 1)
    first_half = (lane % HEAD_DIM) < (HEAD_DIM // 2)

    def rope(x):
        rot = jnp.where(first_half, pltpu.roll(x, LANES - HEAD_DIM // 2, 1),
                        pltpu.roll(x, HEAD_DIM // 2, 1))
        return x * cos + rot * sin

    qk = jnp.dot(hb, w_ref[0, :, :K_END], preferred_element_type=F32)
    for c in range(Q_END // LANES):
        q_ref[:, c * LANES:(c + 1) * LANES] = (
            rope(qk[:, c * LANES:(c + 1) * LANES]) * ATTN_SCALE).astype(BF)
    k_ref[...] = rope(qk[:, Q_END:K_END])
    v_ref[...] = jnp.dot(hb, w_ref[0, :, K_END:V_END], preferred_element_type=F32)
    val = jnp.dot(hb, w_ref[0, :, V_END:V_END + CONV_CH], preferred_element_type=F32)
    gate = jnp.dot(hb, w_ref[0, :, V_END + CONV_CH:], preferred_element_type=F32)
    u_ref[...] = val * jax.nn.sigmoid(gate)


def _inproj_kernel(h_ref, w_ref, cos_ref, sin_ref, q_ref, k_ref, v_ref, u_ref):
    _project(h_ref, w_ref, cos_ref, sin_ref, q_ref, k_ref, v_ref, u_ref)


def _inproj0_kernel(xt_ref, xr_ref, g_ref, b_ref, w_ref, cos_ref, sin_ref,
                    h_ref, q_ref, k_ref, v_ref, u_ref):
    _ln_kernel(xt_ref, xr_ref, g_ref, b_ref, h_ref)
    _project(h_ref, w_ref, cos_ref, sin_ref, q_ref, k_ref, v_ref, u_ref)


def _rope_tables():
    half = HEAD_DIM // 2
    pos = np.zeros((SEQ + TAIL,), np.float64)
    pos[:SEQ] = N_META + np.arange(SEQ)
    for b in range(BATCH):
        pos[SEQ + b * BLOCK:SEQ + b * BLOCK + N_META] = np.arange(N_META)
    s0 = SEQ + SMP0 - MET0
    pos[s0:s0 + NS] = PAST_LEN + np.tile(np.arange(DEC_SEQ), DEC_BATCH)
    inv = ROPE_THETA ** (-np.arange(half, dtype=np.float64) / half)
    ang = pos[:, None] * inv[None, :]
    cos = np.cos(ang).astype(np.float32)
    sin = np.sin(ang).astype(np.float32)
    return np.tile(cos, (1, 4)), np.tile(np.concatenate([-sin, sin], -1), (1, 2))


def _inproj_specs(layer):
    row = lambda i: (i, 0)
    per_batch = SEQ // TM
    tab = lambda i: (jnp.where(i < 2 * per_batch, i % per_batch, i - per_batch), 0)
    in_specs = [pl.BlockSpec((1, D_MODEL, IN_COLS), lambda i: (layer, 0, 0)),
                pl.BlockSpec((TM, LANES), tab),
                pl.BlockSpec((TM, LANES), tab)]
    out_specs = [pl.BlockSpec((TM, ATTN_WIDTH), row),
                 pl.BlockSpec((TM, KV_WIDTH), row),
                 pl.BlockSpec((TM, KV_WIDTH), row),
                 pl.BlockSpec((TM, CONV_CH), row)]
    out_shape = [jax.ShapeDtypeStruct((T, ATTN_WIDTH), BF),
                 jax.ShapeDtypeStruct((T, KV_WIDTH), F32),
                 jax.ShapeDtypeStruct((T, KV_WIDTH), F32),
                 jax.ShapeDtypeStruct((T, CONV_CH), F32)]
    return in_specs, out_specs, out_shape


def _inproj_call(h, w_bf, layer, cos_t, sin_t):
    in_specs, out_specs, out_shape = _inproj_specs(layer)
    return pl.pallas_call(
        _inproj_kernel,
        grid=(T // TM,),
        in_specs=[pl.BlockSpec((TM, D_MODEL), lambda i: (i, 0))] + in_specs,
        out_specs=out_specs,
        out_shape=out_shape,
        compiler_params=_cparams(1),
        name="inproj",
    )(h, w_bf, cos_t, sin_t)


def _inproj0_call(x_tok, x_tail, g, b, w_bf, cos_t, sin_t):
    in_specs, out_specs, out_shape = _inproj_specs(0)
    n_tok = TOK // TM
    row = pl.BlockSpec((TM, D_MODEL), lambda i: (i, 0))
    return pl.pallas_call(
        _inproj0_kernel,
        grid=(T // TM,),
        in_specs=[pl.BlockSpec((TM, D_MODEL), lambda i: (jnp.minimum(i, n_tok - 1), 0)),
                  pl.BlockSpec((TM, D_MODEL), lambda i: (jnp.maximum(i - n_tok, 0), 0)),
                  _vec(D_MODEL), _vec(D_MODEL)] + in_specs,
        out_specs=[row] + out_specs,
        out_shape=[jax.ShapeDtypeStruct((T, D_MODEL), F32)] + out_shape,
        compiler_params=_cparams(1),
        name="ln_inproj",
    )(x_tok, x_tail, g.reshape(1, D_MODEL), b.reshape(1, D_MODEL), w_bf, cos_t, sin_t)


def _head_variants(x):
    lane = lax.broadcasted_iota(jnp.int32, x.shape, x.ndim - 1)
    lo = lane < HEAD_DIM
    xr = pltpu.roll(x, HEAD_DIM, x.ndim - 1)
    zero = jnp.zeros_like(x)
    return [[jnp.where(lo, x, zero).astype(BF), jnp.where(lo, zero, xr).astype(BF)],
            [jnp.where(lo, xr, zero).astype(BF), jnp.where(lo, zero, x).astype(BF)]]


def _attn_bias():
    qi = np.arange(2 * BLOCK)[:, None] % BLOCK
    col = np.arange(N_KEYS)[None, :]
    d = BLOCK + qi - col
    band = (col < 2 * BLOCK) & (d >= 0) & (d <= WINDOW)
    is_meta = col >= 2 * BLOCK
    normal = band | is_meta
    first = (band & (col >= BLOCK)) | is_meta
    meta_q = is_meta & ((col - 2 * BLOCK) <= qi)
    return np.where(np.stack([normal, first, meta_q]), 0.0, -np.inf).astype(np.float32)


def _attn_kernel(sinks_ref, bias_ref, q_ref, kp_ref, kc_ref, km0_ref, km1_ref,
                 vp_ref, vc_ref, vm0_ref, vm1_ref, o_ref):
    s_id = pl.program_id(0)
    n_tok = BATCH * ATT_STEPS

    @pl.when(s_id > n_tok)
    def _():
        o_ref[...] = jnp.zeros_like(o_ref)

    @pl.when(s_id <= n_tok)
    def _():
        is_meta = s_id == n_tok
        first = (s_id % ATT_STEPS) == 0
        variant = [jnp.where(is_meta, 2, jnp.where(first, 1, 0)), jnp.where(is_meta, 2, 0)]
        kc = kc_ref[...]
        vc = vc_ref[...]
        keys = [jnp.concatenate([kp_ref[...], kc[:BLOCK], km0_ref[...]], axis=0),
                jnp.concatenate([kc[:BLOCK], kc[BLOCK:], km1_ref[...]], axis=0)]
        vals = [jnp.concatenate([vp_ref[...], vc[:BLOCK], vm0_ref[...]], axis=0),
                jnp.concatenate([vc[:BLOCK], vc[BLOCK:], vm1_ref[...]], axis=0)]
        upper = lax.broadcasted_iota(jnp.int32, (2 * BLOCK, 1), 0) < BLOCK

        for h in range(2):
            rows = slice(h * BLOCK, (h + 1) * BLOCK)
            kvar = _head_variants(keys[h])
            vvar = _head_variants(vals[h])
            bias = bias_ref[variant[h]]
            for g in range(N_KV_HEADS):
                qg = jnp.concatenate([q_ref[rows, (2 * g) * LANES:(2 * g + 1) * LANES],
                                      q_ref[rows, (2 * g + 1) * LANES:(2 * g + 2) * LANES]],
                                     axis=0)
                acc = jnp.zeros((2 * BLOCK, LANES), F32)
                for e in range(2):
                    s = lax.dot_general(qg, kvar[g][e], (((1,), (1,)), ((), ())),
                                        preferred_element_type=F32) + bias
                    sk = jnp.where(upper, sinks_ref[4 * g + e], sinks_ref[4 * g + 2 + e])
                    m = jnp.maximum(jnp.max(s, -1, keepdims=True), sk)
                    p = jnp.exp(s - m)
                    den = jnp.sum(p, -1, keepdims=True) + jnp.exp(sk - m)
                    pn = (p * (1.0 / den)).astype(BF)
                    acc = acc + jnp.dot(pn, vvar[g][e], preferred_element_type=F32)
                o_ref[rows, (2 * g) * LANES:(2 * g + 1) * LANES] = acc[:BLOCK].astype(BF)
                o_ref[rows, (2 * g + 1) * LANES:(2 * g + 2) * LANES] = acc[BLOCK:].astype(BF)


def _attn_call(sinks, bias, q, k, v):
    n_tok = BATCH * ATT_STEPS
    cur = lambda s: (jnp.minimum(s, n_tok), 0)

    def prev(s):
        return (jnp.minimum(jnp.where(s % ATT_STEPS == 0, 2 * s, 2 * s - 1), 2 * n_tok), 0)

    def meta(h):
        def index(s):
            b = jnp.where(s < n_tok, s // ATT_STEPS, jnp.where(s == n_tok, h, 0))
            return ((MET0 + b * BLOCK) // N_META, 0)
        return index

    return pl.pallas_call(
        _attn_kernel,
        grid=(T // ATT_Q,),
        in_specs=[pl.BlockSpec(memory_space=pltpu.SMEM),
                  pl.BlockSpec((3, ATT_Q, N_KEYS), lambda s: (0, 0, 0)),
                  pl.BlockSpec((ATT_Q, ATTN_WIDTH), cur),
                  pl.BlockSpec((BLOCK, KV_WIDTH), prev),
                  pl.BlockSpec((ATT_Q, KV_WIDTH), cur),
                  pl.BlockSpec((N_META, KV_WIDTH), meta(0)),
                  pl.BlockSpec((N_META, KV_WIDTH), meta(1)),
                  pl.BlockSpec((BLOCK, KV_WIDTH), prev),
                  pl.BlockSpec((ATT_Q, KV_WIDTH), cur),
                  pl.BlockSpec((N_META, KV_WIDTH), meta(0)),
                  pl.BlockSpec((N_META, KV_WIDTH), meta(1))],
        out_specs=pl.BlockSpec((ATT_Q, ATTN_WIDTH), lambda s: (s, 0)),
        out_shape=jax.ShapeDtypeStruct((T, ATTN_WIDTH), BF),
        compiler_params=_cparams(1),
        name="prompt_attn",
    )(sinks, bias, q, k, k, k, k, v, v, v, v)


SAMPLE_ROWS = N_KV_HEADS * DEC_SEQ * (N_Q_HEADS // N_KV_HEADS)


def _sattn_kernel(sk_ref, q_ref, kl_ref, km_ref, vl_ref, vm_ref, o_ref):
    q = q_ref[...]
    nt = (((2,), (2,)), ((0,), (0,)))
    s_loc = lax.dot_general(q, kl_ref[...].astype(BF), nt, preferred_element_type=F32)
    s_meta = lax.dot_general(q, km_ref[...].astype(BF), nt, preferred_element_type=F32)
    r = lax.broadcasted_iota(jnp.int32, (1, SAMPLE_ROWS, W_LOC), 1)
    idx = lax.broadcasted_iota(jnp.int32, (1, SAMPLE_ROWS, W_LOC), 2)
    sq = (r % (SAMPLE_ROWS // N_KV_HEADS)) // (N_Q_HEADS // N_KV_HEADS)
    ok = (idx >= sq) & (idx <= WINDOW + sq)
    s_loc = jnp.where(ok, s_loc, -jnp.inf)
    sk = sk_ref[...][None, :, :1]
    m = jnp.maximum(jnp.maximum(jnp.max(s_loc, -1, keepdims=True),
                                jnp.max(s_meta, -1, keepdims=True)), sk)
    p_loc = jnp.exp(s_loc - m)
    p_meta = jnp.exp(s_meta - m)
    den = (jnp.sum(p_loc, -1, keepdims=True) + jnp.sum(p_meta, -1, keepdims=True)
           + jnp.exp(sk - m))
    inv = 1.0 / den
    nn = (((2,), (1,)), ((0,), (0,)))
    o_ref[...] = (
        lax.dot_general((p_loc * inv).astype(BF), vl_ref[...].astype(BF), nn,
                        preferred_element_type=F32)
        + lax.dot_general((p_meta * inv).astype(BF), vm_ref[...].astype(BF), nn,
                          preferred_element_type=F32))


def _sattn_call(sink_rows, qz, k_loc, mk, v_loc, mv):
    b3 = lambda i: (i, 0, 0)
    return pl.pallas_call(
        _sattn_kernel,
        grid=(DEC_BATCH // SAMPLE_BT,),
        in_specs=[pl.BlockSpec((SAMPLE_ROWS, LANES), lambda i: (0, 0)),
                  pl.BlockSpec((SAMPLE_BT, SAMPLE_ROWS, LANES), b3),
                  pl.BlockSpec((SAMPLE_BT, W_LOC, KV_WIDTH), b3),
                  pl.BlockSpec((SAMPLE_BT, N_META, KV_WIDTH), b3),
                  pl.BlockSpec((SAMPLE_BT, W_LOC, KV_WIDTH), b3),
                  pl.BlockSpec((SAMPLE_BT, N_META, KV_WIDTH), b3)],
        out_specs=pl.BlockSpec((SAMPLE_BT, SAMPLE_ROWS, LANES), b3),
        out_shape=jax.ShapeDtypeStruct((DEC_BATCH, SAMPLE_ROWS, LANES), F32),
        compiler_params=_cparams(1),
        name="sample_attn",
    )(sink_rows, qz, k_loc, mk, v_loc, mv)


def _conv_tail(y, g, b):
    return _silu(_ln_rows(y, g, b))


def _conv_rows(ext_ref, sh_ref, y_ref, w_ref, bdw_ref, g_ref, b_ref, o_ref, n_rows):
    off = CONV_HALO - CONV_STATE
    span = n_rows + CONV_HALO - SUBLANES
    for s in range(1, SUBLANES):
        sh_ref[s - 1, :span, :] = ext_ref[s:s + span, :]
    for lg in range(CONV_CH // LANES):
        lanes = slice(lg * LANES, (lg + 1) * LANES)
        for r0 in range(0, n_rows, CONV_CHUNK):
            acc = jnp.broadcast_to(bdw_ref[:, lanes], (CONV_CHUNK, LANES))
            for j in range(CONV_WIDTH):
                a, s = divmod(off + j, SUBLANES)
                src = ext_ref if s == 0 else sh_ref.at[s - 1]
                x = src[r0 + a * SUBLANES:r0 + a * SUBLANES + CONV_CHUNK, lanes]
                acc = acc + x * w_ref[j:j + 1, lanes]
            y_ref[r0:r0 + CONV_CHUNK, lanes] = acc
    for r0 in range(0, n_rows, CONV_CHUNK):
        y = y_ref[r0:r0 + CONV_CHUNK, :]
        o_ref[r0:r0 + CONV_CHUNK, :] = _conv_tail(y, g_ref[...], b_ref[...]).astype(BF)


def _conv_scratch(n_rows):
    return [pltpu.VMEM((CONV_HALO + n_rows, CONV_CH), F32),
            pltpu.VMEM((SUBLANES - 1, CONV_HALO + n_rows - SUBLANES, CONV_CH), F32),
            pltpu.VMEM((n_rows, CONV_CH), F32)]


def _pconv_kernel(prev_ref, cur_ref, w_ref, bdw_ref, g_ref, b_ref, o_ref, ext_ref, sh_ref,
                  y_ref):
    t = pl.program_id(1)
    n_tok = SEQ // CONV_TILE

    @pl.when(t == n_tok)
    def _():
        o_ref[...] = jnp.zeros_like(o_ref)

    @pl.when(t < n_tok)
    def _():
        half = CONV_HALO // 2
        ext_ref[:half] = jnp.where(t > 0, prev_ref[:half], 0.0)
        ext_ref[half:CONV_HALO] = jnp.where(t > 0, prev_ref[half:], prev_ref[:half])
        ext_ref[CONV_HALO:] = cur_ref[...]
        _conv_rows(ext_ref, sh_ref, y_ref, w_ref, bdw_ref, g_ref, b_ref, o_ref, CONV_TILE)


def _pconv_call(u, w_dw, b_dw, g, b):
    n_tok = SEQ // CONV_TILE
    per_tile = CONV_TILE // CONV_HALO

    def cur(bb, t):
        return (jnp.where(t < n_tok, bb * n_tok + t, MET0 // CONV_TILE + bb), 0)

    def prev(bb, t):
        tok = (bb * n_tok + jnp.minimum(t, n_tok - 1)) * per_tile - 1
        return (jnp.where(t > 0, tok, (MET0 + bb * BLOCK) // CONV_HALO), 0)

    fixed = lambda bb, t: (0, 0)
    return pl.pallas_call(
        _pconv_kernel,
        grid=(BATCH, n_tok + 1),
        in_specs=[pl.BlockSpec((CONV_HALO, CONV_CH), prev),
                  pl.BlockSpec((CONV_TILE, CONV_CH), cur),
                  pl.BlockSpec((CONV_WIDTH, CONV_CH), fixed),
                  _vec(CONV_CH), _vec(CONV_CH), _vec(CONV_CH)],
        out_specs=pl.BlockSpec((CONV_TILE, CONV_CH), cur),
        out_shape=jax.ShapeDtypeStruct((T, CONV_CH), BF),
        scratch_shapes=_conv_scratch(CONV_TILE),
        compiler_params=_cparams(2),
        name="prompt_conv",
    )(u, u, w_dw, b_dw.reshape(1, CONV_CH), g.reshape(1, CONV_CH), b.reshape(1, CONV_CH))


def _mconv_kernel(cur_ref, w_ref, bdw_ref, g_ref, b_ref, o_ref, ext_ref, sh_ref, y_ref):
    ext_ref[:CONV_HALO] = jnp.zeros((CONV_HALO, CONV_CH), F32)
    ext_ref[CONV_HALO:] = cur_ref[...]
    _conv_rows(ext_ref, sh_ref, y_ref, w_ref, bdw_ref, g_ref, b_ref, o_ref, BLOCK)


def _mconv_call(u, w_dw, b_dw, g, b):
    fixed = lambda bb: (0, 0)
    return pl.pallas_call(
        _mconv_kernel,
        grid=(BATCH,),
        in_specs=[pl.BlockSpec((BLOCK, CONV_CH), lambda bb: (MET0 // BLOCK + bb, 0)),
                  pl.BlockSpec((CONV_WIDTH, CONV_CH), fixed),
                  _vec(CONV_CH), _vec(CONV_CH), _vec(CONV_CH)],
        out_specs=pl.BlockSpec((BLOCK, CONV_CH), lambda bb: (bb, 0)),
        out_shape=jax.ShapeDtypeStruct((BATCH * BLOCK, CONV_CH), BF),
        scratch_shapes=_conv_scratch(BLOCK),
        compiler_params=_cparams(1),
        name="meta_conv",
    )(u, w_dw, b_dw.reshape(1, CONV_CH), g.reshape(1, CONV_CH), b.reshape(1, CONV_CH))


def _sconv_kernel(st_ref, u_ref, w_ref, bdw_ref, g_ref, b_ref, o_ref, y_ref):
    l = pl.program_id(0)

    def ext_row(r):
        if r < CONV_STATE:
            return st_ref[pl.ds(r, DEC_BATCH, stride=CONV_STATE), :]
        return u_ref[pl.ds(r - CONV_STATE, DEC_BATCH, stride=DEC_SEQ), :]

    rows = [ext_row(r) for r in range(CONV_STATE + DEC_SEQ)]
    for s in range(DEC_SEQ):
        acc = jnp.broadcast_to(bdw_ref[...], (DEC_BATCH, LANES))
        for j in range(CONV_WIDTH):
            acc = acc + rows[s + j] * w_ref[j:j + 1, :]
        y_ref[l, s] = acc

    @pl.when(l == CONV_CH // LANES - 1)
    def _():
        for s in range(DEC_SEQ):
            y = jnp.concatenate([y_ref[c, s] for c in range(CONV_CH // LANES)], axis=-1)
            o_ref[s] = _conv_tail(y, g_ref[...], b_ref[...]).astype(BF)


def _sconv_call(state, u, w_dw, b_dw, g, b):
    grp = lambda rows: pl.BlockSpec((rows, LANES), lambda l: (0, l))
    full = lambda shape: pl.BlockSpec(shape, lambda l: (0,) * len(shape))
    return pl.pallas_call(
        _sconv_kernel,
        grid=(CONV_CH // LANES,),
        in_specs=[grp(DEC_BATCH * CONV_STATE), grp(NS), grp(CONV_WIDTH), grp(1),
                  full((1, CONV_CH)), full((1, CONV_CH))],
        out_specs=full((DEC_SEQ, DEC_BATCH, CONV_CH)),
        out_shape=jax.ShapeDtypeStruct((DEC_SEQ, DEC_BATCH, CONV_CH), BF),
        scratch_shapes=[pltpu.VMEM((CONV_CH // LANES, DEC_SEQ, DEC_BATCH, LANES), F32)],
        compiler_params=_cparams(1),
        name="sample_conv",
    )(state, u, w_dw, b_dw.reshape(1, CONV_CH), g.reshape(1, CONV_CH), b.reshape(1, CONV_CH))


def _mix(a_ref, c_ref, h_ref, wo_ref, g1_ref, b1_ref):
    mix = (jnp.dot(a_ref[...], wo_ref[0, :ATTN_WIDTH, :], preferred_element_type=F32)
           + jnp.dot(c_ref[...], wo_ref[0, ATTN_WIDTH:, :], preferred_element_type=F32))
    return _ln_rows(ALPHA * h_ref[...] + mix, g1_ref[...], b1_ref[...])


def _dense_kernel(a_ref, c_ref, h_ref, wo_ref, g1_ref, b1_ref, wg_ref, wu_ref, wd_ref,
                  g2_ref, b2_ref, o_ref):
    h1 = _mix(a_ref, c_ref, h_ref, wo_ref, g1_ref, b1_ref)
    hb = h1.astype(BF)
    gate = jnp.dot(hb, wg_ref[0], preferred_element_type=F32)
    up = jnp.dot(hb, wu_ref[0], preferred_element_type=F32)
    f = jnp.dot((_silu(gate) * up).astype(BF), wd_ref[0], preferred_element_type=F32)
    o_ref[...] = _ln_rows(ALPHA * h1 + f, g2_ref[...], b2_ref[...])


def _dense_call(a, c, h, wo, layer, g1, b1, wg, wu, wd, j, g2, b2):
    row = lambda i: (i, 0)
    once = dict(pipeline_mode=pl.Buffered(1))
    return pl.pallas_call(
        _dense_kernel,
        grid=(T // TM,),
        in_specs=[pl.BlockSpec((TM, ATTN_WIDTH), row),
                  pl.BlockSpec((TM, CONV_CH), row),
                  pl.BlockSpec((TM, D_MODEL), row),
                  pl.BlockSpec((1, D_MODEL, D_MODEL), lambda i: (layer, 0, 0), **once),
                  _vec(D_MODEL), _vec(D_MODEL),
                  pl.BlockSpec((1, D_MODEL, D_FF_DENSE), lambda i: (j, 0, 0), **once),
                  pl.BlockSpec((1, D_MODEL, D_FF_DENSE), lambda i: (j, 0, 0), **once),
                  pl.BlockSpec((1, D_FF_DENSE, D_MODEL), lambda i: (j, 0, 0), **once),
                  _vec(D_MODEL), _vec(D_MODEL)],
        out_specs=pl.BlockSpec((TM, D_MODEL), row),
        out_shape=jax.ShapeDtypeStruct((T, D_MODEL), F32),
        compiler_params=_cparams(1),
        name="mix_dense_ffn",
    )(a, c, h, wo, g1.reshape(1, -1), b1.reshape(1, -1), wg, wu, wd,
      g2.reshape(1, -1), b2.reshape(1, -1))


def _router_kernel(a_ref, c_ref, h_ref, wo_ref, g1_ref, b1_ref, rwh_ref, rwl_ref, rb_ref,
                   h1_ref, comb_ref, sel_ref):
    h1 = _mix(a_ref, c_ref, h_ref, wo_ref, g1_ref, b1_ref)
    h1_ref[...] = h1
    xh = h1.astype(BF)
    xl = (h1 - xh.astype(F32)).astype(BF)
    logits = (jnp.dot(xh, rwh_ref[...], preferred_element_type=F32)
              + jnp.dot(xl, rwh_ref[...], preferred_element_type=F32)
              + jnp.dot(xh, rwl_ref[...], preferred_element_type=F32)) + rb_ref[...]
    lane = lax.broadcasted_iota(jnp.int32, (TM, LANES), 1)
    logits = jnp.where(lane < N_EXPERTS, logits, -jnp.inf)
    v1 = jnp.max(logits, -1, keepdims=True)
    i1 = jnp.min(jnp.where(logits == v1, lane, LANES), -1, keepdims=True)
    rest = jnp.where(lane == i1, -jnp.inf, logits)
    v2 = jnp.max(rest, -1, keepdims=True)
    i2 = jnp.min(jnp.where(rest == v2, lane, LANES), -1, keepdims=True)
    e2 = jnp.exp(v2 - v1)
    den = 1.0 + e2
    comb_ref[...] = jnp.where(lane == i1, 1.0 / den, jnp.where(lane == i2, e2 / den, 0.0))
    sel_ref[...] = jnp.where((lane == i1) | (lane == i2), 1.0, 0.0).astype(BF)


def _router_call(a, c, h, wo, layer, g1, b1, rwh, rwl, rb):
    row = lambda i: (i, 0)
    fixed = lambda i: (0, 0)
    return pl.pallas_call(
        _router_kernel,
        grid=(T // TM,),
        in_specs=[pl.BlockSpec((TM, ATTN_WIDTH), row),
                  pl.BlockSpec((TM, CONV_CH), row),
                  pl.BlockSpec((TM, D_MODEL), row),
                  pl.BlockSpec((1, D_MODEL, D_MODEL), lambda i: (layer, 0, 0)),
                  _vec(D_MODEL), _vec(D_MODEL),
                  pl.BlockSpec((D_MODEL, LANES), fixed),
                  pl.BlockSpec((D_MODEL, LANES), fixed),
                  _vec(LANES)],
        out_specs=[pl.BlockSpec((TM, D_MODEL), row),
                   pl.BlockSpec((TM, LANES), row),
                   pl.BlockSpec((TM, LANES), row)],
        out_shape=[jax.ShapeDtypeStruct((T, D_MODEL), F32),
                   jax.ShapeDtypeStruct((T, LANES), F32),
                   jax.ShapeDtypeStruct((T, LANES), BF)],
        compiler_params=_cparams(1),
        name="mix_router",
    )(a, c, h, wo, g1.reshape(1, -1), b1.reshape(1, -1), rwh, rwl, rb)


def _moe_kernel(x_ref, comb_ref, sel_ref, tri_ref, eg_ref, eu_ref, ed_ref, g2_ref, b2_ref,
                o_ref, xb_ref, rankc_ref, rankr_ref, selr_ref, acc_ref):
    e = pl.program_id(1)
    lane = lax.broadcasted_iota(jnp.int32, (TM_MOE, LANES), 1)

    @pl.when(e == 0)
    def _():
        xb_ref[...] = x_ref[...].astype(BF)
        sel = sel_ref[...]
        rank = jnp.dot(tri_ref[...], sel, preferred_element_type=F32)
        rankc_ref[...] = rank
        rankr_ref[...] = rank.T
        selr_ref[...] = sel.astype(F32).T
        acc_ref[...] = jnp.zeros_like(acc_ref)

    onlane = lane == e
    sel_col = jnp.sum(jnp.where(onlane, sel_ref[...].astype(F32), 0.0), -1, keepdims=True)
    rank_col = jnp.sum(jnp.where(onlane, rankc_ref[...], 0.0), -1, keepdims=True)
    comb_col = jnp.sum(jnp.where(onlane, comb_ref[...], 0.0), -1, keepdims=True)
    sel_row = selr_ref[pl.ds(e, 1), :]
    rank_row = rankr_ref[pl.ds(e, 1), :]
    count = jnp.sum(sel_row).astype(jnp.int32)

    def expert_pass(first, size):
        base = first.astype(F32)
        slot_r = lax.broadcasted_iota(jnp.int32, (size, TM_MOE), 0).astype(F32)
        slot_c = lax.broadcasted_iota(jnp.int32, (TM_MOE, size), 1).astype(F32)
        take = jnp.where((rank_row - base == slot_r) & (sel_row > 0.0), 1.0, 0.0).astype(BF)
        put = jnp.where((rank_col - base == slot_c) & (sel_col > 0.0), 1.0, 0.0).astype(BF)
        xc = jnp.dot(take, xb_ref[...], preferred_element_type=F32).astype(BF)
        gate = jnp.dot(xc, eg_ref[0, 0], preferred_element_type=F32)
        up = jnp.dot(xc, eu_ref[0, 0], preferred_element_type=F32)
        out = jnp.dot((_silu(gate) * up).astype(BF), ed_ref[0, 0], preferred_element_type=F32)
        acc_ref[...] += comb_col * jnp.dot(put, out.astype(BF), preferred_element_type=F32)

    n_big = (count + MOE_CHUNK - 1) // (2 * MOE_CHUNK)

    def big_pass(ci, carry):
        expert_pass(ci * (2 * MOE_CHUNK), 2 * MOE_CHUNK)
        return carry

    lax.fori_loop(0, n_big, big_pass, 0)

    @pl.when(count > n_big * (2 * MOE_CHUNK))
    def _():
        expert_pass(n_big * (2 * MOE_CHUNK), MOE_CHUNK)

    @pl.when(e == N_EXPERTS - 1)
    def _():
        o_ref[...] = _ln_rows(ALPHA * x_ref[...] + acc_ref[...], g2_ref[...], b2_ref[...])


def _moe_call(x, comb, sel, tri, eg, eu, ed, j, g2, b2, tile0, n_tiles):
    row = lambda i, e: (tile0 + i, 0)
    fixed = lambda i, e: (0, 0)
    exp = lambda i, e: (j, e, 0, 0)
    return pl.pallas_call(
        _moe_kernel,
        grid=(n_tiles, N_EXPERTS),
        in_specs=[pl.BlockSpec((TM_MOE, D_MODEL), row),
                  pl.BlockSpec((TM_MOE, LANES), row),
                  pl.BlockSpec((TM_MOE, LANES), row),
                  pl.BlockSpec((TM_MOE, TM_MOE), fixed),
                  pl.BlockSpec((1, 1, D_MODEL, D_FF_EXPERT), exp),
                  pl.BlockSpec((1, 1, D_MODEL, D_FF_EXPERT), exp),
                  pl.BlockSpec((1, 1, D_FF_EXPERT, D_MODEL), exp),
                  _vec(D_MODEL), _vec(D_MODEL)],
        out_specs=pl.BlockSpec((TM_MOE, D_MODEL), lambda i, e: (i, 0)),
        out_shape=jax.ShapeDtypeStruct((n_tiles * TM_MOE, D_MODEL), F32),
        scratch_shapes=[pltpu.VMEM((TM_MOE, D_MODEL), BF),
                        pltpu.VMEM((TM_MOE, LANES), F32),
                        pltpu.VMEM((LANES, TM_MOE), F32),
                        pltpu.VMEM((LANES, TM_MOE), F32),
                        pltpu.VMEM((TM_MOE, D_MODEL), F32)],
        compiler_params=_cparams(2),
        name="moe_ffn",
    )(x, comb, sel, tri, eg, eu, ed, g2.reshape(1, -1), b2.reshape(1, -1))


def kernel(x_prompt, x_sample, cache_meta_k, cache_meta_v, cache_k, cache_v, state_conv,
           meta_tokens, ln_in_g, ln_in_b, w_in, w_dw, b_dw, conv_ln_g, conv_ln_b, sinks, w_out,
           ln1_g, ln1_b, ln2_g, ln2_b, ffd_w_gate, ffd_w_up, ffd_w_down,
           router_w, router_b, exp_w_gate, exp_w_up, exp_w_down):
    meta_blk = jnp.concatenate([meta_tokens, jnp.zeros((BLOCK - N_META, D_MODEL), F32)], 0)
    x_tail = jnp.concatenate([meta_blk] * BATCH + [x_sample.reshape(NS, D_MODEL),
                                                   jnp.zeros((T - PAD0, D_MODEL), F32)], 0)
    cos_np, sin_np = _rope_tables()
    cos_t, sin_t = jnp.asarray(cos_np), jnp.asarray(sin_np)
    tri = jnp.asarray(np.tril(np.ones((TM_MOE, TM_MOE), np.float32), -1), dtype=BF)
    bias = jnp.asarray(_attn_bias())
    w_in_b, w_out_b = w_in.astype(BF), w_out.astype(BF)
    ffd_b = (ffd_w_gate.astype(BF), ffd_w_up.astype(BF), ffd_w_down.astype(BF))
    exp_b = (exp_w_gate.astype(BF), exp_w_up.astype(BF), exp_w_down.astype(BF))

    per_kv = SAMPLE_ROWS // N_KV_HEADS
    head_of_row = (jnp.arange(SAMPLE_ROWS) // per_kv) * 4 + jnp.arange(SAMPLE_ROWS) % 4
    met_rows = lambda x, b: x[MET0 + b * BLOCK:MET0 + b * BLOCK + N_META]
    kv5 = lambda rows: jnp.stack(rows).reshape(BATCH, -1, N_KV_HEADS, HEAD_DIM)

    pmk, pmv, pk, pv, pc, sk, sv, sc = [], [], [], [], [], [], [], []
    y_tok = y_tail = None
    for i in range(DEPTH):
        if i == 0:
            h, q, k, v, u = _inproj0_call(x_prompt.reshape(TOK, D_MODEL), x_tail, ln_in_g,
                                          ln_in_b, w_in_b, cos_t, sin_t)
        else:
            q, k, v, u = _inproj_call(h, w_in_b, i, cos_t, sin_t)

        a = _attn_call(sinks[i], bias, q, k, v)
        c = _pconv_call(u, w_dw[i], b_dw[i], conv_ln_g[i], conv_ln_b[i])
        c_m = _mconv_call(u, w_dw[i], b_dw[i], conv_ln_g[i], conv_ln_b[i])
        pmk.append(kv5([met_rows(k, b) for b in range(BATCH)]))
        pmv.append(kv5([met_rows(v, b) for b in range(BATCH)]))
        pk.append(kv5([k[(b + 1) * SEQ - WINDOW:(b + 1) * SEQ] for b in range(BATCH)]))
        pv.append(kv5([v[(b + 1) * SEQ - WINDOW:(b + 1) * SEQ] for b in range(BATCH)]))
        pc.append(jnp.stack([u[(b + 1) * SEQ - CONV_STATE:(b + 1) * SEQ] for b in range(BATCH)]))

        k2 = k[SMP0:PAD0].reshape(DEC_BATCH, DEC_SEQ, KV_WIDTH)
        v2 = v[SMP0:PAD0].reshape(DEC_BATCH, DEC_SEQ, KV_WIDTH)
        k_loc = jnp.concatenate([cache_k[i].reshape(DEC_BATCH, WINDOW, KV_WIDTH), k2], 1)
        v_loc = jnp.concatenate([cache_v[i].reshape(DEC_BATCH, WINDOW, KV_WIDTH), v2], 1)
        q2 = q[SMP0:PAD0].reshape(DEC_BATCH, DEC_SEQ, N_KV_HEADS, 4, HEAD_DIM)
        q2 = q2.transpose(0, 2, 1, 3, 4).reshape(DEC_BATCH, N_KV_HEADS, per_kv, HEAD_DIM)
        zq = jnp.zeros_like(q2[:, 0])
        qz = jnp.concatenate([jnp.concatenate([q2[:, 0], zq], -1),
                              jnp.concatenate([zq, q2[:, 1]], -1)], 1)
        sink_rows = jnp.broadcast_to(sinks[i][head_of_row][:, None], (SAMPLE_ROWS, LANES))
        o2 = _sattn_call(sink_rows, qz, k_loc,
                         cache_meta_k[i].reshape(DEC_BATCH, N_META, KV_WIDTH), v_loc,
                         cache_meta_v[i].reshape(DEC_BATCH, N_META, KV_WIDTH))
        o2 = jnp.stack([o2[:, :per_kv, :HEAD_DIM], o2[:, per_kv:, HEAD_DIM:]], 1)
        a_s = o2.reshape(DEC_BATCH, N_KV_HEADS, DEC_SEQ, 4, HEAD_DIM).transpose(0, 2, 1, 3, 4)
        a_s = a_s.reshape(NS, ATTN_WIDTH).astype(BF)
        u2 = u[SMP0:PAD0]
        c_s = _sconv_call(state_conv[i].reshape(DEC_BATCH * CONV_STATE, CONV_CH), u2,
                          w_dw[i], b_dw[i], conv_ln_g[i], conv_ln_b[i])
        c_s = c_s.transpose(1, 0, 2).reshape(NS, CONV_CH)
        sk.append(k_loc[:, DEC_SEQ:].reshape(DEC_BATCH, WINDOW, N_KV_HEADS, HEAD_DIM))
        sv.append(v_loc[:, DEC_SEQ:].reshape(DEC_BATCH, WINDOW, N_KV_HEADS, HEAD_DIM))
        sc.append(jnp.concatenate([state_conv[i][:, DEC_SEQ:],
                                   u2.reshape(DEC_BATCH, DEC_SEQ, CONV_CH)], 1))

        a = lax.dynamic_update_slice(a, a_s, (SMP0, 0))
        c = lax.dynamic_update_slice(c, jnp.concatenate([c_m, c_s], 0), (MET0, 0))
        j = i // 2
        if i % 2 == 0:
            h = _dense_call(a, c, h, w_out_b, i, ln1_g[i], ln1_b[i], *ffd_b, j,
                            ln2_g[i], ln2_b[i])
        else:
            rw = jnp.pad(router_w[j], ((0, 0), (0, LANES - N_EXPERTS)))
            rwh = rw.astype(BF)
            rwl = (rw - rwh.astype(F32)).astype(BF)
            rb = jnp.pad(router_b[j], (0, LANES - N_EXPERTS)).reshape(1, LANES)
            h1, comb, sel = _router_call(a, c, h, w_out_b, i, ln1_g[i], ln1_b[i], rwh, rwl, rb)
            moe = functools.partial(_moe_call, h1, comb, sel, tri, *exp_b, j,
                                    ln2_g[i], ln2_b[i])
            if i < DEPTH - 1:
                h = moe(0, T // TM_MOE)
            else:
                y_tok = moe(0, TOK // TM_MOE)
                y_tail = moe(TOK // TM_MOE, TAIL // TM_MOE)

    y_prompt = y_tok.reshape(BATCH, SEQ, D_MODEL)
    y_sample = y_tail[SMP0 - TOK:PAD0 - TOK].reshape(DEC_BATCH, DEC_SEQ, D_MODEL)
    return (y_prompt, y_sample, jnp.stack(pmk), jnp.stack(pmv), jnp.stack(pk), jnp.stack(pv),
            jnp.stack(pc), jnp.stack(sk), jnp.stack(sv), jnp.stack(sc))
```

```python
import functools

import numpy as np
import jax
import jax.numpy as jnp
from jax import lax
from jax.experimental import pallas as pl
from jax.experimental.pallas import tpu as pltpu

D_MODEL = 1024
BATCH = 2
SEQ = 8192
DEPTH = 4
DEC_BATCH = 128
DEC_SEQ = 4
PAST_LEN = 8192
N_META = 16
HEAD_DIM = 64
N_Q_HEADS = 8
N_KV_HEADS = 2
ATTN_WIDTH = N_Q_HEADS * HEAD_DIM
KV_WIDTH = N_KV_HEADS * HEAD_DIM
CONV_CH = D_MODEL - ATTN_WIDTH
Q_END = ATTN_WIDTH
K_END = Q_END + KV_WIDTH
V_END = K_END + KV_WIDTH
IN_COLS = V_END + 2 * CONV_CH
CONV_WIDTH = 31
CONV_STATE = CONV_WIDTH - 1
WINDOW = 128
BLOCK = 128
ROPE_THETA = 10000.0
ATTN_SCALE = HEAD_DIM ** -0.5
D_FF_DENSE = 2816
N_EXPERTS = 8
D_FF_EXPERT = 1024
ALPHA = (2 * DEPTH) ** 0.25
LN_EPS = 1e-5

LANES = 128
SUBLANES = 8
TOK = BATCH * SEQ
MET0 = TOK
SMP0 = MET0 + BATCH * BLOCK
NS = DEC_BATCH * DEC_SEQ
PAD0 = SMP0 + NS
TM = 512
TM_MOE = 1024
T = -(-PAD0 // TM_MOE) * TM_MOE
TAIL = T - TOK
ATT_Q = 2 * BLOCK
ATT_STEPS = SEQ // ATT_Q
W_LOC = WINDOW + DEC_SEQ
N_KEYS = 2 * BLOCK + N_META
CONV_HALO = 32
CONV_CHUNK = 64
SAMPLE_BT = 16
MOE_CHUNK = 128
VMEM_LIMIT = 56 * 1024 * 1024

BF = jnp.bfloat16
F32 = jnp.float32


def _cparams(n_axes):
    return pltpu.CompilerParams(dimension_semantics=("arbitrary",) * n_axes,
                                vmem_limit_bytes=VMEM_LIMIT)


def _ln_rows(x, g, b):
    mu = jnp.mean(x, -1, keepdims=True)
    xc = x - mu
    var = jnp.mean(xc * xc, -1, keepdims=True)
    return xc * lax.rsqrt(var + LN_EPS) * g + b


def _silu(x):
    return x * jax.nn.sigmoid(x)


def _vec(n):
    return pl.BlockSpec((1, n), lambda *_: (0, 0))


def _ln_kernel(xt_ref, xr_ref, g_ref, b_ref, o_ref):
    i = pl.program_id(0)

    @pl.when(i < TOK // TM)
    def _():
        o_ref[...] = _ln_rows(xt_ref[...], g_ref[...], b_ref[...])

    @pl.when(i >= TOK // TM)
    def _():
        o_ref[...] = _ln_rows(xr_ref[...], g_ref[...], b_ref[...])


def _project(h_ref, w_ref, cos_ref, sin_ref, q_ref, k_ref, v_ref, u_ref):
    hb = h_ref[...].astype(BF)
    cos = cos_ref[...]
    sin = sin_ref[...]
    lane = lax.broadcasted_iota(jnp.int32, (TM, LANES), 1)
    first_half = (lane % HEAD_DIM) < (HEAD_DIM // 2)

    def rope(x):
        rot = jnp.where(first_half, pltpu.roll(x, LANES - HEAD_DIM // 2, 1),
                        pltpu.roll(x, HEAD_DIM // 2, 1))
        return x * cos + rot * sin

    qk = jnp.dot(hb, w_ref[0, :, :K_END], preferred_element_type=F32)
    for c in range(Q_END // LANES):
        q_ref[:, c * LANES:(c + 1) * LANES] = (
            rope(qk[:, c * LANES:(c + 1) * LANES]) * ATTN_SCALE).astype(BF)
    k_ref[...] = rope(qk[:, Q_END:K_END])
    v_ref[...] = jnp.dot(hb, w_ref[0, :, K_END:V_END], preferred_element_type=F32)
    val = jnp.dot(hb, w_ref[0, :, V_END:V_END + CONV_CH], preferred_element_type=F32)
    gate = jnp.dot(hb, w_ref[0, :, V_END + CONV_CH:], preferred_element_type=F32)
    u_ref[...] = val * jax.nn.sigmoid(gate)


def _inproj_kernel(h_ref, w_ref, cos_ref, sin_ref, q_ref, k_ref, v_ref, u_ref):
    _project(h_ref, w_ref, cos_ref, sin_ref, q_ref, k_ref, v_ref, u_ref)


def _inproj0_kernel(xt_ref, xr_ref, g_ref, b_ref, w_ref, cos_ref, sin_ref,
                    h_ref, q_ref, k_ref, v_ref, u_ref):
    _ln_kernel(xt_ref, xr_ref, g_ref, b_ref, h_ref)
    _project(h_ref, w_ref, cos_ref, sin_ref, q_ref, k_ref, v_ref, u_ref)


def _rope_tables():
    half = HEAD_DIM // 2
    pos = np.zeros((SEQ + TAIL,), np.float64)
    pos[:SEQ] = N_META + np.arange(SEQ)
    for b in range(BATCH):
        pos[SEQ + b * BLOCK:SEQ + b * BLOCK + N_META] = np.arange(N_META)
    s0 = SEQ + SMP0 - MET0
    pos[s0:s0 + NS] = PAST_LEN + np.tile(np.arange(DEC_SEQ), DEC_BATCH)
    inv = ROPE_THETA ** (-np.arange(half, dtype=np.float64) / half)
    ang = pos[:, None] * inv[None, :]
    cos = np.cos(ang).astype(np.float32)
    sin = np.sin(ang).astype(np.float32)
    return np.tile(cos, (1, 4)), np.tile(np.concatenate([-sin, sin], -1), (1, 2))


def _inproj_specs(layer):
    row = lambda i: (i, 0)
    per_batch = SEQ // TM
    tab = lambda i: (jnp.where(i < 2 * per_batch, i % per_batch, i - per_batch), 0)
    in_specs = [pl.BlockSpec((1, D_MODEL, IN_COLS), lambda i: (layer, 0, 0)),
                pl.BlockSpec((TM, LANES), tab),
                pl.BlockSpec((TM, LANES), tab)]
    out_specs = [pl.BlockSpec((TM, ATTN_WIDTH), row),
                 pl.BlockSpec((TM, KV_WIDTH), row),
                 pl.BlockSpec((TM, KV_WIDTH), row),
                 pl.BlockSpec((TM, CONV_CH), row)]
    out_shape = [jax.ShapeDtypeStruct((T, ATTN_WIDTH), BF),
                 jax.ShapeDtypeStruct((T, KV_WIDTH), F32),
                 jax.ShapeDtypeStruct((T, KV_WIDTH), F32),
                 jax.ShapeDtypeStruct((T, CONV_CH), F32)]
    return in_specs, out_specs, out_shape


def _inproj_call(h, w_bf, layer, cos_t, sin_t):
    in_specs, out_specs, out_shape = _inproj_specs(layer)
    return pl.pallas_call(
        _inproj_kernel,
        grid=(T // TM,),
        in_specs=[pl.BlockSpec((TM, D_MODEL), lambda i: (i, 0))] + in_specs,
        out_specs=out_specs,
        out_shape=out_shape,
        compiler_params=_cparams(1),
        name="inproj",
    )(h, w_bf, cos_t, sin_t)


def _inproj0_call(x_tok, x_tail, g, b, w_bf, cos_t, sin_t):
    in_specs, out_specs, out_shape = _inproj_specs(0)
    n_tok = TOK // TM
    row = pl.BlockSpec((TM, D_MODEL), lambda i: (i, 0))
    return pl.pallas_call(
        _inproj0_kernel,
        grid=(T // TM,),
        in_specs=[pl.BlockSpec((TM, D_MODEL), lambda i: (jnp.minimum(i, n_tok - 1), 0)),
                  pl.BlockSpec((TM, D_MODEL), lambda i: (jnp.maximum(i - n_tok, 0), 0)),
                  _vec(D_MODEL), _vec(D_MODEL)] + in_specs,
        out_specs=[row] + out_specs,
        out_shape=[jax.ShapeDtypeStruct((T, D_MODEL), F32)] + out_shape,
        compiler_params=_cparams(1),
        name="ln_inproj",
    )(x_tok, x_tail, g.reshape(1, D_MODEL), b.reshape(1, D_MODEL), w_bf, cos_t, sin_t)


def _head_variants(x):
    lane = lax.broadcasted_iota(jnp.int32, x.shape, x.ndim - 1)
    lo = lane < HEAD_DIM
    xr = pltpu.roll(x, HEAD_DIM, x.ndim - 1)
    zero = jnp.zeros_like(x)
    return [[jnp.where(lo, x, zero).astype(BF), jnp.where(lo, zero, xr).astype(BF)],
            [jnp.where(lo, xr, zero).astype(BF), jnp.where(lo, zero, x).astype(BF)]]


def _attn_bias():
    qi = np.arange(2 * BLOCK)[:, None] % BLOCK
    col = np.arange(N_KEYS)[None, :]
    d = BLOCK + qi - col
    band = (col < 2 * BLOCK) & (d >= 0) & (d <= WINDOW)
    is_meta = col >= 2 * BLOCK
    normal = band | is_meta
    first = (band & (col >= BLOCK)) | is_meta
    meta_q = is_meta & ((col - 2 * BLOCK) <= qi)
    return np.where(np.stack([normal, first, meta_q]), 0.0, -np.inf).astype(np.float32)


def _attend(sinks_ref, bias_ref, variant, q_ref, kp_ref, kc_ref, km_refs, vp_ref, vc_ref,
            vm_refs, o_ref):
    kc = kc_ref[...]
    vc = vc_ref[...]
    keys = [jnp.concatenate([kp_ref[...], kc[:BLOCK], km_refs[0][...]], axis=0),
            jnp.concatenate([kc[:BLOCK], kc[BLOCK:], km_refs[1][...]], axis=0)]
    vals = [jnp.concatenate([vp_ref[...], vc[:BLOCK], vm_refs[0][...]], axis=0),
            jnp.concatenate([vc[:BLOCK], vc[BLOCK:], vm_refs[1][...]], axis=0)]
    upper = lax.broadcasted_iota(jnp.int32, (2 * BLOCK, 1), 0) < BLOCK

    for h in range(2):
        rows = slice(h * BLOCK, (h + 1) * BLOCK)
        kvar = _head_variants(keys[h])
        vvar = _head_variants(vals[h])
        bias = bias_ref[variant[h]]
        for g in range(N_KV_HEADS):
            qg = jnp.concatenate([q_ref[rows, (2 * g) * LANES:(2 * g + 1) * LANES],
                                  q_ref[rows, (2 * g + 1) * LANES:(2 * g + 2) * LANES]],
                                 axis=0)
            acc = jnp.zeros((2 * BLOCK, LANES), F32)
            for e in range(2):
                s = lax.dot_general(qg, kvar[g][e], (((1,), (1,)), ((), ())),
                                    preferred_element_type=F32) + bias
                sk = jnp.where(upper, sinks_ref[4 * g + e], sinks_ref[4 * g + 2 + e])
                m = jnp.maximum(jnp.max(s, -1, keepdims=True), sk)
                p = jnp.exp(s - m)
                den = jnp.sum(p, -1, keepdims=True) + jnp.exp(sk - m)
                pn = (p * (1.0 / den)).astype(BF)
                acc = acc + jnp.dot(pn, vvar[g][e], preferred_element_type=F32)
            o_ref[rows, (2 * g) * LANES:(2 * g + 1) * LANES] = acc[:BLOCK].astype(BF)
            o_ref[rows, (2 * g + 1) * LANES:(2 * g + 2) * LANES] = acc[BLOCK:].astype(BF)


def _prompt_kernel(sinks_ref, bias_ref, q_ref, kp_ref, kc_ref, km0_ref, km1_ref,
                   vp_ref, vc_ref, vm0_ref, vm1_ref, up_ref, uc_ref, w_ref, bdw_ref, g_ref,
                   b_ref, a_ref, c_ref, ext_ref, sh_ref, y_ref):
    s_id = pl.program_id(0)
    n_tok = BATCH * ATT_STEPS
    attn_refs = (q_ref, kp_ref, kc_ref, (km0_ref, km1_ref), vp_ref, vc_ref, (vm0_ref, vm1_ref),
                 a_ref)

    @pl.when(s_id > n_tok)
    def _():
        a_ref[...] = jnp.zeros_like(a_ref)
        c_ref[...] = jnp.zeros_like(c_ref)

    @pl.when(s_id == n_tok)
    def _():
        _attend(sinks_ref, bias_ref, (2, 2), *attn_refs)
        c_ref[...] = jnp.zeros_like(c_ref)

    @pl.when(s_id < n_tok)
    def _():
        first = (s_id % ATT_STEPS) == 0
        _attend(sinks_ref, bias_ref, (jnp.where(first, 1, 0), 0), *attn_refs)
        half = CONV_HALO // 2
        ext_ref[:half] = jnp.where(first, 0.0, up_ref[:half])
        ext_ref[half:CONV_HALO] = jnp.where(first, up_ref[:half], up_ref[half:])
        ext_ref[CONV_HALO:] = uc_ref[...]
        _conv_rows(ext_ref, sh_ref, y_ref, w_ref, bdw_ref, g_ref, b_ref, c_ref, ATT_Q)


def _prompt_call(sinks, bias, q, k, v, u, w_dw, b_dw, g, b):
    n_tok = BATCH * ATT_STEPS
    cur = lambda s: (jnp.minimum(s, n_tok), 0)
    fixed = lambda s: (0, 0)

    def prev(s):
        return (jnp.minimum(jnp.where(s % ATT_STEPS == 0, 2 * s, 2 * s - 1), 2 * n_tok), 0)

    def meta(h):
        def index(s):
            b = jnp.where(s < n_tok, s // ATT_STEPS, jnp.where(s == n_tok, h, 0))
            return ((MET0 + b * BLOCK) // N_META, 0)
        return index

    def u_prev(s):
        first = (MET0 + (s // ATT_STEPS) * BLOCK) // CONV_HALO
        tok = s * (ATT_Q // CONV_HALO) - 1
        return (jnp.where(s < n_tok, jnp.where(s % ATT_STEPS == 0, first, tok), 0), 0)

    return pl.pallas_call(
        _prompt_kernel,
        grid=(T // ATT_Q,),
        in_specs=[pl.BlockSpec(memory_space=pltpu.SMEM),
                  pl.BlockSpec((3, ATT_Q, N_KEYS), lambda s: (0, 0, 0)),
                  pl.BlockSpec((ATT_Q, ATTN_WIDTH), cur),
                  pl.BlockSpec((BLOCK, KV_WIDTH), prev),
                  pl.BlockSpec((ATT_Q, KV_WIDTH), cur),
                  pl.BlockSpec((N_META, KV_WIDTH), meta(0)),
                  pl.BlockSpec((N_META, KV_WIDTH), meta(1)),
                  pl.BlockSpec((BLOCK, KV_WIDTH), prev),
                  pl.BlockSpec((ATT_Q, KV_WIDTH), cur),
                  pl.BlockSpec((N_META, KV_WIDTH), meta(0)),
                  pl.BlockSpec((N_META, KV_WIDTH), meta(1)),
                  pl.BlockSpec((CONV_HALO, CONV_CH), u_prev),
                  pl.BlockSpec((ATT_Q, CONV_CH), cur),
                  pl.BlockSpec((CONV_WIDTH, CONV_CH), fixed),
                  _vec(CONV_CH), _vec(CONV_CH), _vec(CONV_CH)],
        out_specs=[pl.BlockSpec((ATT_Q, ATTN_WIDTH), lambda s: (s, 0)),
                   pl.BlockSpec((ATT_Q, CONV_CH), lambda s: (s, 0))],
        out_shape=[jax.ShapeDtypeStruct((T, ATTN_WIDTH), BF),
                   jax.ShapeDtypeStruct((T, CONV_CH), BF)],
        scratch_shapes=_conv_scratch(ATT_Q),
        compiler_params=_cparams(1),
        name="prompt_attn_conv",
    )(sinks, bias, q, k, k, k, k, v, v, v, v, u, u, w_dw, b_dw.reshape(1, CONV_CH),
      g.reshape(1, CONV_CH), b.reshape(1, CONV_CH))


SAMPLE_ROWS = N_KV_HEADS * DEC_SEQ * (N_Q_HEADS // N_KV_HEADS)


def _sattn_kernel(sk_ref, q_ref, kl_ref, km_ref, vl_ref, vm_ref, o_ref):
    q = q_ref[...]
    nt = (((2,), (2,)), ((0,), (0,)))
    s_loc = lax.dot_general(q, kl_ref[...].astype(BF), nt, preferred_element_type=F32)
    s_meta = lax.dot_general(q, km_ref[...].astype(BF), nt, preferred_element_type=F32)
    r = lax.broadcasted_iota(jnp.int32, (1, SAMPLE_ROWS, W_LOC), 1)
    idx = lax.broadcasted_iota(jnp.int32, (1, SAMPLE_ROWS, W_LOC), 2)
    sq = (r % (SAMPLE_ROWS // N_KV_HEADS)) // (N_Q_HEADS // N_KV_HEADS)
    ok = (idx >= sq) & (idx <= WINDOW + sq)
    s_loc = jnp.where(ok, s_loc, -jnp.inf)
    sk = sk_ref[...][None, :, :1]
    m = jnp.maximum(jnp.maximum(jnp.max(s_loc, -1, keepdims=True),
                                jnp.max(s_meta, -1, keepdims=True)), sk)
    p_loc = jnp.exp(s_loc - m)
    p_meta = jnp.exp(s_meta - m)
    den = (jnp.sum(p_loc, -1, keepdims=True) + jnp.sum(p_meta, -1, keepdims=True)
           + jnp.exp(sk - m))
    inv = 1.0 / den
    nn = (((2,), (1,)), ((0,), (0,)))
    o_ref[...] = (
        lax.dot_general((p_loc * inv).astype(BF), vl_ref[...].astype(BF), nn,
                        preferred_element_type=F32)
        + lax.dot_general((p_meta * inv).astype(BF), vm_ref[...].astype(BF), nn,
                          preferred_element_type=F32))


def _sattn_call(sink_rows, qz, k_loc, mk, v_loc, mv):
    b3 = lambda i: (i, 0, 0)
    return pl.pallas_call(
        _sattn_kernel,
        grid=(DEC_BATCH // SAMPLE_BT,),
        in_specs=[pl.BlockSpec((SAMPLE_ROWS, LANES), lambda i: (0, 0)),
                  pl.BlockSpec((SAMPLE_BT, SAMPLE_ROWS, LANES), b3),
                  pl.BlockSpec((SAMPLE_BT, W_LOC, KV_WIDTH), b3),
                  pl.BlockSpec((SAMPLE_BT, N_META, KV_WIDTH), b3),
                  pl.BlockSpec((SAMPLE_BT, W_LOC, KV_WIDTH), b3),
                  pl.BlockSpec((SAMPLE_BT, N_META, KV_WIDTH), b3)],
        out_specs=pl.BlockSpec((SAMPLE_BT, SAMPLE_ROWS, LANES), b3),
        out_shape=jax.ShapeDtypeStruct((DEC_BATCH, SAMPLE_ROWS, LANES), F32),
        compiler_params=_cparams(1),
        name="sample_attn",
    )(sink_rows, qz, k_loc, mk, v_loc, mv)


def _conv_tail(y, g, b):
    return _silu(_ln_rows(y, g, b))


def _conv_rows(ext_ref, sh_ref, y_ref, w_ref, bdw_ref, g_ref, b_ref, o_ref, n_rows):
    off = CONV_HALO - CONV_STATE
    span = n_rows + CONV_HALO - SUBLANES
    for s in range(1, SUBLANES):
        sh_ref[s - 1, :span, :] = ext_ref[s:s + span, :]
    for lg in range(CONV_CH // LANES):
        lanes = slice(lg * LANES, (lg + 1) * LANES)
        for r0 in range(0, n_rows, CONV_CHUNK):
            acc = jnp.broadcast_to(bdw_ref[:, lanes], (CONV_CHUNK, LANES))
            for j in range(CONV_WIDTH):
                a, s = divmod(off + j, SUBLANES)
                src = ext_ref if s == 0 else sh_ref.at[s - 1]
                x = src[r0 + a * SUBLANES:r0 + a * SUBLANES + CONV_CHUNK, lanes]
                acc = acc + x * w_ref[j:j + 1, lanes]
            y_ref[r0:r0 + CONV_CHUNK, lanes] = acc
    for r0 in range(0, n_rows, CONV_CHUNK):
        y = y_ref[r0:r0 + CONV_CHUNK, :]
        o_ref[r0:r0 + CONV_CHUNK, :] = _conv_tail(y, g_ref[...], b_ref[...]).astype(BF)


def _conv_scratch(n_rows):
    return [pltpu.VMEM((CONV_HALO + n_rows, CONV_CH), F32),
            pltpu.VMEM((SUBLANES - 1, CONV_HALO + n_rows - SUBLANES, CONV_CH), F32),
            pltpu.VMEM((n_rows, CONV_CH), F32)]


def _mconv_kernel(cur_ref, w_ref, bdw_ref, g_ref, b_ref, o_ref, ext_ref, sh_ref, y_ref):
    ext_ref[:CONV_HALO] = jnp.zeros((CONV_HALO, CONV_CH), F32)
    ext_ref[CONV_HALO:] = cur_ref[...]
    _conv_rows(ext_ref, sh_ref, y_ref, w_ref, bdw_ref, g_ref, b_ref, o_ref, BLOCK)


def _mconv_call(u, w_dw, b_dw, g, b):
    fixed = lambda bb: (0, 0)
    return pl.pallas_call(
        _mconv_kernel,
        grid=(BATCH,),
        in_specs=[pl.BlockSpec((BLOCK, CONV_CH), lambda bb: (MET0 // BLOCK + bb, 0)),
                  pl.BlockSpec((CONV_WIDTH, CONV_CH), fixed),
                  _vec(CONV_CH), _vec(CONV_CH), _vec(CONV_CH)],
        out_specs=pl.BlockSpec((BLOCK, CONV_CH), lambda bb: (bb, 0)),
        out_shape=jax.ShapeDtypeStruct((BATCH * BLOCK, CONV_CH), BF),
        scratch_shapes=_conv_scratch(BLOCK),
        compiler_params=_cparams(1),
        name="meta_conv",
    )(u, w_dw, b_dw.reshape(1, CONV_CH), g.reshape(1, CONV_CH), b.reshape(1, CONV_CH))


def _sconv_kernel(st_ref, u_ref, w_ref, bdw_ref, g_ref, b_ref, o_ref, y_ref):
    l = pl.program_id(0)

    def ext_row(r):
        if r < CONV_STATE:
            return st_ref[pl.ds(r, DEC_BATCH, stride=CONV_STATE), :]
        return u_ref[pl.ds(r - CONV_STATE, DEC_BATCH, stride=DEC_SEQ), :]

    rows = [ext_row(r) for r in range(CONV_STATE + DEC_SEQ)]
    for s in range(DEC_SEQ):
        acc = jnp.broadcast_to(bdw_ref[...], (DEC_BATCH, LANES))
        for j in range(CONV_WIDTH):
            acc = acc + rows[s + j] * w_ref[j:j + 1, :]
        y_ref[l, s] = acc

    @pl.when(l == CONV_CH // LANES - 1)
    def _():
        for s in range(DEC_SEQ):
            y = jnp.concatenate([y_ref[c, s] for c in range(CONV_CH // LANES)], axis=-1)
            o_ref[s] = _conv_tail(y, g_ref[...], b_ref[...]).astype(BF)


def _sconv_call(state, u, w_dw, b_dw, g, b):
    grp = lambda rows: pl.BlockSpec((rows, LANES), lambda l: (0, l))
    full = lambda shape: pl.BlockSpec(shape, lambda l: (0,) * len(shape))
    return pl.pallas_call(
        _sconv_kernel,
        grid=(CONV_CH // LANES,),
        in_specs=[grp(DEC_BATCH * CONV_STATE), grp(NS), grp(CONV_WIDTH), grp(1),
                  full((1, CONV_CH)), full((1, CONV_CH))],
        out_specs=full((DEC_SEQ, DEC_BATCH, CONV_CH)),
        out_shape=jax.ShapeDtypeStruct((DEC_SEQ, DEC_BATCH, CONV_CH), BF),
        scratch_shapes=[pltpu.VMEM((CONV_CH // LANES, DEC_SEQ, DEC_BATCH, LANES), F32)],
        compiler_params=_cparams(1),
        name="sample_conv",
    )(state, u, w_dw, b_dw.reshape(1, CONV_CH), g.reshape(1, CONV_CH), b.reshape(1, CONV_CH))


def _mix(a_ref, c_ref, h_ref, wo_ref, g1_ref, b1_ref):
    mix = (jnp.dot(a_ref[...], wo_ref[0, :ATTN_WIDTH, :], preferred_element_type=F32)
           + jnp.dot(c_ref[...], wo_ref[0, ATTN_WIDTH:, :], preferred_element_type=F32))
    return _ln_rows(ALPHA * h_ref[...] + mix, g1_ref[...], b1_ref[...])


def _dense_kernel(a_ref, c_ref, h_ref, wo_ref, g1_ref, b1_ref, wg_ref, wu_ref, wd_ref,
                  g2_ref, b2_ref, o_ref):
    h1 = _mix(a_ref, c_ref, h_ref, wo_ref, g1_ref, b1_ref)
    hb = h1.astype(BF)
    gate = jnp.dot(hb, wg_ref[0], preferred_element_type=F32)
    up = jnp.dot(hb, wu_ref[0], preferred_element_type=F32)
    f = jnp.dot((_silu(gate) * up).astype(BF), wd_ref[0], preferred_element_type=F32)
    o_ref[...] = _ln_rows(ALPHA * h1 + f, g2_ref[...], b2_ref[...])


def _dense_call(a, c, h, wo, layer, g1, b1, wg, wu, wd, j, g2, b2):
    row = lambda i: (i, 0)
    once = dict(pipeline_mode=pl.Buffered(1))
    return pl.pallas_call(
        _dense_kernel,
        grid=(T // TM,),
        in_specs=[pl.BlockSpec((TM, ATTN_WIDTH), row),
                  pl.BlockSpec((TM, CONV_CH), row),
                  pl.BlockSpec((TM, D_MODEL), row),
                  pl.BlockSpec((1, D_MODEL, D_MODEL), lambda i: (layer, 0, 0), **once),
                  _vec(D_MODEL), _vec(D_MODEL),
                  pl.BlockSpec((1, D_MODEL, D_FF_DENSE), lambda i: (j, 0, 0), **once),
                  pl.BlockSpec((1, D_MODEL, D_FF_DENSE), lambda i: (j, 0, 0), **once),
                  pl.BlockSpec((1, D_FF_DENSE, D_MODEL), lambda i: (j, 0, 0), **once),
                  _vec(D_MODEL), _vec(D_MODEL)],
        out_specs=pl.BlockSpec((TM, D_MODEL), row),
        out_shape=jax.ShapeDtypeStruct((T, D_MODEL), F32),
        compiler_params=_cparams(1),
        name="mix_dense_ffn",
    )(a, c, h, wo, g1.reshape(1, -1), b1.reshape(1, -1), wg, wu, wd,
      g2.reshape(1, -1), b2.reshape(1, -1))


def _router_kernel(a_ref, c_ref, h_ref, wo_ref, g1_ref, b1_ref, rwh_ref, rwl_ref, rb_ref,
                   h1_ref, comb_ref, sel_ref):
    h1 = _mix(a_ref, c_ref, h_ref, wo_ref, g1_ref, b1_ref)
    h1_ref[...] = h1
    xh = h1.astype(BF)
    xl = (h1 - xh.astype(F32)).astype(BF)
    logits = (jnp.dot(xh, rwh_ref[...], preferred_element_type=F32)
              + jnp.dot(xl, rwh_ref[...], preferred_element_type=F32)
              + jnp.dot(xh, rwl_ref[...], preferred_element_type=F32)) + rb_ref[...]
    lane = lax.broadcasted_iota(jnp.int32, (TM, LANES), 1)
    logits = jnp.where(lane < N_EXPERTS, logits, -jnp.inf)
    v1 = jnp.max(logits, -1, keepdims=True)
    i1 = jnp.min(jnp.where(logits == v1, lane, LANES), -1, keepdims=True)
    rest = jnp.where(lane == i1, -jnp.inf, logits)
    v2 = jnp.max(rest, -1, keepdims=True)
    i2 = jnp.min(jnp.where(rest == v2, lane, LANES), -1, keepdims=True)
    e2 = jnp.exp(v2 - v1)
    den = 1.0 + e2
    comb_ref[...] = jnp.where(lane == i1, 1.0 / den, jnp.where(lane == i2, e2 / den, 0.0))
    sel_ref[...] = jnp.where((lane == i1) | (lane == i2), 1.0, 0.0).astype(BF)


def _router_call(a, c, h, wo, layer, g1, b1, rwh, rwl, rb):
    row = lambda i: (i, 0)
    fixed = lambda i: (0, 0)
    return pl.pallas_call(
        _router_kernel,
        grid=(T // TM,),
        in_specs=[pl.BlockSpec((TM, ATTN_WIDTH), row),
                  pl.BlockSpec((TM, CONV_CH), row),
                  pl.BlockSpec((TM, D_MODEL), row),
                  pl.BlockSpec((1, D_MODEL, D_MODEL), lambda i: (layer, 0, 0)),
                  _vec(D_MODEL), _vec(D_MODEL),
                  pl.BlockSpec((D_MODEL, LANES), fixed),
                  pl.BlockSpec((D_MODEL, LANES), fixed),
                  _vec(LANES)],
        out_specs=[pl.BlockSpec((TM, D_MODEL), row),
                   pl.BlockSpec((TM, LANES), row),
                   pl.BlockSpec((TM, LANES), row)],
        out_shape=[jax.ShapeDtypeStruct((T, D_MODEL), F32),
                   jax.ShapeDtypeStruct((T, LANES), F32),
                   jax.ShapeDtypeStruct((T, LANES), BF)],
        compiler_params=_cparams(1),
        name="mix_router",
    )(a, c, h, wo, g1.reshape(1, -1), b1.reshape(1, -1), rwh, rwl, rb)


def _moe_kernel(x_ref, comb_ref, sel_ref, tri_ref, eg_ref, eu_ref, ed_ref, g2_ref, b2_ref,
                o_ref, xb_ref, rankc_ref, rankr_ref, selr_ref, acc_ref):
    e = pl.program_id(1)
    lane = lax.broadcasted_iota(jnp.int32, (TM_MOE, LANES), 1)

    @pl.when(e == 0)
    def _():
        xb_ref[...] = x_ref[...].astype(BF)
        sel = sel_ref[...]
        rank = jnp.dot(tri_ref[...], sel, preferred_element_type=F32)
        rankc_ref[...] = rank
        rankr_ref[...] = rank.T
        selr_ref[...] = sel.astype(F32).T
        acc_ref[...] = jnp.zeros_like(acc_ref)

    onlane = lane == e
    sel_col = jnp.sum(jnp.where(onlane, sel_ref[...].astype(F32), 0.0), -1, keepdims=True)
    rank_col = jnp.sum(jnp.where(onlane, rankc_ref[...], 0.0), -1, keepdims=True)
    comb_col = jnp.sum(jnp.where(onlane, comb_ref[...], 0.0), -1, keepdims=True)
    sel_row = selr_ref[pl.ds(e, 1), :]
    rank_row = rankr_ref[pl.ds(e, 1), :]
    count = jnp.sum(sel_row).astype(jnp.int32)

    def expert_pass(first, size):
        base = first.astype(F32)
        slot_r = lax.broadcasted_iota(jnp.int32, (size, TM_MOE), 0).astype(F32)
        slot_c = lax.broadcasted_iota(jnp.int32, (TM_MOE, size), 1).astype(F32)
        take = jnp.where((rank_row - base == slot_r) & (sel_row > 0.0), 1.0, 0.0).astype(BF)
        put = jnp.where((rank_col - base == slot_c) & (sel_col > 0.0), 1.0, 0.0).astype(BF)
        xc = jnp.dot(take, xb_ref[...], preferred_element_type=F32).astype(BF)
        gate = jnp.dot(xc, eg_ref[0, 0], preferred_element_type=F32)
        up = jnp.dot(xc, eu_ref[0, 0], preferred_element_type=F32)
        out = jnp.dot((_silu(gate) * up).astype(BF), ed_ref[0, 0], preferred_element_type=F32)
        acc_ref[...] += comb_col * jnp.dot(put, out.astype(BF), preferred_element_type=F32)

    n_big = (count + MOE_CHUNK - 1) // (2 * MOE_CHUNK)

    def big_pass(ci, carry):
        expert_pass(ci * (2 * MOE_CHUNK), 2 * MOE_CHUNK)
        return carry

    lax.fori_loop(0, n_big, big_pass, 0)

    @pl.when(count > n_big * (2 * MOE_CHUNK))
    def _():
        expert_pass(n_big * (2 * MOE_CHUNK), MOE_CHUNK)

    @pl.when(e == N_EXPERTS - 1)
    def _():
        o_ref[...] = _ln_rows(ALPHA * x_ref[...] + acc_ref[...], g2_ref[...], b2_ref[...])


def _moe_call(x, comb, sel, tri, eg, eu, ed, j, g2, b2, tile0, n_tiles):
    row = lambda i, e: (tile0 + i, 0)
    fixed = lambda i, e: (0, 0)
    exp = lambda i, e: (j, e, 0, 0)
    return pl.pallas_call(
        _moe_kernel,
        grid=(n_tiles, N_EXPERTS),
        in_specs=[pl.BlockSpec((TM_MOE, D_MODEL), row),
                  pl.BlockSpec((TM_MOE, LANES), row),
                  pl.BlockSpec((TM_MOE, LANES), row),
                  pl.BlockSpec((TM_MOE, TM_MOE), fixed),
                  pl.BlockSpec((1, 1, D_MODEL, D_FF_EXPERT), exp),
                  pl.BlockSpec((1, 1, D_MODEL, D_FF_EXPERT), exp),
                  pl.BlockSpec((1, 1, D_FF_EXPERT, D_MODEL), exp),
                  _vec(D_MODEL), _vec(D_MODEL)],
        out_specs=pl.BlockSpec((TM_MOE, D_MODEL), lambda i, e: (i, 0)),
        out_shape=jax.ShapeDtypeStruct((n_tiles * TM_MOE, D_MODEL), F32),
        scratch_shapes=[pltpu.VMEM((TM_MOE, D_MODEL), BF),
                        pltpu.VMEM((TM_MOE, LANES), F32),
                        pltpu.VMEM((LANES, TM_MOE), F32),
                        pltpu.VMEM((LANES, TM_MOE), F32),
                        pltpu.VMEM((TM_MOE, D_MODEL), F32)],
        compiler_params=_cparams(2),
        name="moe_ffn",
    )(x, comb, sel, tri, eg, eu, ed, g2.reshape(1, -1), b2.reshape(1, -1))


def kernel(x_prompt, x_sample, cache_meta_k, cache_meta_v, cache_k, cache_v, state_conv,
           meta_tokens, ln_in_g, ln_in_b, w_in, w_dw, b_dw, conv_ln_g, conv_ln_b, sinks, w_out,
           ln1_g, ln1_b, ln2_g, ln2_b, ffd_w_gate, ffd_w_up, ffd_w_down,
           router_w, router_b, exp_w_gate, exp_w_up, exp_w_down):
    meta_blk = jnp.concatenate([meta_tokens, jnp.zeros((BLOCK - N_META, D_MODEL), F32)], 0)
    x_tail = jnp.concatenate([meta_blk] * BATCH + [x_sample.reshape(NS, D_MODEL),
                                                   jnp.zeros((T - PAD0, D_MODEL), F32)], 0)
    cos_np, sin_np = _rope_tables()
    cos_t, sin_t = jnp.asarray(cos_np), jnp.asarray(sin_np)
    tri = jnp.asarray(np.tril(np.ones((TM_MOE, TM_MOE), np.float32), -1), dtype=BF)
    bias = jnp.asarray(_attn_bias())
    w_in_b, w_out_b = w_in.astype(BF), w_out.astype(BF)
    ffd_b = (ffd_w_gate.astype(BF), ffd_w_up.astype(BF), ffd_w_down.astype(BF))
    exp_b = (exp_w_gate.astype(BF), exp_w_up.astype(BF), exp_w_down.astype(BF))

    per_kv = SAMPLE_ROWS // N_KV_HEADS
    head_of_row = (jnp.arange(SAMPLE_ROWS) // per_kv) * 4 + jnp.arange(SAMPLE_ROWS) % 4
    met_rows = lambda x, b: x[MET0 + b * BLOCK:MET0 + b * BLOCK + N_META]
    kv5 = lambda rows: jnp.stack(rows).reshape(BATCH, -1, N_KV_HEADS, HEAD_DIM)

    pmk, pmv, pk, pv, pc, sk, sv, sc = [], [], [], [], [], [], [], []
    y_tok = y_tail = None
    for i in range(DEPTH):
        if i == 0:
            h, q, k, v, u = _inproj0_call(x_prompt.reshape(TOK, D_MODEL), x_tail, ln_in_g,
                                          ln_in_b, w_in_b, cos_t, sin_t)
        else:
            q, k, v, u = _inproj_call(h, w_in_b, i, cos_t, sin_t)

        a, c = _prompt_call(sinks[i], bias, q, k, v, u, w_dw[i], b_dw[i], conv_ln_g[i],
                            conv_ln_b[i])
        c_m = _mconv_call(u, w_dw[i], b_dw[i], conv_ln_g[i], conv_ln_b[i])
        pmk.append(kv5([met_rows(k, b) for b in range(BATCH)]))
        pmv.append(kv5([met_rows(v, b) for b in range(BATCH)]))
        pk.append(kv5([k[(b + 1) * SEQ - WINDOW:(b + 1) * SEQ] for b in range(BATCH)]))
        pv.append(kv5([v[(b + 1) * SEQ - WINDOW:(b + 1) * SEQ] for b in range(BATCH)]))
        pc.append(jnp.stack([u[(b + 1) * SEQ - CONV_STATE:(b + 1) * SEQ] for b in range(BATCH)]))

        k2 = k[SMP0:PAD0].reshape(DEC_BATCH, DEC_SEQ, KV_WIDTH)
        v2 = v[SMP0:PAD0].reshape(DEC_BATCH, DEC_SEQ, KV_WIDTH)
        k_loc = jnp.concatenate([cache_k[i].reshape(DEC_BATCH, WINDOW, KV_WIDTH), k2], 1)
        v_loc = jnp.concatenate([cache_v[i].reshape(DEC_BATCH, WINDOW, KV_WIDTH), v2], 1)
        q2 = q[SMP0:PAD0].reshape(DEC_BATCH, DEC_SEQ, N_KV_HEADS, 4, HEAD_DIM)
        q2 = q2.transpose(0, 2, 1, 3, 4).reshape(DEC_BATCH, N_KV_HEADS, per_kv, HEAD_DIM)
        zq = jnp.zeros_like(q2[:, 0])
        qz = jnp.concatenate([jnp.concatenate([q2[:, 0], zq], -1),
                              jnp.concatenate([zq, q2[:, 1]], -1)], 1)
        sink_rows = jnp.broadcast_to(sinks[i][head_of_row][:, None], (SAMPLE_ROWS, LANES))
        o2 = _sattn_call(sink_rows, qz, k_loc,
                         cache_meta_k[i].reshape(DEC_BATCH, N_META, KV_WIDTH), v_loc,
                         cache_meta_v[i].reshape(DEC_BATCH, N_META, KV_WIDTH))
        o2 = jnp.stack([o2[:, :per_kv, :HEAD_DIM], o2[:, per_kv:, HEAD_DIM:]], 1)
        a_s = o2.reshape(DEC_BATCH, N_KV_HEADS, DEC_SEQ, 4, HEAD_DIM).transpose(0, 2, 1, 3, 4)
        a_s = a_s.reshape(NS, ATTN_WIDTH).astype(BF)
        u2 = u[SMP0:PAD0]
        c_s = _sconv_call(state_conv[i].reshape(DEC_BATCH * CONV_STATE, CONV_CH), u2,
                          w_dw[i], b_dw[i], conv_ln_g[i], conv_ln_b[i])
        c_s = c_s.transpose(1, 0, 2).reshape(NS, CONV_CH)
        sk.append(k_loc[:, DEC_SEQ:].reshape(DEC_BATCH, WINDOW, N_KV_HEADS, HEAD_DIM))
        sv.append(v_loc[:, DEC_SEQ:].reshape(DEC_BATCH, WINDOW, N_KV_HEADS, HEAD_DIM))
        sc.append(jnp.concatenate([state_conv[i][:, DEC_SEQ:],
                                   u2.reshape(DEC_BATCH, DEC_SEQ, CONV_CH)], 1))

        a = lax.dynamic_update_slice(a, a_s, (SMP0, 0))
        c = lax.dynamic_update_slice(c, jnp.concatenate([c_m, c_s], 0), (MET0, 0))
        j = i // 2
        if i % 2 == 0:
            h = _dense_call(a, c, h, w_out_b, i, ln1_g[i], ln1_b[i], *ffd_b, j,
                            ln2_g[i], ln2_b[i])
        else:
            rw = jnp.pad(router_w[j], ((0, 0), (0, LANES - N_EXPERTS)))
            rwh = rw.astype(BF)
            rwl = (rw - rwh.astype(F32)).astype(BF)
            rb = jnp.pad(router_b[j], (0, LANES - N_EXPERTS)).reshape(1, LANES)
            h1, comb, sel = _router_call(a, c, h, w_out_b, i, ln1_g[i], ln1_b[i], rwh, rwl, rb)
            moe = functools.partial(_moe_call, h1, comb, sel, tri, *exp_b, j,
                                    ln2_g[i], ln2_b[i])
            if i < DEPTH - 1:
                h = moe(0, T // TM_MOE)
            else:
                y_tok = moe(0, TOK // TM_MOE)
                y_tail = moe(TOK // TM_MOE, TAIL // TM_MOE)

    y_prompt = y_tok.reshape(BATCH, SEQ, D_MODEL)
    y_sample = y_tail[SMP0 - TOK:PAD0 - TOK].reshape(DEC_BATCH, DEC_SEQ, D_MODEL)
    return (y_prompt, y_sample, jnp.stack(pmk), jnp.stack(pmv), jnp.stack(pk), jnp.stack(pv),
            jnp.stack(pc), jnp.stack(sk), jnp.stack(sv), jnp.stack(sc))
```

```python
import functools

import numpy as np
import jax
import jax.numpy as jnp
from jax import lax
from jax.experimental import pallas as pl
from jax.experimental.pallas import tpu as pltpu

D_MODEL = 1024
BATCH = 2
SEQ = 8192
DEPTH = 4
DEC_BATCH = 128
DEC_SEQ = 4
PAST_LEN = 8192
N_META = 16
HEAD_DIM = 64
N_Q_HEADS = 8
N_KV_HEADS = 2
ATTN_WIDTH = N_Q_HEADS * HEAD_DIM
KV_WIDTH = N_KV_HEADS * HEAD_DIM
CONV_CH = D_MODEL - ATTN_WIDTH
Q_END = ATTN_WIDTH
K_END = Q_END + KV_WIDTH
V_END = K_END + KV_WIDTH
IN_COLS = V_END + 2 * CONV_CH
CONV_WIDTH = 31
CONV_STATE = CONV_WIDTH - 1
WINDOW = 128
BLOCK = 128
ROPE_THETA = 10000.0
ATTN_SCALE = HEAD_DIM ** -0.5
D_FF_DENSE = 2816
N_EXPERTS = 8
D_FF_EXPERT = 1024
ALPHA = (2 * DEPTH) ** 0.25
LN_EPS = 1e-5

LANES = 128
SUBLANES = 8
TOK = BATCH * SEQ
MET0 = TOK
SMP0 = MET0 + BATCH * BLOCK
NS = DEC_BATCH * DEC_SEQ
PAD0 = SMP0 + NS
TM = 512
TM_MOE = 1024
T = -(-PAD0 // TM_MOE) * TM_MOE
TAIL = T - TOK
ATT_Q = 2 * BLOCK
ATT_STEPS = SEQ // ATT_Q
W_LOC = WINDOW + DEC_SEQ
N_KEYS = 2 * BLOCK + N_META
CONV_HALO = 32
CONV_CHUNK = 64
SAMPLE_BT = 16
MOE_CHUNK = 128
VMEM_LIMIT = 56 * 1024 * 1024

BF = jnp.bfloat16
F32 = jnp.float32


def _cparams(n_axes):
    return pltpu.CompilerParams(dimension_semantics=("arbitrary",) * n_axes,
                                vmem_limit_bytes=VMEM_LIMIT)


def _ln_rows(x, g, b):
    mu = jnp.mean(x, -1, keepdims=True)
    xc = x - mu
    var = jnp.mean(xc * xc, -1, keepdims=True)
    return xc * lax.rsqrt(var + LN_EPS) * g + b


def _silu(x):
    return x * jax.nn.sigmoid(x)


def _vec(n):
    return pl.BlockSpec((1, n), lambda *_: (0, 0))


def _ln_kernel(xt_ref, xr_ref, g_ref, b_ref, o_ref):
    i = pl.program_id(0)

    @pl.when(i < TOK // TM)
    def _():
        o_ref[...] = _ln_rows(xt_ref[...], g_ref[...], b_ref[...])

    @pl.when(i >= TOK // TM)
    def _():
        o_ref[...] = _ln_rows(xr_ref[...], g_ref[...], b_ref[...])


def _project(h_ref, w_ref, cos_ref, sin_ref, q_ref, k_ref, v_ref, u_ref):
    hb = h_ref[...].astype(BF)
    cos = cos_ref[...]
    sin = sin_ref[...]
    lane = lax.broadcasted_iota(jnp.int32, (TM, LANES), 1)
    first_half = (lane % HEAD_DIM) < (HEAD_DIM // 2)

    def rope(x):
        rot = jnp.where(first_half, pltpu.roll(x, LANES - HEAD_DIM // 2, 1),
                        pltpu.roll(x, HEAD_DIM // 2, 1))
        return x * cos + rot * sin

    qk = jnp.dot(hb, w_ref[0, :, :K_END], preferred_element_type=F32)
    for c in range(Q_END // LANES):
        q_ref[:, c * LANES:(c + 1) * LANES] = (
            rope(qk[:, c * LANES:(c + 1) * LANES]) * ATTN_SCALE).astype(BF)
    k_ref[...] = rope(qk[:, Q_END:K_END])
    v_ref[...] = jnp.dot(hb, w_ref[0, :, K_END:V_END], preferred_element_type=F32)
    val = jnp.dot(hb, w_ref[0, :, V_END:V_END + CONV_CH], preferred_element_type=F32)
    gate = jnp.dot(hb, w_ref[0, :, V_END + CONV_CH:], preferred_element_type=F32)
    u_ref[...] = val * jax.nn.sigmoid(gate)


def _inproj_kernel(h_ref, w_ref, cos_ref, sin_ref, q_ref, k_ref, v_ref, u_ref):
    _project(h_ref, w_ref, cos_ref, sin_ref, q_ref, k_ref, v_ref, u_ref)


def _inproj0_kernel(xt_ref, xr_ref, g_ref, b_ref, w_ref, cos_ref, sin_ref,
                    h_ref, q_ref, k_ref, v_ref, u_ref):
    _ln_kernel(xt_ref, xr_ref, g_ref, b_ref, h_ref)
    _project(h_ref, w_ref, cos_ref, sin_ref, q_ref, k_ref, v_ref, u_ref)


def _rope_tables():
    half = HEAD_DIM // 2
    pos = np.zeros((SEQ + TAIL,), np.float64)
    pos[:SEQ] = N_META + np.arange(SEQ)
    for b in range(BATCH):
        pos[SEQ + b * BLOCK:SEQ + b * BLOCK + N_META] = np.arange(N_META)
    s0 = SEQ + SMP0 - MET0
    pos[s0:s0 + NS] = PAST_LEN + np.tile(np.arange(DEC_SEQ), DEC_BATCH)
    inv = ROPE_THETA ** (-np.arange(half, dtype=np.float64) / half)
    ang = pos[:, None] * inv[None, :]
    cos = np.cos(ang).astype(np.float32)
    sin = np.sin(ang).astype(np.float32)
    return np.tile(cos, (1, 4)), np.tile(np.concatenate([-sin, sin], -1), (1, 2))


def _inproj_specs(layer):
    row = lambda i: (i, 0)
    per_batch = SEQ // TM
    tab = lambda i: (jnp.where(i < 2 * per_batch, i % per_batch, i - per_batch), 0)
    in_specs = [pl.BlockSpec((1, D_MODEL, IN_COLS), lambda i: (layer, 0, 0)),
                pl.BlockSpec((TM, LANES), tab),
                pl.BlockSpec((TM, LANES), tab)]
    out_specs = [pl.BlockSpec((TM, ATTN_WIDTH), row),
                 pl.BlockSpec((TM, KV_WIDTH), row),
                 pl.BlockSpec((TM, KV_WIDTH), row),
                 pl.BlockSpec((TM, CONV_CH), row)]
    out_shape = [jax.ShapeDtypeStruct((T, ATTN_WIDTH), BF),
                 jax.ShapeDtypeStruct((T, KV_WIDTH), F32),
                 jax.ShapeDtypeStruct((T, KV_WIDTH), F32),
                 jax.ShapeDtypeStruct((T, CONV_CH), F32)]
    return in_specs, out_specs, out_shape


def _inproj_call(h, w_bf, layer, cos_t, sin_t):
    in_specs, out_specs, out_shape = _inproj_specs(layer)
    return pl.pallas_call(
        _inproj_kernel,
        grid=(T // TM,),
        in_specs=[pl.BlockSpec((TM, D_MODEL), lambda i: (i, 0))] + in_specs,
        out_specs=out_specs,
        out_shape=out_shape,
        compiler_params=_cparams(1),
        name="inproj",
    )(h, w_bf, cos_t, sin_t)


def _inproj0_call(x_tok, x_tail, g, b, w_bf, cos_t, sin_t):
    in_specs, out_specs, out_shape = _inproj_specs(0)
    n_tok = TOK // TM
    row = pl.BlockSpec((TM, D_MODEL), lambda i: (i, 0))
    return pl.pallas_call(
        _inproj0_kernel,
        grid=(T // TM,),
        in_specs=[pl.BlockSpec((TM, D_MODEL), lambda i: (jnp.minimum(i, n_tok - 1), 0)),
                  pl.BlockSpec((TM, D_MODEL), lambda i: (jnp.maximum(i - n_tok, 0), 0)),
                  _vec(D_MODEL), _vec(D_MODEL)] + in_specs,
        out_specs=[row] + out_specs,
        out_shape=[jax.ShapeDtypeStruct((T, D_MODEL), F32)] + out_shape,
        compiler_params=_cparams(1),
        name="ln_inproj",
    )(x_tok, x_tail, g.reshape(1, D_MODEL), b.reshape(1, D_MODEL), w_bf, cos_t, sin_t)


def _head_variants(x):
    lane = lax.broadcasted_iota(jnp.int32, x.shape, x.ndim - 1)
    lo = lane < HEAD_DIM
    xr = pltpu.roll(x, HEAD_DIM, x.ndim - 1)
    zero = jnp.zeros_like(x)
    return [[jnp.where(lo, x, zero).astype(BF), jnp.where(lo, zero, xr).astype(BF)],
            [jnp.where(lo, xr, zero).astype(BF), jnp.where(lo, zero, x).astype(BF)]]


def _attn_bias():
    qi = np.arange(2 * BLOCK)[:, None] % BLOCK
    col = np.arange(N_KEYS)[None, :]
    d = BLOCK + qi - col
    band = (col < 2 * BLOCK) & (d >= 0) & (d <= WINDOW)
    is_meta = col >= 2 * BLOCK
    normal = band | is_meta
    first = (band & (col >= BLOCK)) | is_meta
    meta_q = is_meta & ((col - 2 * BLOCK) <= qi)
    return np.where(np.stack([normal, first, meta_q]), 0.0, -np.inf).astype(np.float32)


def _attend(sinks_ref, bias_ref, variant, q_ref, kp_ref, kc_ref, km_refs, vp_ref, vc_ref,
            vm_refs, o_ref):
    kc = kc_ref[...]
    vc = vc_ref[...]
    keys = [jnp.concatenate([kp_ref[...], kc[:BLOCK], km_refs[0][...]], axis=0),
            jnp.concatenate([kc[:BLOCK], kc[BLOCK:], km_refs[1][...]], axis=0)]
    vals = [jnp.concatenate([vp_ref[...], vc[:BLOCK], vm_refs[0][...]], axis=0),
            jnp.concatenate([vc[:BLOCK], vc[BLOCK:], vm_refs[1][...]], axis=0)]
    upper = lax.broadcasted_iota(jnp.int32, (2 * BLOCK, 1), 0) < BLOCK

    for h in range(2):
        rows = slice(h * BLOCK, (h + 1) * BLOCK)
        kvar = _head_variants(keys[h])
        vvar = _head_variants(vals[h])
        bias = bias_ref[variant[h]]
        for g in range(N_KV_HEADS):
            qg = jnp.concatenate([q_ref[rows, (2 * g) * LANES:(2 * g + 1) * LANES],
                                  q_ref[rows, (2 * g + 1) * LANES:(2 * g + 2) * LANES]],
                                 axis=0)
            acc = jnp.zeros((2 * BLOCK, LANES), F32)
            for e in range(2):
                s = lax.dot_general(qg, kvar[g][e], (((1,), (1,)), ((), ())),
                                    preferred_element_type=F32) + bias
                sk = jnp.where(upper, sinks_ref[4 * g + e], sinks_ref[4 * g + 2 + e])
                m = jnp.maximum(jnp.max(s, -1, keepdims=True), sk)
                p = jnp.exp(s - m)
                den = jnp.sum(p, -1, keepdims=True) + jnp.exp(sk - m)
                pn = (p * (1.0 / den)).astype(BF)
                acc = acc + jnp.dot(pn, vvar[g][e], preferred_element_type=F32)
            o_ref[rows, (2 * g) * LANES:(2 * g + 1) * LANES] = acc[:BLOCK].astype(BF)
            o_ref[rows, (2 * g + 1) * LANES:(2 * g + 2) * LANES] = acc[BLOCK:].astype(BF)


def _prompt_kernel(sinks_ref, bias_ref, q_ref, kp_ref, kc_ref, km0_ref, km1_ref,
                   vp_ref, vc_ref, vm0_ref, vm1_ref, up_ref, uc_ref, w_ref, bdw_ref, g_ref,
                   b_ref, a_ref, c_ref, ext_ref, sh_ref, y_ref):
    s_id = pl.program_id(0)
    n_tok = BATCH * ATT_STEPS
    attn_refs = (q_ref, kp_ref, kc_ref, (km0_ref, km1_ref), vp_ref, vc_ref, (vm0_ref, vm1_ref),
                 a_ref)

    @pl.when(s_id > n_tok)
    def _():
        a_ref[...] = jnp.zeros_like(a_ref)
        c_ref[...] = jnp.zeros_like(c_ref)

    @pl.when(s_id == n_tok)
    def _():
        _attend(sinks_ref, bias_ref, (2, 2), *attn_refs)
        c_ref[...] = jnp.zeros_like(c_ref)

    @pl.when(s_id < n_tok)
    def _():
        first = (s_id % ATT_STEPS) == 0
        _attend(sinks_ref, bias_ref, (jnp.where(first, 1, 0), 0), *attn_refs)
        half = CONV_HALO // 2
        ext_ref[:half] = jnp.where(first, 0.0, up_ref[:half])
        ext_ref[half:CONV_HALO] = jnp.where(first, up_ref[:half], up_ref[half:])
        ext_ref[CONV_HALO:] = uc_ref[...]
        _conv_rows(ext_ref, sh_ref, y_ref, w_ref, bdw_ref, g_ref, b_ref, c_ref, ATT_Q)


def _prompt_call(sinks, bias, q, k, v, u, w_dw, b_dw, g, b):
    n_tok = BATCH * ATT_STEPS
    cur = lambda s: (jnp.minimum(s, n_tok), 0)
    fixed = lambda s: (0, 0)

    def prev(s):
        return (jnp.minimum(jnp.where(s % ATT_STEPS == 0, 2 * s, 2 * s - 1), 2 * n_tok), 0)

    def meta(h):
        def index(s):
            b = jnp.where(s < n_tok, s // ATT_STEPS, jnp.where(s == n_tok, h, 0))
            return ((MET0 + b * BLOCK) // N_META, 0)
        return index

    def u_prev(s):
        first = (MET0 + (s // ATT_STEPS) * BLOCK) // CONV_HALO
        tok = s * (ATT_Q // CONV_HALO) - 1
        return (jnp.where(s < n_tok, jnp.where(s % ATT_STEPS == 0, first, tok), 0), 0)

    return pl.pallas_call(
        _prompt_kernel,
        grid=(T // ATT_Q,),
        in_specs=[pl.BlockSpec(memory_space=pltpu.SMEM),
                  pl.BlockSpec((3, ATT_Q, N_KEYS), lambda s: (0, 0, 0)),
                  pl.BlockSpec((ATT_Q, ATTN_WIDTH), cur),
                  pl.BlockSpec((BLOCK, KV_WIDTH), prev),
                  pl.BlockSpec((ATT_Q, KV_WIDTH), cur),
                  pl.BlockSpec((N_META, KV_WIDTH), meta(0)),
                  pl.BlockSpec((N_META, KV_WIDTH), meta(1)),
                  pl.BlockSpec((BLOCK, KV_WIDTH), prev),
                  pl.BlockSpec((ATT_Q, KV_WIDTH), cur),
                  pl.BlockSpec((N_META, KV_WIDTH), meta(0)),
                  pl.BlockSpec((N_META, KV_WIDTH), meta(1)),
                  pl.BlockSpec((CONV_HALO, CONV_CH), u_prev),
                  pl.BlockSpec((ATT_Q, CONV_CH), cur),
                  pl.BlockSpec((CONV_WIDTH, CONV_CH), fixed),
                  _vec(CONV_CH), _vec(CONV_CH), _vec(CONV_CH)],
        out_specs=[pl.BlockSpec((ATT_Q, ATTN_WIDTH), lambda s: (s, 0)),
                   pl.BlockSpec((ATT_Q, CONV_CH), lambda s: (s, 0))],
        out_shape=[jax.ShapeDtypeStruct((T, ATTN_WIDTH), BF),
                   jax.ShapeDtypeStruct((T, CONV_CH), BF)],
        scratch_shapes=_conv_scratch(ATT_Q),
        compiler_params=_cparams(1),
        name="prompt_attn_conv",
    )(sinks, bias, q, k, k, k, k, v, v, v, v, u, u, w_dw, b_dw.reshape(1, CONV_CH),
      g.reshape(1, CONV_CH), b.reshape(1, CONV_CH))


SAMPLE_ROWS = N_KV_HEADS * DEC_SEQ * (N_Q_HEADS // N_KV_HEADS)


def _sattn_kernel(sk_ref, q_ref, kl_ref, km_ref, vl_ref, vm_ref, o_ref):
    q = q_ref[...]
    nt = (((2,), (2,)), ((0,), (0,)))
    s_loc = lax.dot_general(q, kl_ref[...].astype(BF), nt, preferred_element_type=F32)
    s_meta = lax.dot_general(q, km_ref[...].astype(BF), nt, preferred_element_type=F32)
    r = lax.broadcasted_iota(jnp.int32, (1, SAMPLE_ROWS, W_LOC), 1)
    idx = lax.broadcasted_iota(jnp.int32, (1, SAMPLE_ROWS, W_LOC), 2)
    sq = (r % (SAMPLE_ROWS // N_KV_HEADS)) // (N_Q_HEADS // N_KV_HEADS)
    ok = (idx >= sq) & (idx <= WINDOW + sq)
    s_loc = jnp.where(ok, s_loc, -jnp.inf)
    sk = sk_ref[...][None, :, :1]
    m = jnp.maximum(jnp.maximum(jnp.max(s_loc, -1, keepdims=True),
                                jnp.max(s_meta, -1, keepdims=True)), sk)
    p_loc = jnp.exp(s_loc - m)
    p_meta = jnp.exp(s_meta - m)
    den = (jnp.sum(p_loc, -1, keepdims=True) + jnp.sum(p_meta, -1, keepdims=True)
           + jnp.exp(sk - m))
    inv = 1.0 / den
    nn = (((2,), (1,)), ((0,), (0,)))
    o_ref[...] = (
        lax.dot_general((p_loc * inv).astype(BF), vl_ref[...].astype(BF), nn,
                        preferred_element_type=F32)
        + lax.dot_general((p_meta * inv).astype(BF), vm_ref[...].astype(BF), nn,
                          preferred_element_type=F32))


def _sattn_call(sink_rows, qz, k_loc, mk, v_loc, mv):
    b3 = lambda i: (i, 0, 0)
    return pl.pallas_call(
        _sattn_kernel,
        grid=(DEC_BATCH // SAMPLE_BT,),
        in_specs=[pl.BlockSpec((SAMPLE_ROWS, LANES), lambda i: (0, 0)),
                  pl.BlockSpec((SAMPLE_BT, SAMPLE_ROWS, LANES), b3),
                  pl.BlockSpec((SAMPLE_BT, W_LOC, KV_WIDTH), b3),
                  pl.BlockSpec((SAMPLE_BT, N_META, KV_WIDTH), b3),
                  pl.BlockSpec((SAMPLE_BT, W_LOC, KV_WIDTH), b3),
                  pl.BlockSpec((SAMPLE_BT, N_META, KV_WIDTH), b3)],
        out_specs=pl.BlockSpec((SAMPLE_BT, SAMPLE_ROWS, LANES), b3),
        out_shape=jax.ShapeDtypeStruct((DEC_BATCH, SAMPLE_ROWS, LANES), F32),
        compiler_params=_cparams(1),
        name="sample_attn",
    )(sink_rows, qz, k_loc, mk, v_loc, mv)


def _conv_tail(y, g, b):
    return _silu(_ln_rows(y, g, b))


def _conv_rows(ext_ref, sh_ref, y_ref, w_ref, bdw_ref, g_ref, b_ref, o_ref, n_rows):
    off = CONV_HALO - CONV_STATE
    span = n_rows + CONV_HALO - SUBLANES
    for s in range(1, SUBLANES):
        sh_ref[s - 1, :span, :] = ext_ref[s:s + span, :]
    for lg in range(CONV_CH // LANES):
        lanes = slice(lg * LANES, (lg + 1) * LANES)
        for r0 in range(0, n_rows, CONV_CHUNK):
            acc = jnp.broadcast_to(bdw_ref[:, lanes], (CONV_CHUNK, LANES))
            for j in range(CONV_WIDTH):
                a, s = divmod(off + j, SUBLANES)
                src = ext_ref if s == 0 else sh_ref.at[s - 1]
                x = src[r0 + a * SUBLANES:r0 + a * SUBLANES + CONV_CHUNK, lanes]
                acc = acc + x * w_ref[j:j + 1, lanes]
            y_ref[r0:r0 + CONV_CHUNK, lanes] = acc
    for r0 in range(0, n_rows, CONV_CHUNK):
        y = y_ref[r0:r0 + CONV_CHUNK, :]
        o_ref[r0:r0 + CONV_CHUNK, :] = _conv_tail(y, g_ref[...], b_ref[...]).astype(BF)


def _conv_scratch(n_rows):
    return [pltpu.VMEM((CONV_HALO + n_rows, CONV_CH), F32),
            pltpu.VMEM((SUBLANES - 1, CONV_HALO + n_rows - SUBLANES, CONV_CH), F32),
            pltpu.VMEM((n_rows, CONV_CH), F32)]


def _mconv_kernel(cur_ref, w_ref, bdw_ref, g_ref, b_ref, o_ref, ext_ref, sh_ref, y_ref):
    ext_ref[:CONV_HALO] = jnp.zeros((CONV_HALO, CONV_CH), F32)
    ext_ref[CONV_HALO:] = cur_ref[...]
    _conv_rows(ext_ref, sh_ref, y_ref, w_ref, bdw_ref, g_ref, b_ref, o_ref, BLOCK)


def _mconv_call(u, w_dw, b_dw, g, b):
    fixed = lambda bb: (0, 0)
    return pl.pallas_call(
        _mconv_kernel,
        grid=(BATCH,),
        in_specs=[pl.BlockSpec((BLOCK, CONV_CH), lambda bb: (MET0 // BLOCK + bb, 0)),
                  pl.BlockSpec((CONV_WIDTH, CONV_CH), fixed),
                  _vec(CONV_CH), _vec(CONV_CH), _vec(CONV_CH)],
        out_specs=pl.BlockSpec((BLOCK, CONV_CH), lambda bb: (bb, 0)),
        out_shape=jax.ShapeDtypeStruct((BATCH * BLOCK, CONV_CH), BF),
        scratch_shapes=_conv_scratch(BLOCK),
        compiler_params=_cparams(1),
        name="meta_conv",
    )(u, w_dw, b_dw.reshape(1, CONV_CH), g.reshape(1, CONV_CH), b.reshape(1, CONV_CH))


def _sconv_kernel(st_ref, u_ref, w_ref, bdw_ref, g_ref, b_ref, o_ref, new_ref, y_ref):
    l = pl.program_id(0)

    def ext_row(r):
        if r < CONV_STATE:
            return st_ref[0, r]
        return u_ref[pl.ds(r - CONV_STATE, DEC_BATCH, stride=DEC_SEQ), :]

    rows = [ext_row(r) for r in range(CONV_STATE + DEC_SEQ)]
    for r in range(CONV_STATE):
        new_ref[r] = rows[r + DEC_SEQ]
    for s in range(DEC_SEQ):
        acc = jnp.broadcast_to(bdw_ref[...], (DEC_BATCH, LANES))
        for j in range(CONV_WIDTH):
            acc = acc + rows[s + j] * w_ref[j:j + 1, :]
        y_ref[l, s] = acc

    @pl.when(l == CONV_CH // LANES - 1)
    def _():
        for s in range(DEC_SEQ):
            y = jnp.concatenate([y_ref[c, s] for c in range(CONV_CH // LANES)], axis=-1)
            o_ref[s] = _conv_tail(y, g_ref[...], b_ref[...]).astype(BF)


def _sconv_call(state_t, layer, u, w_dw, b_dw, g, b):
    grp = lambda rows: pl.BlockSpec((rows, LANES), lambda l: (0, l))
    full = lambda shape: pl.BlockSpec(shape, lambda l: (0,) * len(shape))
    return pl.pallas_call(
        _sconv_kernel,
        grid=(CONV_CH // LANES,),
        in_specs=[pl.BlockSpec((1, CONV_STATE, DEC_BATCH, LANES), lambda l: (layer, 0, 0, l)),
                  grp(NS), grp(CONV_WIDTH), grp(1),
                  full((1, CONV_CH)), full((1, CONV_CH))],
        out_specs=[full((DEC_SEQ, DEC_BATCH, CONV_CH)),
                   pl.BlockSpec((CONV_STATE, DEC_BATCH, LANES), lambda l: (0, 0, l))],
        out_shape=[jax.ShapeDtypeStruct((DEC_SEQ, DEC_BATCH, CONV_CH), BF),
                   jax.ShapeDtypeStruct((CONV_STATE, DEC_BATCH, CONV_CH), F32)],
        scratch_shapes=[pltpu.VMEM((CONV_CH // LANES, DEC_SEQ, DEC_BATCH, LANES), F32)],
        compiler_params=_cparams(1),
        name="sample_conv",
    )(state_t, u, w_dw, b_dw.reshape(1, CONV_CH), g.reshape(1, CONV_CH), b.reshape(1, CONV_CH))


def _mix(a_ref, c_ref, h_ref, wo_ref, g1_ref, b1_ref):
    mix = (jnp.dot(a_ref[...], wo_ref[0, :ATTN_WIDTH, :], preferred_element_type=F32)
           + jnp.dot(c_ref[...], wo_ref[0, ATTN_WIDTH:, :], preferred_element_type=F32))
    return _ln_rows(ALPHA * h_ref[...] + mix, g1_ref[...], b1_ref[...])


def _dense_kernel(a_ref, c_ref, h_ref, wo_ref, g1_ref, b1_ref, wg_ref, wu_ref, wd_ref,
                  g2_ref, b2_ref, o_ref):
    h1 = _mix(a_ref, c_ref, h_ref, wo_ref, g1_ref, b1_ref)
    hb = h1.astype(BF)
    gate = jnp.dot(hb, wg_ref[0], preferred_element_type=F32)
    up = jnp.dot(hb, wu_ref[0], preferred_element_type=F32)
    f = jnp.dot((_silu(gate) * up).astype(BF), wd_ref[0], preferred_element_type=F32)
    o_ref[...] = _ln_rows(ALPHA * h1 + f, g2_ref[...], b2_ref[...])


def _dense_call(a, c, h, wo, layer, g1, b1, wg, wu, wd, j, g2, b2):
    row = lambda i: (i, 0)
    once = dict(pipeline_mode=pl.Buffered(1))
    return pl.pallas_call(
        _dense_kernel,
        grid=(T // TM,),
        in_specs=[pl.BlockSpec((TM, ATTN_WIDTH), row),
                  pl.BlockSpec((TM, CONV_CH), row),
                  pl.BlockSpec((TM, D_MODEL), row),
                  pl.BlockSpec((1, D_MODEL, D_MODEL), lambda i: (layer, 0, 0), **once),
                  _vec(D_MODEL), _vec(D_MODEL),
                  pl.BlockSpec((1, D_MODEL, D_FF_DENSE), lambda i: (j, 0, 0), **once),
                  pl.BlockSpec((1, D_MODEL, D_FF_DENSE), lambda i: (j, 0, 0), **once),
                  pl.BlockSpec((1, D_FF_DENSE, D_MODEL), lambda i: (j, 0, 0), **once),
                  _vec(D_MODEL), _vec(D_MODEL)],
        out_specs=pl.BlockSpec((TM, D_MODEL), row),
        out_shape=jax.ShapeDtypeStruct((T, D_MODEL), F32),
        compiler_params=_cparams(1),
        name="mix_dense_ffn",
    )(a, c, h, wo, g1.reshape(1, -1), b1.reshape(1, -1), wg, wu, wd,
      g2.reshape(1, -1), b2.reshape(1, -1))


def _router_kernel(a_ref, c_ref, h_ref, wo_ref, g1_ref, b1_ref, rwh_ref, rwl_ref, rb_ref,
                   h1_ref, comb_ref, sel_ref):
    h1 = _mix(a_ref, c_ref, h_ref, wo_ref, g1_ref, b1_ref)
    h1_ref[...] = h1
    xh = h1.astype(BF)
    xl = (h1 - xh.astype(F32)).astype(BF)
    logits = (jnp.dot(xh, rwh_ref[...], preferred_element_type=F32)
              + jnp.dot(xl, rwh_ref[...], preferred_element_type=F32)
              + jnp.dot(xh, rwl_ref[...], preferred_element_type=F32)) + rb_ref[...]
    lane = lax.broadcasted_iota(jnp.int32, (TM, LANES), 1)
    logits = jnp.where(lane < N_EXPERTS, logits, -jnp.inf)
    v1 = jnp.max(logits, -1, keepdims=True)
    i1 = jnp.min(jnp.where(logits == v1, lane, LANES), -1, keepdims=True)
    rest = jnp.where(lane == i1, -jnp.inf, logits)
    v2 = jnp.max(rest, -1, keepdims=True)
    i2 = jnp.min(jnp.where(rest == v2, lane, LANES), -1, keepdims=True)
    e2 = jnp.exp(v2 - v1)
    den = 1.0 + e2
    comb_ref[...] = jnp.where(lane == i1, 1.0 / den, jnp.where(lane == i2, e2 / den, 0.0))
    sel_ref[...] = jnp.where((lane == i1) | (lane == i2), 1.0, 0.0).astype(BF)


def _router_call(a, c, h, wo, layer, g1, b1, rwh, rwl, rb):
    row = lambda i: (i, 0)
    fixed = lambda i: (0, 0)
    return pl.pallas_call(
        _router_kernel,
        grid=(T // TM,),
        in_specs=[pl.BlockSpec((TM, ATTN_WIDTH), row),
                  pl.BlockSpec((TM, CONV_CH), row),
                  pl.BlockSpec((TM, D_MODEL), row),
                  pl.BlockSpec((1, D_MODEL, D_MODEL), lambda i: (layer, 0, 0)),
                  _vec(D_MODEL), _vec(D_MODEL),
                  pl.BlockSpec((D_MODEL, LANES), fixed),
                  pl.BlockSpec((D_MODEL, LANES), fixed),
                  _vec(LANES)],
        out_specs=[pl.BlockSpec((TM, D_MODEL), row),
                   pl.BlockSpec((TM, LANES), row),
                   pl.BlockSpec((TM, LANES), row)],
        out_shape=[jax.ShapeDtypeStruct((T, D_MODEL), F32),
                   jax.ShapeDtypeStruct((T, LANES), F32),
                   jax.ShapeDtypeStruct((T, LANES), BF)],
        compiler_params=_cparams(1),
        name="mix_router",
    )(a, c, h, wo, g1.reshape(1, -1), b1.reshape(1, -1), rwh, rwl, rb)


def _moe_kernel(x_ref, comb_ref, sel_ref, tri_ref, eg_ref, eu_ref, ed_ref, g2_ref, b2_ref,
                o_ref, xb_ref, rankr_ref, selr_ref, combr_ref, put_ref, acc_ref):
    e = pl.program_id(1)

    @pl.when(e == 0)
    def _():
        xb_ref[...] = x_ref[...].astype(BF)
        sel = sel_ref[...]
        rank = jnp.dot(tri_ref[...], sel, preferred_element_type=F32)
        rankr_ref[...] = rank.T
        selr_ref[...] = sel.astype(F32).T
        combr_ref[...] = comb_ref[...].T
        acc_ref[...] = jnp.zeros_like(acc_ref)

    sel_row = selr_ref[pl.ds(e, 1), :]
    rank_row = rankr_ref[pl.ds(e, 1), :]
    comb_row = combr_ref[pl.ds(e, 1), :]
    count = jnp.sum(sel_row).astype(jnp.int32)

    def expert_pass(first, size):
        base = first.astype(F32)
        slot = lax.broadcasted_iota(jnp.int32, (size, TM_MOE), 0).astype(F32)
        take = jnp.where((rank_row - base == slot) & (sel_row > 0.0), 1.0, 0.0)
        weight = jnp.sum(take * comb_row, -1, keepdims=True)
        xc = jnp.dot(take.astype(BF), xb_ref[...], preferred_element_type=F32).astype(BF)
        gate = jnp.dot(xc, eg_ref[0, 0], preferred_element_type=F32)
        up = jnp.dot(xc, eu_ref[0, 0], preferred_element_type=F32)
        out = jnp.dot((_silu(gate) * up).astype(BF), ed_ref[0, 0], preferred_element_type=F32)
        put_ref[:, :size] = take.T.astype(BF)
        acc_ref[...] += jnp.dot(put_ref[:, :size], (out * weight).astype(BF),
                                preferred_element_type=F32)

    n_big = (count + MOE_CHUNK - 1) // (2 * MOE_CHUNK)

    def big_pass(ci, carry):
        expert_pass(ci * (2 * MOE_CHUNK), 2 * MOE_CHUNK)
        return carry

    lax.fori_loop(0, n_big, big_pass, 0)

    @pl.when(count > n_big * (2 * MOE_CHUNK))
    def _():
        expert_pass(n_big * (2 * MOE_CHUNK), MOE_CHUNK)

    @pl.when(e == N_EXPERTS - 1)
    def _():
        o_ref[...] = _ln_rows(ALPHA * x_ref[...] + acc_ref[...], g2_ref[...], b2_ref[...])


def _moe_call(x, comb, sel, tri, eg, eu, ed, j, g2, b2, tile0, n_tiles):
    row = lambda i, e: (tile0 + i, 0)
    fixed = lambda i, e: (0, 0)
    exp = lambda i, e: (j, e, 0, 0)
    return pl.pallas_call(
        _moe_kernel,
        grid=(n_tiles, N_EXPERTS),
        in_specs=[pl.BlockSpec((TM_MOE, D_MODEL), row),
                  pl.BlockSpec((TM_MOE, LANES), row),
                  pl.BlockSpec((TM_MOE, LANES), row),
                  pl.BlockSpec((TM_MOE, TM_MOE), fixed),
                  pl.BlockSpec((1, 1, D_MODEL, D_FF_EXPERT), exp),
                  pl.BlockSpec((1, 1, D_MODEL, D_FF_EXPERT), exp),
                  pl.BlockSpec((1, 1, D_FF_EXPERT, D_MODEL), exp),
                  _vec(D_MODEL), _vec(D_MODEL)],
        out_specs=pl.BlockSpec((TM_MOE, D_MODEL), lambda i, e: (i, 0)),
        out_shape=jax.ShapeDtypeStruct((n_tiles * TM_MOE, D_MODEL), F32),
        scratch_shapes=[pltpu.VMEM((TM_MOE, D_MODEL), BF),
                        pltpu.VMEM((LANES, TM_MOE), F32),
                        pltpu.VMEM((LANES, TM_MOE), F32),
                        pltpu.VMEM((LANES, TM_MOE), F32),
                        pltpu.VMEM((TM_MOE, 2 * MOE_CHUNK), BF),
                        pltpu.VMEM((TM_MOE, D_MODEL), F32)],
        compiler_params=_cparams(2),
        name="moe_ffn",
    )(x, comb, sel, tri, eg, eu, ed, g2.reshape(1, -1), b2.reshape(1, -1))


def kernel(x_prompt, x_sample, cache_meta_k, cache_meta_v, cache_k, cache_v, state_conv,
           meta_tokens, ln_in_g, ln_in_b, w_in, w_dw, b_dw, conv_ln_g, conv_ln_b, sinks, w_out,
           ln1_g, ln1_b, ln2_g, ln2_b, ffd_w_gate, ffd_w_up, ffd_w_down,
           router_w, router_b, exp_w_gate, exp_w_up, exp_w_down):
    meta_blk = jnp.concatenate([meta_tokens, jnp.zeros((BLOCK - N_META, D_MODEL), F32)], 0)
    x_tail = jnp.concatenate([meta_blk] * BATCH + [x_sample.reshape(NS, D_MODEL),
                                                   jnp.zeros((T - PAD0, D_MODEL), F32)], 0)
    cos_np, sin_np = _rope_tables()
    cos_t, sin_t = jnp.asarray(cos_np), jnp.asarray(sin_np)
    tri = jnp.asarray(np.tril(np.ones((TM_MOE, TM_MOE), np.float32), -1), dtype=BF)
    bias = jnp.asarray(_attn_bias())
    w_in_b, w_out_b = w_in.astype(BF), w_out.astype(BF)
    ffd_b = (ffd_w_gate.astype(BF), ffd_w_up.astype(BF), ffd_w_down.astype(BF))
    exp_b = (exp_w_gate.astype(BF), exp_w_up.astype(BF), exp_w_down.astype(BF))

    state_t = state_conv.transpose(0, 2, 1, 3)

    per_kv = SAMPLE_ROWS // N_KV_HEADS
    head_of_row = (jnp.arange(SAMPLE_ROWS) // per_kv) * 4 + jnp.arange(SAMPLE_ROWS) % 4
    met_rows = lambda x, b: x[MET0 + b * BLOCK:MET0 + b * BLOCK + N_META]
    kv5 = lambda rows: jnp.stack(rows).reshape(BATCH, -1, N_KV_HEADS, HEAD_DIM)

    pmk, pmv, pk, pv, pc, sk, sv, sc = [], [], [], [], [], [], [], []
    y_tok = y_tail = None
    for i in range(DEPTH):
        if i == 0:
            h, q, k, v, u = _inproj0_call(x_prompt.reshape(TOK, D_MODEL), x_tail, ln_in_g,
                                          ln_in_b, w_in_b, cos_t, sin_t)
        else:
            q, k, v, u = _inproj_call(h, w_in_b, i, cos_t, sin_t)

        a, c = _prompt_call(sinks[i], bias, q, k, v, u, w_dw[i], b_dw[i], conv_ln_g[i],
                            conv_ln_b[i])
        c_m = _mconv_call(u, w_dw[i], b_dw[i], conv_ln_g[i], conv_ln_b[i])
        pmk.append(kv5([met_rows(k, b) for b in range(BATCH)]))
        pmv.append(kv5([met_rows(v, b) for b in range(BATCH)]))
        pk.append(kv5([k[(b + 1) * SEQ - WINDOW:(b + 1) * SEQ] for b in range(BATCH)]))
        pv.append(kv5([v[(b + 1) * SEQ - WINDOW:(b + 1) * SEQ] for b in range(BATCH)]))
        pc.append(jnp.stack([u[(b + 1) * SEQ - CONV_STATE:(b + 1) * SEQ] for b in range(BATCH)]))

        k2 = k[SMP0:PAD0].reshape(DEC_BATCH, DEC_SEQ, KV_WIDTH)
        v2 = v[SMP0:PAD0].reshape(DEC_BATCH, DEC_SEQ, KV_WIDTH)
        k_loc = jnp.concatenate([cache_k[i].reshape(DEC_BATCH, WINDOW, KV_WIDTH), k2], 1)
        v_loc = jnp.concatenate([cache_v[i].reshape(DEC_BATCH, WINDOW, KV_WIDTH), v2], 1)
        q2 = q[SMP0:PAD0].reshape(DEC_BATCH, DEC_SEQ, N_KV_HEADS, 4, HEAD_DIM)
        q2 = q2.transpose(0, 2, 1, 3, 4).reshape(DEC_BATCH, N_KV_HEADS, per_kv, HEAD_DIM)
        zq = jnp.zeros_like(q2[:, 0])
        qz = jnp.concatenate([jnp.concatenate([q2[:, 0], zq], -1),
                              jnp.concatenate([zq, q2[:, 1]], -1)], 1)
        sink_rows = jnp.broadcast_to(sinks[i][head_of_row][:, None], (SAMPLE_ROWS, LANES))
        o2 = _sattn_call(sink_rows, qz, k_loc,
                         cache_meta_k[i].reshape(DEC_BATCH, N_META, KV_WIDTH), v_loc,
                         cache_meta_v[i].reshape(DEC_BATCH, N_META, KV_WIDTH))
        o2 = jnp.stack([o2[:, :per_kv, :HEAD_DIM], o2[:, per_kv:, HEAD_DIM:]], 1)
        a_s = o2.reshape(DEC_BATCH, N_KV_HEADS, DEC_SEQ, 4, HEAD_DIM).transpose(0, 2, 1, 3, 4)
        a_s = a_s.reshape(NS, ATTN_WIDTH).astype(BF)
        u2 = u[SMP0:PAD0]
        c_s, sc_i = _sconv_call(state_t, i, u2, w_dw[i], b_dw[i], conv_ln_g[i], conv_ln_b[i])
        c_s = c_s.transpose(1, 0, 2).reshape(NS, CONV_CH)
        sk.append(k_loc[:, DEC_SEQ:].reshape(DEC_BATCH, WINDOW, N_KV_HEADS, HEAD_DIM))
        sv.append(v_loc[:, DEC_SEQ:].reshape(DEC_BATCH, WINDOW, N_KV_HEADS, HEAD_DIM))
        sc.append(sc_i)

        a = lax.dynamic_update_slice(a, a_s, (SMP0, 0))
        c = lax.dynamic_update_slice(c, jnp.concatenate([c_m, c_s], 0), (MET0, 0))
        j = i // 2
        if i % 2 == 0:
            h = _dense_call(a, c, h, w_out_b, i, ln1_g[i], ln1_b[i], *ffd_b, j,
                            ln2_g[i], ln2_b[i])
        else:
            rw = jnp.pad(router_w[j], ((0, 0), (0, LANES - N_EXPERTS)))
            rwh = rw.astype(BF)
            rwl = (rw - rwh.astype(F32)).astype(BF)
            rb = jnp.pad(router_b[j], (0, LANES - N_EXPERTS)).reshape(1, LANES)
            h1, comb, sel = _router_call(a, c, h, w_out_b, i, ln1_g[i], ln1_b[i], rwh, rwl, rb)
            moe = functools.partial(_moe_call, h1, comb, sel, tri, *exp_b, j,
                                    ln2_g[i], ln2_b[i])
            if i < DEPTH - 1:
                h = moe(0, T // TM_MOE)
            else:
                y_tok = moe(0, TOK // TM_MOE)
                y_tail = moe(TOK // TM_MOE, TAIL // TM_MOE)

    y_prompt = y_tok.reshape(BATCH, SEQ, D_MODEL)
    y_sample = y_tail[SMP0 - TOK:PAD0 - TOK].reshape(DEC_BATCH, DEC_SEQ, D_MODEL)
    return (y_prompt, y_sample, jnp.stack(pmk), jnp.stack(pmv), jnp.stack(pk), jnp.stack(pv),
            jnp.stack(pc), jnp.stack(sk), jnp.stack(sv), jnp.stack(sc).transpose(0, 2, 1, 3))
```

```python
import functools

import numpy as np
import jax
import jax.numpy as jnp
from jax import lax
from jax.experimental import pallas as pl
from jax.experimental.pallas import tpu as pltpu

D_MODEL = 1024
BATCH = 2
SEQ = 8192
DEPTH = 4
DEC_BATCH = 128
DEC_SEQ = 4
PAST_LEN = 8192
N_META = 16
HEAD_DIM = 64
N_Q_HEADS = 8
N_KV_HEADS = 2
ATTN_WIDTH = N_Q_HEADS * HEAD_DIM
KV_WIDTH = N_KV_HEADS * HEAD_DIM
CONV_CH = D_MODEL - ATTN_WIDTH
Q_END = ATTN_WIDTH
K_END = Q_END + KV_WIDTH
V_END = K_END + KV_WIDTH
IN_COLS = V_END + 2 * CONV_CH
CONV_WIDTH = 31
CONV_STATE = CONV_WIDTH - 1
WINDOW = 128
BLOCK = 128
ROPE_THETA = 10000.0
ATTN_SCALE = HEAD_DIM ** -0.5
D_FF_DENSE = 2816
N_EXPERTS = 8
D_FF_EXPERT = 1024
ALPHA = (2 * DEPTH) ** 0.25
LN_EPS = 1e-5

LANES = 128
SUBLANES = 8
TOK = BATCH * SEQ
MET0 = TOK
SMP0 = MET0 + BATCH * BLOCK
NS = DEC_BATCH * DEC_SEQ
PAD0 = SMP0 + NS
TM = 512
TM_MOE = 1024
T = -(-PAD0 // TM_MOE) * TM_MOE
TAIL = T - TOK
ATT_NB = 2
ATT_Q = ATT_NB * BLOCK
ATT_STEPS = SEQ // ATT_Q
N_KEYS = 2 * BLOCK + N_META
CONV_HALO = 32
CONV_CHUNK = 64
SAMPLE_BT = 16
MOE_CHUNK = 128
VMEM_LIMIT = 56 * 1024 * 1024

BF = jnp.bfloat16
F32 = jnp.float32


def _cparams(n_axes):
    return pltpu.CompilerParams(dimension_semantics=("arbitrary",) * n_axes,
                                vmem_limit_bytes=VMEM_LIMIT)


def _ln_rows(x, g, b):
    mu = jnp.mean(x, -1, keepdims=True)
    xc = x - mu
    var = jnp.mean(xc * xc, -1, keepdims=True)
    return xc * lax.rsqrt(var + LN_EPS) * g + b


def _silu(x):
    return x * jax.nn.sigmoid(x)


def _vec(n):
    return pl.BlockSpec((1, n), lambda *_: (0, 0))


def _ln_kernel(xt_ref, xr_ref, g_ref, b_ref, o_ref):
    i = pl.program_id(0)

    @pl.when(i < TOK // TM)
    def _():
        o_ref[...] = _ln_rows(xt_ref[...], g_ref[...], b_ref[...])

    @pl.when(i >= TOK // TM)
    def _():
        o_ref[...] = _ln_rows(xr_ref[...], g_ref[...], b_ref[...])


def _project(h_ref, w_ref, cos_ref, sin_ref, q_ref, k_ref, v_ref, u_ref):
    hb = h_ref[...].astype(BF)
    cos = cos_ref[...]
    sin = sin_ref[...]
    lane = lax.broadcasted_iota(jnp.int32, (TM, LANES), 1)
    first_half = (lane % HEAD_DIM) < (HEAD_DIM // 2)

    def rope(x):
        rot = jnp.where(first_half, pltpu.roll(x, LANES - HEAD_DIM // 2, 1),
                        pltpu.roll(x, HEAD_DIM // 2, 1))
        return x * cos + rot * sin

    qk = jnp.dot(hb, w_ref[0, :, :K_END], preferred_element_type=F32)
    for c in range(Q_END // LANES):
        q_ref[:, c * LANES:(c + 1) * LANES] = (
            rope(qk[:, c * LANES:(c + 1) * LANES]) * ATTN_SCALE).astype(BF)
    k_ref[...] = rope(qk[:, Q_END:K_END])
    v_ref[...] = jnp.dot(hb, w_ref[0, :, K_END:V_END], preferred_element_type=F32)
    val = jnp.dot(hb, w_ref[0, :, V_END:V_END + CONV_CH], preferred_element_type=F32)
    gate = jnp.dot(hb, w_ref[0, :, V_END + CONV_CH:], preferred_element_type=F32)
    u_ref[...] = val * jax.nn.sigmoid(gate)


def _inproj_kernel(h_ref, w_ref, cos_ref, sin_ref, q_ref, k_ref, v_ref, u_ref):
    _project(h_ref, w_ref, cos_ref, sin_ref, q_ref, k_ref, v_ref, u_ref)


def _inproj0_kernel(xt_ref, xr_ref, g_ref, b_ref, w_ref, cos_ref, sin_ref,
                    h_ref, q_ref, k_ref, v_ref, u_ref):
    _ln_kernel(xt_ref, xr_ref, g_ref, b_ref, h_ref)
    _project(h_ref, w_ref, cos_ref, sin_ref, q_ref, k_ref, v_ref, u_ref)


def _rope_tables():
    half = HEAD_DIM // 2
    pos = np.zeros((SEQ + TAIL,), np.float64)
    pos[:SEQ] = N_META + np.arange(SEQ)
    for b in range(BATCH):
        pos[SEQ + b * BLOCK:SEQ + b * BLOCK + N_META] = np.arange(N_META)
    s0 = SEQ + SMP0 - MET0
    pos[s0:s0 + NS] = PAST_LEN + np.tile(np.arange(DEC_SEQ), DEC_BATCH)
    inv = ROPE_THETA ** (-np.arange(half, dtype=np.float64) / half)
    ang = pos[:, None] * inv[None, :]
    cos = np.cos(ang).astype(np.float32)
    sin = np.sin(ang).astype(np.float32)
    return np.tile(cos, (1, 4)), np.tile(np.concatenate([-sin, sin], -1), (1, 2))


def _inproj_specs(layer):
    row = lambda i: (i, 0)
    per_batch = SEQ // TM
    tab = lambda i: (jnp.where(i < 2 * per_batch, i % per_batch, i - per_batch), 0)
    in_specs = [pl.BlockSpec((1, D_MODEL, IN_COLS), lambda i: (layer, 0, 0)),
                pl.BlockSpec((TM, LANES), tab),
                pl.BlockSpec((TM, LANES), tab)]
    out_specs = [pl.BlockSpec((TM, ATTN_WIDTH), row),
                 pl.BlockSpec((TM, KV_WIDTH), row),
                 pl.BlockSpec((TM, KV_WIDTH), row),
                 pl.BlockSpec((TM, CONV_CH), row)]
    out_shape = [jax.ShapeDtypeStruct((T, ATTN_WIDTH), BF),
                 jax.ShapeDtypeStruct((T, KV_WIDTH), F32),
                 jax.ShapeDtypeStruct((T, KV_WIDTH), F32),
                 jax.ShapeDtypeStruct((T, CONV_CH), F32)]
    return in_specs, out_specs, out_shape


def _inproj_call(h, w_bf, layer, cos_t, sin_t):
    in_specs, out_specs, out_shape = _inproj_specs(layer)
    return pl.pallas_call(
        _inproj_kernel,
        grid=(T // TM,),
        in_specs=[pl.BlockSpec((TM, D_MODEL), lambda i: (i, 0))] + in_specs,
        out_specs=out_specs,
        out_shape=out_shape,
        compiler_params=_cparams(1),
        name="inproj",
    )(h, w_bf, cos_t, sin_t)


def _inproj0_call(x_tok, x_tail, g, b, w_bf, cos_t, sin_t):
    in_specs, out_specs, out_shape = _inproj_specs(0)
    n_tok = TOK // TM
    row = pl.BlockSpec((TM, D_MODEL), lambda i: (i, 0))
    return pl.pallas_call(
        _inproj0_kernel,
        grid=(T // TM,),
        in_specs=[pl.BlockSpec((TM, D_MODEL), lambda i: (jnp.minimum(i, n_tok - 1), 0)),
                  pl.BlockSpec((TM, D_MODEL), lambda i: (jnp.maximum(i - n_tok, 0), 0)),
                  _vec(D_MODEL), _vec(D_MODEL)] + in_specs,
        out_specs=[row] + out_specs,
        out_shape=[jax.ShapeDtypeStruct((T, D_MODEL), F32)] + out_shape,
        compiler_params=_cparams(1),
        name="ln_inproj",
    )(x_tok, x_tail, g.reshape(1, D_MODEL), b.reshape(1, D_MODEL), w_bf, cos_t, sin_t)


def _head_variants(x):
    lane = lax.broadcasted_iota(jnp.int32, x.shape, x.ndim - 1)
    lo = lane < HEAD_DIM
    xr = pltpu.roll(x, HEAD_DIM, x.ndim - 1)
    zero = jnp.zeros_like(x)
    return [[jnp.where(lo, x, zero).astype(BF), jnp.where(lo, zero, xr).astype(BF)],
            [jnp.where(lo, xr, zero).astype(BF), jnp.where(lo, zero, x).astype(BF)]]


def _attn_bias():
    qi = np.arange(2 * BLOCK)[:, None] % BLOCK
    col = np.arange(N_KEYS)[None, :]
    d = BLOCK + qi - col
    band = (col < 2 * BLOCK) & (d >= 0) & (d <= WINDOW)
    is_meta = col >= 2 * BLOCK
    normal = band | is_meta
    first = (band & (col >= BLOCK)) | is_meta
    meta_q = is_meta & ((col - 2 * BLOCK) <= qi)
    return np.where(np.stack([normal, first, meta_q]), 0.0, -np.inf).astype(np.float32)


def _attend(sinks_ref, bias_ref, variant, q_ref, kp_ref, kc_ref, km_refs, vp_ref, vc_ref,
            vm_refs, o_ref):
    upper = lax.broadcasted_iota(jnp.int32, (2 * BLOCK, 1), 0) < BLOCK

    def keys_of(prev_ref, cur_ref, meta_refs, h):
        prev = prev_ref[...] if h == 0 else cur_ref[(h - 1) * BLOCK:h * BLOCK, :]
        cur = cur_ref[h * BLOCK:(h + 1) * BLOCK, :]
        return jnp.concatenate([prev, cur, meta_refs[min(h, 1)][...]], axis=0)

    for h in range(ATT_NB):
        rows = slice(h * BLOCK, (h + 1) * BLOCK)
        kvar = _head_variants(keys_of(kp_ref, kc_ref, km_refs, h))
        vvar = _head_variants(keys_of(vp_ref, vc_ref, vm_refs, h))
        bias = bias_ref[variant[h]]
        for g in range(N_KV_HEADS):
            qg = jnp.concatenate([q_ref[rows, (2 * g) * LANES:(2 * g + 1) * LANES],
                                  q_ref[rows, (2 * g + 1) * LANES:(2 * g + 2) * LANES]],
                                 axis=0)
            acc = jnp.zeros((2 * BLOCK, LANES), F32)
            for e in range(2):
                s = lax.dot_general(qg, kvar[g][e], (((1,), (1,)), ((), ())),
                                    preferred_element_type=F32) + bias
                sk = jnp.where(upper, sinks_ref[4 * g + e], sinks_ref[4 * g + 2 + e])
                m = jnp.maximum(jnp.max(s, -1, keepdims=True), sk)
                p = jnp.exp(s - m)
                den = jnp.sum(p, -1, keepdims=True) + jnp.exp(sk - m)
                pn = (p * (1.0 / den)).astype(BF)
                acc = acc + jnp.dot(pn, vvar[g][e], preferred_element_type=F32)
            o_ref[rows, (2 * g) * LANES:(2 * g + 1) * LANES] = acc[:BLOCK].astype(BF)
            o_ref[rows, (2 * g + 1) * LANES:(2 * g + 2) * LANES] = acc[BLOCK:].astype(BF)


def _prompt_kernel(sinks_ref, bias_ref, q_ref, kp_ref, kc_ref, km0_ref, km1_ref,
                   vp_ref, vc_ref, vm0_ref, vm1_ref, up_ref, uc_ref, w_ref, bdw_ref, g_ref,
                   b_ref, a_ref, c_ref, ext_ref, sh_ref, y_ref):
    s_id = pl.program_id(0)
    n_tok = BATCH * ATT_STEPS
    attn_refs = (q_ref, kp_ref, kc_ref, (km0_ref, km1_ref), vp_ref, vc_ref, (vm0_ref, vm1_ref),
                 a_ref)

    @pl.when(s_id > n_tok)
    def _():
        a_ref[...] = jnp.zeros_like(a_ref)
        c_ref[...] = jnp.zeros_like(c_ref)

    @pl.when(s_id == n_tok)
    def _():
        _attend(sinks_ref, bias_ref, (2,) * ATT_NB, *attn_refs)
        c_ref[...] = jnp.zeros_like(c_ref)

    @pl.when(s_id < n_tok)
    def _():
        first = (s_id % ATT_STEPS) == 0
        _attend(sinks_ref, bias_ref, (jnp.where(first, 1, 0),) + (0,) * (ATT_NB - 1),
                *attn_refs)
        half = CONV_HALO // 2
        ext_ref[:half] = jnp.where(first, 0.0, up_ref[:half])
        ext_ref[half:CONV_HALO] = jnp.where(first, up_ref[:half], up_ref[half:])
        ext_ref[CONV_HALO:] = uc_ref[...]
        _conv_rows(ext_ref, sh_ref, y_ref, w_ref, bdw_ref, g_ref, b_ref, c_ref, ATT_Q)


def _prompt_call(sinks, bias, q, k, v, u, w_dw, b_dw, g, b):
    n_tok = BATCH * ATT_STEPS
    cur = lambda s: (jnp.minimum(s, n_tok), 0)
    fixed = lambda s: (0, 0)

    def prev(s):
        blk = jnp.where(s % ATT_STEPS == 0, ATT_NB * s, ATT_NB * s - 1)
        return (jnp.minimum(blk, ATT_NB * n_tok), 0)

    def meta(h):
        def index(s):
            b = jnp.where(s < n_tok, s // ATT_STEPS, jnp.where(s == n_tok, h, 0))
            return ((MET0 + b * BLOCK) // N_META, 0)
        return index

    def u_prev(s):
        first = (MET0 + (s // ATT_STEPS) * BLOCK) // CONV_HALO
        tok = s * (ATT_Q // CONV_HALO) - 1
        return (jnp.where(s < n_tok, jnp.where(s % ATT_STEPS == 0, first, tok), 0), 0)

    return pl.pallas_call(
        _prompt_kernel,
        grid=(T // ATT_Q,),
        in_specs=[pl.BlockSpec(memory_space=pltpu.SMEM),
                  pl.BlockSpec((3, 2 * BLOCK, N_KEYS), lambda s: (0, 0, 0)),
                  pl.BlockSpec((ATT_Q, ATTN_WIDTH), cur),
                  pl.BlockSpec((BLOCK, KV_WIDTH), prev),
                  pl.BlockSpec((ATT_Q, KV_WIDTH), cur),
                  pl.BlockSpec((N_META, KV_WIDTH), meta(0)),
                  pl.BlockSpec((N_META, KV_WIDTH), meta(1)),
                  pl.BlockSpec((BLOCK, KV_WIDTH), prev),
                  pl.BlockSpec((ATT_Q, KV_WIDTH), cur),
                  pl.BlockSpec((N_META, KV_WIDTH), meta(0)),
                  pl.BlockSpec((N_META, KV_WIDTH), meta(1)),
                  pl.BlockSpec((CONV_HALO, CONV_CH), u_prev),
                  pl.BlockSpec((ATT_Q, CONV_CH), cur),
                  pl.BlockSpec((CONV_WIDTH, CONV_CH), fixed),
                  _vec(CONV_CH), _vec(CONV_CH), _vec(CONV_CH)],
        out_specs=[pl.BlockSpec((ATT_Q, ATTN_WIDTH), lambda s: (s, 0)),
                   pl.BlockSpec((ATT_Q, CONV_CH), lambda s: (s, 0))],
        out_shape=[jax.ShapeDtypeStruct((T, ATTN_WIDTH), BF),
                   jax.ShapeDtypeStruct((T, CONV_CH), BF)],
        scratch_shapes=_conv_scratch(ATT_Q),
        compiler_params=_cparams(1),
        name="prompt_attn_conv",
    )(sinks, bias, q, k, k, k, k, v, v, v, v, u, u, w_dw, b_dw.reshape(1, CONV_CH),
      g.reshape(1, CONV_CH), b.reshape(1, CONV_CH))


SAMPLE_ROWS = N_KV_HEADS * DEC_SEQ * (N_Q_HEADS // N_KV_HEADS)


N_SMALL = N_META + DEC_SEQ


def _sattn_kernel(sk_ref, q_ref, kw_ref, ks_ref, vw_ref, vs_ref, o_ref, nk_ref, nv_ref):
    q = q_ref[...]
    nt = (((2,), (2,)), ((0,), (0,)))
    s_win = lax.dot_general(q, kw_ref[...].astype(BF), nt, preferred_element_type=F32)
    s_small = lax.dot_general(q, ks_ref[...].astype(BF), nt, preferred_element_type=F32)
    per_kv = SAMPLE_ROWS // N_KV_HEADS

    def step_of_row(n_cols):
        r = lax.broadcasted_iota(jnp.int32, (1, SAMPLE_ROWS, n_cols), 1)
        return (r % per_kv) // (N_Q_HEADS // N_KV_HEADS)

    idx = lax.broadcasted_iota(jnp.int32, (1, SAMPLE_ROWS, WINDOW), 2)
    s_win = jnp.where(idx >= step_of_row(WINDOW), s_win, -jnp.inf)
    col = lax.broadcasted_iota(jnp.int32, (1, SAMPLE_ROWS, N_SMALL), 2)
    s_small = jnp.where(col - N_META <= step_of_row(N_SMALL), s_small, -jnp.inf)
    sk = sk_ref[...][None, :, :1]
    m = jnp.maximum(jnp.maximum(jnp.max(s_win, -1, keepdims=True),
                                jnp.max(s_small, -1, keepdims=True)), sk)
    p_win = jnp.exp(s_win - m)
    p_small = jnp.exp(s_small - m)
    den = (jnp.sum(p_win, -1, keepdims=True) + jnp.sum(p_small, -1, keepdims=True)
           + jnp.exp(sk - m))
    inv = 1.0 / den
    nn = (((2,), (1,)), ((0,), (0,)))
    o_ref[...] = (
        lax.dot_general((p_win * inv).astype(BF), vw_ref[...].astype(BF), nn,
                        preferred_element_type=F32)
        + lax.dot_general((p_small * inv).astype(BF), vs_ref[...].astype(BF), nn,
                          preferred_element_type=F32))
    for src_w, src_s, dst in ((kw_ref, ks_ref, nk_ref), (vw_ref, vs_ref, nv_ref)):
        dst[:, :WINDOW - DEC_SEQ, :] = src_w[:, DEC_SEQ:, :]
        dst[:, WINDOW - DEC_SEQ:, :] = src_s[:, N_META:, :]


def _sattn_call(sink_rows, qz, k_win, k_small, v_win, v_small):
    b3 = lambda i: (i, 0, 0)
    win = pl.BlockSpec((SAMPLE_BT, WINDOW, KV_WIDTH), b3)
    small = pl.BlockSpec((SAMPLE_BT, N_SMALL, KV_WIDTH), b3)
    return pl.pallas_call(
        _sattn_kernel,
        grid=(DEC_BATCH // SAMPLE_BT,),
        in_specs=[pl.BlockSpec((SAMPLE_ROWS, LANES), lambda i: (0, 0)),
                  pl.BlockSpec((SAMPLE_BT, SAMPLE_ROWS, LANES), b3),
                  win, small, win, small],
        out_specs=[pl.BlockSpec((SAMPLE_BT, SAMPLE_ROWS, LANES), b3), win, win],
        out_shape=[jax.ShapeDtypeStruct((DEC_BATCH, SAMPLE_ROWS, LANES), F32),
                   jax.ShapeDtypeStruct((DEC_BATCH, WINDOW, KV_WIDTH), F32),
                   jax.ShapeDtypeStruct((DEC_BATCH, WINDOW, KV_WIDTH), F32)],
        compiler_params=_cparams(1),
        name="sample_attn",
    )(sink_rows, qz, k_win, k_small, v_win, v_small)


def _conv_tail(y, g, b):
    return _silu(_ln_rows(y, g, b))


def _conv_rows(ext_ref, sh_ref, y_ref, w_ref, bdw_ref, g_ref, b_ref, o_ref, n_rows):
    off = CONV_HALO - CONV_STATE
    span = n_rows + CONV_HALO - SUBLANES
    for s in range(1, SUBLANES):
        sh_ref[s - 1, :span, :] = ext_ref[s:s + span, :]
    for lg in range(CONV_CH // LANES):
        lanes = slice(lg * LANES, (lg + 1) * LANES)
        for r0 in range(0, n_rows, CONV_CHUNK):
            acc = jnp.broadcast_to(bdw_ref[:, lanes], (CONV_CHUNK, LANES))
            for j in range(CONV_WIDTH):
                a, s = divmod(off + j, SUBLANES)
                src = ext_ref if s == 0 else sh_ref.at[s - 1]
                x = src[r0 + a * SUBLANES:r0 + a * SUBLANES + CONV_CHUNK, lanes]
                acc = acc + x * w_ref[j:j + 1, lanes]
            y_ref[r0:r0 + CONV_CHUNK, lanes] = acc
    for r0 in range(0, n_rows, CONV_CHUNK):
        y = y_ref[r0:r0 + CONV_CHUNK, :]
        o_ref[r0:r0 + CONV_CHUNK, :] = _conv_tail(y, g_ref[...], b_ref[...]).astype(BF)


def _conv_scratch(n_rows):
    return [pltpu.VMEM((CONV_HALO + n_rows, CONV_CH), F32),
            pltpu.VMEM((SUBLANES - 1, CONV_HALO + n_rows - SUBLANES, CONV_CH), F32),
            pltpu.VMEM((n_rows, CONV_CH), F32)]


def _mconv_kernel(cur_ref, w_ref, bdw_ref, g_ref, b_ref, o_ref, ext_ref, sh_ref, y_ref):
    ext_ref[:CONV_HALO] = jnp.zeros((CONV_HALO, CONV_CH), F32)
    ext_ref[CONV_HALO:] = cur_ref[...]
    _conv_rows(ext_ref, sh_ref, y_ref, w_ref, bdw_ref, g_ref, b_ref, o_ref, BLOCK)


def _mconv_call(u, w_dw, b_dw, g, b):
    fixed = lambda bb: (0, 0)
    return pl.pallas_call(
        _mconv_kernel,
        grid=(BATCH,),
        in_specs=[pl.BlockSpec((BLOCK, CONV_CH), lambda bb: (MET0 // BLOCK + bb, 0)),
                  pl.BlockSpec((CONV_WIDTH, CONV_CH), fixed),
                  _vec(CONV_CH), _vec(CONV_CH), _vec(CONV_CH)],
        out_specs=pl.BlockSpec((BLOCK, CONV_CH), lambda bb: (bb, 0)),
        out_shape=jax.ShapeDtypeStruct((BATCH * BLOCK, CONV_CH), BF),
        scratch_shapes=_conv_scratch(BLOCK),
        compiler_params=_cparams(1),
        name="meta_conv",
    )(u, w_dw, b_dw.reshape(1, CONV_CH), g.reshape(1, CONV_CH), b.reshape(1, CONV_CH))


def _sconv_kernel(st_ref, u_ref, w_ref, bdw_ref, g_ref, b_ref, o_ref, new_ref, y_ref):
    l = pl.program_id(0)

    def ext_row(r):
        if r < CONV_STATE:
            return st_ref[0, r]
        return u_ref[pl.ds(r - CONV_STATE, DEC_BATCH, stride=DEC_SEQ), :]

    rows = [ext_row(r) for r in range(CONV_STATE + DEC_SEQ)]
    for r in range(CONV_STATE):
        new_ref[r] = rows[r + DEC_SEQ]
    for s in range(DEC_SEQ):
        acc = jnp.broadcast_to(bdw_ref[...], (DEC_BATCH, LANES))
        for j in range(CONV_WIDTH):
            acc = acc + rows[s + j] * w_ref[j:j + 1, :]
        y_ref[l, s] = acc

    @pl.when(l == CONV_CH // LANES - 1)
    def _():
        for s in range(DEC_SEQ):
            y = jnp.concatenate([y_ref[c, s] for c in range(CONV_CH // LANES)], axis=-1)
            o_ref[s] = _conv_tail(y, g_ref[...], b_ref[...]).astype(BF)


def _sconv_call(state_t, layer, u, w_dw, b_dw, g, b):
    grp = lambda rows: pl.BlockSpec((rows, LANES), lambda l: (0, l))
    full = lambda shape: pl.BlockSpec(shape, lambda l: (0,) * len(shape))
    return pl.pallas_call(
        _sconv_kernel,
        grid=(CONV_CH // LANES,),
        in_specs=[pl.BlockSpec((1, CONV_STATE, DEC_BATCH, LANES), lambda l: (layer, 0, 0, l)),
                  grp(NS), grp(CONV_WIDTH), grp(1),
                  full((1, CONV_CH)), full((1, CONV_CH))],
        out_specs=[full((DEC_SEQ, DEC_BATCH, CONV_CH)),
                   pl.BlockSpec((CONV_STATE, DEC_BATCH, LANES), lambda l: (0, 0, l))],
        out_shape=[jax.ShapeDtypeStruct((DEC_SEQ, DEC_BATCH, CONV_CH), BF),
                   jax.ShapeDtypeStruct((CONV_STATE, DEC_BATCH, CONV_CH), F32)],
        scratch_shapes=[pltpu.VMEM((CONV_CH // LANES, DEC_SEQ, DEC_BATCH, LANES), F32)],
        compiler_params=_cparams(1),
        name="sample_conv",
    )(state_t, u, w_dw, b_dw.reshape(1, CONV_CH), g.reshape(1, CONV_CH), b.reshape(1, CONV_CH))


def _mix(a_ref, c_ref, h_ref, wo_ref, g1_ref, b1_ref):
    mix = (jnp.dot(a_ref[...], wo_ref[0, :ATTN_WIDTH, :], preferred_element_type=F32)
           + jnp.dot(c_ref[...], wo_ref[0, ATTN_WIDTH:, :], preferred_element_type=F32))
    return _ln_rows(ALPHA * h_ref[...] + mix, g1_ref[...], b1_ref[...])


def _dense_kernel(a_ref, c_ref, h_ref, wo_ref, g1_ref, b1_ref, wg_ref, wu_ref, wd_ref,
                  g2_ref, b2_ref, o_ref):
    h1 = _mix(a_ref, c_ref, h_ref, wo_ref, g1_ref, b1_ref)
    hb = h1.astype(BF)
    gate = jnp.dot(hb, wg_ref[0], preferred_element_type=F32)
    up = jnp.dot(hb, wu_ref[0], preferred_element_type=F32)
    f = jnp.dot((_silu(gate) * up).astype(BF), wd_ref[0], preferred_element_type=F32)
    o_ref[...] = _ln_rows(ALPHA * h1 + f, g2_ref[...], b2_ref[...])


def _dense_call(a, c, h, wo, layer, g1, b1, wg, wu, wd, j, g2, b2):
    row = lambda i: (i, 0)
    once = dict(pipeline_mode=pl.Buffered(1))
    return pl.pallas_call(
        _dense_kernel,
        grid=(T // TM,),
        in_specs=[pl.BlockSpec((TM, ATTN_WIDTH), row),
                  pl.BlockSpec((TM, CONV_CH), row),
                  pl.BlockSpec((TM, D_MODEL), row),
                  pl.BlockSpec((1, D_MODEL, D_MODEL), lambda i: (layer, 0, 0), **once),
                  _vec(D_MODEL), _vec(D_MODEL),
                  pl.BlockSpec((1, D_MODEL, D_FF_DENSE), lambda i: (j, 0, 0), **once),
                  pl.BlockSpec((1, D_MODEL, D_FF_DENSE), lambda i: (j, 0, 0), **once),
                  pl.BlockSpec((1, D_FF_DENSE, D_MODEL), lambda i: (j, 0, 0), **once),
                  _vec(D_MODEL), _vec(D_MODEL)],
        out_specs=pl.BlockSpec((TM, D_MODEL), row),
        out_shape=jax.ShapeDtypeStruct((T, D_MODEL), F32),
        compiler_params=_cparams(1),
        name="mix_dense_ffn",
    )(a, c, h, wo, g1.reshape(1, -1), b1.reshape(1, -1), wg, wu, wd,
      g2.reshape(1, -1), b2.reshape(1, -1))


def _router_kernel(a_ref, c_ref, h_ref, wo_ref, g1_ref, b1_ref, rwh_ref, rwl_ref, rb_ref,
                   h1_ref, comb_ref, sel_ref):
    h1 = _mix(a_ref, c_ref, h_ref, wo_ref, g1_ref, b1_ref)
    h1_ref[...] = h1
    xh = h1.astype(BF)
    xl = (h1 - xh.astype(F32)).astype(BF)
    logits = (jnp.dot(xh, rwh_ref[...], preferred_element_type=F32)
              + jnp.dot(xl, rwh_ref[...], preferred_element_type=F32)
              + jnp.dot(xh, rwl_ref[...], preferred_element_type=F32)) + rb_ref[...]
    lane = lax.broadcasted_iota(jnp.int32, (TM, LANES), 1)
    logits = jnp.where(lane < N_EXPERTS, logits, -jnp.inf)
    v1 = jnp.max(logits, -1, keepdims=True)
    i1 = jnp.min(jnp.where(logits == v1, lane, LANES), -1, keepdims=True)
    rest = jnp.where(lane == i1, -jnp.inf, logits)
    v2 = jnp.max(rest, -1, keepdims=True)
    i2 = jnp.min(jnp.where(rest == v2, lane, LANES), -1, keepdims=True)
    e2 = jnp.exp(v2 - v1)
    den = 1.0 + e2
    comb_ref[...] = jnp.where(lane == i1, 1.0 / den, jnp.where(lane == i2, e2 / den, 0.0))
    sel_ref[...] = jnp.where((lane == i1) | (lane == i2), 1.0, 0.0).astype(BF)


def _router_call(a, c, h, wo, layer, g1, b1, rwh, rwl, rb):
    row = lambda i: (i, 0)
    fixed = lambda i: (0, 0)
    return pl.pallas_call(
        _router_kernel,
        grid=(T // TM,),
        in_specs=[pl.BlockSpec((TM, ATTN_WIDTH), row),
                  pl.BlockSpec((TM, CONV_CH), row),
                  pl.BlockSpec((TM, D_MODEL), row),
                  pl.BlockSpec((1, D_MODEL, D_MODEL), lambda i: (layer, 0, 0)),
                  _vec(D_MODEL), _vec(D_MODEL),
                  pl.BlockSpec((D_MODEL, LANES), fixed),
                  pl.BlockSpec((D_MODEL, LANES), fixed),
                  _vec(LANES)],
        out_specs=[pl.BlockSpec((TM, D_MODEL), row),
                   pl.BlockSpec((TM, LANES), row),
                   pl.BlockSpec((TM, LANES), row)],
        out_shape=[jax.ShapeDtypeStruct((T, D_MODEL), F32),
                   jax.ShapeDtypeStruct((T, LANES), F32),
                   jax.ShapeDtypeStruct((T, LANES), BF)],
        compiler_params=_cparams(1),
        name="mix_router",
    )(a, c, h, wo, g1.reshape(1, -1), b1.reshape(1, -1), rwh, rwl, rb)


def _moe_kernel(x_ref, comb_ref, sel_ref, tri_ref, eg_ref, eu_ref, ed_ref, g2_ref, b2_ref,
                o_ref, xb_ref, rankr_ref, selr_ref, combr_ref, put_ref, acc_ref):
    e = pl.program_id(1)

    @pl.when(e == 0)
    def _():
        xb_ref[...] = x_ref[...].astype(BF)
        sel = sel_ref[...]
        rank = jnp.dot(tri_ref[...], sel, preferred_element_type=F32)
        rankr_ref[...] = rank.T
        selr_ref[...] = sel.astype(F32).T
        combr_ref[...] = comb_ref[...].T
        acc_ref[...] = jnp.zeros_like(acc_ref)

    sel_row = selr_ref[pl.ds(e, 1), :]
    rank_row = rankr_ref[pl.ds(e, 1), :]
    comb_row = combr_ref[pl.ds(e, 1), :]
    count = jnp.sum(sel_row).astype(jnp.int32)

    def expert_pass(first, size):
        base = first.astype(F32)
        slot = lax.broadcasted_iota(jnp.int32, (size, TM_MOE), 0).astype(F32)
        take = jnp.where((rank_row - base == slot) & (sel_row > 0.0), 1.0, 0.0)
        weight = jnp.sum(take * comb_row, -1, keepdims=True)
        xc = jnp.dot(take.astype(BF), xb_ref[...], preferred_element_type=F32).astype(BF)
        gate = jnp.dot(xc, eg_ref[0, 0], preferred_element_type=F32)
        up = jnp.dot(xc, eu_ref[0, 0], preferred_element_type=F32)
        out = jnp.dot((_silu(gate) * up).astype(BF), ed_ref[0, 0], preferred_element_type=F32)
        put_ref[:, :size] = take.T.astype(BF)
        acc_ref[...] += jnp.dot(put_ref[:, :size], (out * weight).astype(BF),
                                preferred_element_type=F32)

    n_big = (count + MOE_CHUNK - 1) // (2 * MOE_CHUNK)

    def big_pass(ci, carry):
        expert_pass(ci * (2 * MOE_CHUNK), 2 * MOE_CHUNK)
        return carry

    lax.fori_loop(0, n_big, big_pass, 0)

    @pl.when(count > n_big * (2 * MOE_CHUNK))
    def _():
        expert_pass(n_big * (2 * MOE_CHUNK), MOE_CHUNK)

    @pl.when(e == N_EXPERTS - 1)
    def _():
        o_ref[...] = _ln_rows(ALPHA * x_ref[...] + acc_ref[...], g2_ref[...], b2_ref[...])


def _moe_call(x, comb, sel, tri, eg, eu, ed, j, g2, b2, tile0, n_tiles):
    row = lambda i, e: (tile0 + i, 0)
    fixed = lambda i, e: (0, 0)
    exp = lambda i, e: (j, e, 0, 0)
    return pl.pallas_call(
        _moe_kernel,
        grid=(n_tiles, N_EXPERTS),
        in_specs=[pl.BlockSpec((TM_MOE, D_MODEL), row),
                  pl.BlockSpec((TM_MOE, LANES), row),
                  pl.BlockSpec((TM_MOE, LANES), row),
                  pl.BlockSpec((TM_MOE, TM_MOE), fixed),
                  pl.BlockSpec((1, 1, D_MODEL, D_FF_EXPERT), exp),
                  pl.BlockSpec((1, 1, D_MODEL, D_FF_EXPERT), exp),
                  pl.BlockSpec((1, 1, D_FF_EXPERT, D_MODEL), exp),
                  _vec(D_MODEL), _vec(D_MODEL)],
        out_specs=pl.BlockSpec((TM_MOE, D_MODEL), lambda i, e: (i, 0)),
        out_shape=jax.ShapeDtypeStruct((n_tiles * TM_MOE, D_MODEL), F32),
        scratch_shapes=[pltpu.VMEM((TM_MOE, D_MODEL), BF),
                        pltpu.VMEM((LANES, TM_MOE), F32),
                        pltpu.VMEM((LANES, TM_MOE), F32),
                        pltpu.VMEM((LANES, TM_MOE), F32),
                        pltpu.VMEM((TM_MOE, 2 * MOE_CHUNK), BF),
                        pltpu.VMEM((TM_MOE, D_MODEL), F32)],
        compiler_params=_cparams(2),
        name="moe_ffn",
    )(x, comb, sel, tri, eg, eu, ed, g2.reshape(1, -1), b2.reshape(1, -1))


def kernel(x_prompt, x_sample, cache_meta_k, cache_meta_v, cache_k, cache_v, state_conv,
           meta_tokens, ln_in_g, ln_in_b, w_in, w_dw, b_dw, conv_ln_g, conv_ln_b, sinks, w_out,
           ln1_g, ln1_b, ln2_g, ln2_b, ffd_w_gate, ffd_w_up, ffd_w_down,
           router_w, router_b, exp_w_gate, exp_w_up, exp_w_down):
    meta_blk = jnp.concatenate([meta_tokens, jnp.zeros((BLOCK - N_META, D_MODEL), F32)], 0)
    x_tail = jnp.concatenate([meta_blk] * BATCH + [x_sample.reshape(NS, D_MODEL),
                                                   jnp.zeros((T - PAD0, D_MODEL), F32)], 0)
    cos_np, sin_np = _rope_tables()
    cos_t, sin_t = jnp.asarray(cos_np), jnp.asarray(sin_np)
    tri = jnp.asarray(np.tril(np.ones((TM_MOE, TM_MOE), np.float32), -1), dtype=BF)
    bias = jnp.asarray(_attn_bias())
    w_in_b, w_out_b = w_in.astype(BF), w_out.astype(BF)
    ffd_b = (ffd_w_gate.astype(BF), ffd_w_up.astype(BF), ffd_w_down.astype(BF))
    exp_b = (exp_w_gate.astype(BF), exp_w_up.astype(BF), exp_w_down.astype(BF))

    state_t = state_conv.transpose(0, 2, 1, 3)

    per_kv = SAMPLE_ROWS // N_KV_HEADS
    head_of_row = (jnp.arange(SAMPLE_ROWS) // per_kv) * 4 + jnp.arange(SAMPLE_ROWS) % 4
    met_rows = lambda x, b: x[MET0 + b * BLOCK:MET0 + b * BLOCK + N_META]
    kv5 = lambda rows: jnp.stack(rows).reshape(BATCH, -1, N_KV_HEADS, HEAD_DIM)

    pmk, pmv, pk, pv, pc, sk, sv, sc = [], [], [], [], [], [], [], []
    y_tok = y_tail = None
    for i in range(DEPTH):
        if i == 0:
            h, q, k, v, u = _inproj0_call(x_prompt.reshape(TOK, D_MODEL), x_tail, ln_in_g,
                                          ln_in_b, w_in_b, cos_t, sin_t)
        else:
            q, k, v, u = _inproj_call(h, w_in_b, i, cos_t, sin_t)

        a, c = _prompt_call(sinks[i], bias, q, k, v, u, w_dw[i], b_dw[i], conv_ln_g[i],
                            conv_ln_b[i])
        c_m = _mconv_call(u, w_dw[i], b_dw[i], conv_ln_g[i], conv_ln_b[i])
        pmk.append(kv5([met_rows(k, b) for b in range(BATCH)]))
        pmv.append(kv5([met_rows(v, b) for b in range(BATCH)]))
        pk.append(kv5([k[(b + 1) * SEQ - WINDOW:(b + 1) * SEQ] for b in range(BATCH)]))
        pv.append(kv5([v[(b + 1) * SEQ - WINDOW:(b + 1) * SEQ] for b in range(BATCH)]))
        pc.append(jnp.stack([u[(b + 1) * SEQ - CONV_STATE:(b + 1) * SEQ] for b in range(BATCH)]))

        k2 = k[SMP0:PAD0].reshape(DEC_BATCH, DEC_SEQ, KV_WIDTH)
        v2 = v[SMP0:PAD0].reshape(DEC_BATCH, DEC_SEQ, KV_WIDTH)
        k_small = jnp.concatenate([cache_meta_k[i].reshape(DEC_BATCH, N_META, KV_WIDTH), k2], 1)
        v_small = jnp.concatenate([cache_meta_v[i].reshape(DEC_BATCH, N_META, KV_WIDTH), v2], 1)
        q2 = q[SMP0:PAD0].reshape(DEC_BATCH, DEC_SEQ, N_KV_HEADS, 4, HEAD_DIM)
        q2 = q2.transpose(0, 2, 1, 3, 4).reshape(DEC_BATCH, N_KV_HEADS, per_kv, HEAD_DIM)
        zq = jnp.zeros_like(q2[:, 0])
        qz = jnp.concatenate([jnp.concatenate([q2[:, 0], zq], -1),
                              jnp.concatenate([zq, q2[:, 1]], -1)], 1)
        sink_rows = jnp.broadcast_to(sinks[i][head_of_row][:, None], (SAMPLE_ROWS, LANES))
        o2, nk, nv = _sattn_call(sink_rows, qz,
                                 cache_k[i].reshape(DEC_BATCH, WINDOW, KV_WIDTH), k_small,
                                 cache_v[i].reshape(DEC_BATCH, WINDOW, KV_WIDTH), v_small)
        o2 = jnp.stack([o2[:, :per_kv, :HEAD_DIM], o2[:, per_kv:, HEAD_DIM:]], 1)
        a_s = o2.reshape(DEC_BATCH, N_KV_HEADS, DEC_SEQ, 4, HEAD_DIM).transpose(0, 2, 1, 3, 4)
        a_s = a_s.reshape(NS, ATTN_WIDTH).astype(BF)
        u2 = u[SMP0:PAD0]
        c_s, sc_i = _sconv_call(state_t, i, u2, w_dw[i], b_dw[i], conv_ln_g[i], conv_ln_b[i])
        c_s = c_s.transpose(1, 0, 2).reshape(NS, CONV_CH)
        sk.append(nk.reshape(DEC_BATCH, WINDOW, N_KV_HEADS, HEAD_DIM))
        sv.append(nv.reshape(DEC_BATCH, WINDOW, N_KV_HEADS, HEAD_DIM))
        sc.append(sc_i)

        a = lax.dynamic_update_slice(a, a_s, (SMP0, 0))
        c = lax.dynamic_update_slice(c, jnp.concatenate([c_m, c_s], 0), (MET0, 0))
        j = i // 2
        if i % 2 == 0:
            h = _dense_call(a, c, h, w_out_b, i, ln1_g[i], ln1_b[i], *ffd_b, j,
                            ln2_g[i], ln2_b[i])
        else:
            rw = jnp.pad(router_w[j], ((0, 0), (0, LANES - N_EXPERTS)))
            rwh = rw.astype(BF)
            rwl = (rw - rwh.astype(F32)).astype(BF)
            rb = jnp.pad(router_b[j], (0, LANES - N_EXPERTS)).reshape(1, LANES)
            h1, comb, sel = _router_call(a, c, h, w_out_b, i, ln1_g[i], ln1_b[i], rwh, rwl, rb)
            moe = functools.partial(_moe_call, h1, comb, sel, tri, *exp_b, j,
                                    ln2_g[i], ln2_b[i])
            if i < DEPTH - 1:
                h = moe(0, T // TM_MOE)
            else:
                y_tok = moe(0, TOK // TM_MOE)
                y_tail = moe(TOK // TM_MOE, TAIL // TM_MOE)

    y_prompt = y_tok.reshape(BATCH, SEQ, D_MODEL)
    y_sample = y_tail[SMP0 - TOK:PAD0 - TOK].reshape(DEC_BATCH, DEC_SEQ, D_MODEL)
    return (y_prompt, y_sample, jnp.stack(pmk), jnp.stack(pmv), jnp.stack(pk), jnp.stack(pv),
            jnp.stack(pc), jnp.stack(sk), jnp.stack(sv), jnp.stack(sc).transpose(0, 2, 1, 3))
```

```python
import functools

import numpy as np
import jax
import jax.numpy as jnp
from jax import lax
from jax.experimental import pallas as pl
from jax.experimental.pallas import tpu as pltpu

D_MODEL = 1024
BATCH = 2
SEQ = 8192
DEPTH = 4
DEC_BATCH = 128
DEC_SEQ = 4
PAST_LEN = 8192
N_META = 16
HEAD_DIM = 64
N_Q_HEADS = 8
N_KV_HEADS = 2
ATTN_WIDTH = N_Q_HEADS * HEAD_DIM
KV_WIDTH = N_KV_HEADS * HEAD_DIM
CONV_CH = D_MODEL - ATTN_WIDTH
Q_END = ATTN_WIDTH
K_END = Q_END + KV_WIDTH
V_END = K_END + KV_WIDTH
IN_COLS = V_END + 2 * CONV_CH
CONV_WIDTH = 31
CONV_STATE = CONV_WIDTH - 1
WINDOW = 128
BLOCK = 128
ROPE_THETA = 10000.0
ATTN_SCALE = HEAD_DIM ** -0.5
D_FF_DENSE = 2816
N_EXPERTS = 8
D_FF_EXPERT = 1024
ALPHA = (2 * DEPTH) ** 0.25
LN_EPS = 1e-5

LANES = 128
SUBLANES = 8
TOK = BATCH * SEQ
MET0 = TOK
SMP0 = MET0 + BATCH * BLOCK
NS = DEC_BATCH * DEC_SEQ
PAD0 = SMP0 + NS
TM = 512
TM_MOE = 1024
T = -(-PAD0 // TM_MOE) * TM_MOE
TAIL = T - TOK
ATT_NB = 2
ATT_Q = ATT_NB * BLOCK
ATT_STEPS = SEQ // ATT_Q
N_KEYS = 2 * BLOCK + N_META
CONV_HALO = 32
CONV_CHUNK = 64
SAMPLE_BT = 16
MOE_CHUNK = 128
MOE_WIDE = 320
VMEM_LIMIT = 56 * 1024 * 1024

BF = jnp.bfloat16
F32 = jnp.float32


def _cparams(n_axes):
    return pltpu.CompilerParams(dimension_semantics=("arbitrary",) * n_axes,
                                vmem_limit_bytes=VMEM_LIMIT)


def _ln_rows(x, g, b):
    mu = jnp.mean(x, -1, keepdims=True)
    xc = x - mu
    var = jnp.mean(xc * xc, -1, keepdims=True)
    return xc * lax.rsqrt(var + LN_EPS) * g + b


def _silu(x):
    return x * jax.nn.sigmoid(x)


def _vec(n):
    return pl.BlockSpec((1, n), lambda *_: (0, 0))


def _ln_kernel(xt_ref, xr_ref, g_ref, b_ref, o_ref):
    i = pl.program_id(0)

    @pl.when(i < TOK // TM)
    def _():
        o_ref[...] = _ln_rows(xt_ref[...], g_ref[...], b_ref[...])

    @pl.when(i >= TOK // TM)
    def _():
        o_ref[...] = _ln_rows(xr_ref[...], g_ref[...], b_ref[...])


def _project(h_ref, w_ref, cos_ref, sin_ref, q_ref, k_ref, v_ref, u_ref):
    hb = h_ref[...].astype(BF)
    cos = cos_ref[...]
    sin = sin_ref[...]
    lane = lax.broadcasted_iota(jnp.int32, (TM, LANES), 1)
    first_half = (lane % HEAD_DIM) < (HEAD_DIM // 2)

    def rope(x):
        rot = jnp.where(first_half, pltpu.roll(x, LANES - HEAD_DIM // 2, 1),
                        pltpu.roll(x, HEAD_DIM // 2, 1))
        return x * cos + rot * sin

    qk = jnp.dot(hb, w_ref[0, :, :K_END], preferred_element_type=F32)
    for c in range(Q_END // LANES):
        q_ref[:, c * LANES:(c + 1) * LANES] = (
            rope(qk[:, c * LANES:(c + 1) * LANES]) * ATTN_SCALE).astype(BF)
    k_ref[...] = rope(qk[:, Q_END:K_END])
    v_ref[...] = jnp.dot(hb, w_ref[0, :, K_END:V_END], preferred_element_type=F32)
    val = jnp.dot(hb, w_ref[0, :, V_END:V_END + CONV_CH], preferred_element_type=F32)
    gate = jnp.dot(hb, w_ref[0, :, V_END + CONV_CH:], preferred_element_type=F32)
    u_ref[...] = val * jax.nn.sigmoid(gate)


def _inproj_kernel(h_ref, w_ref, cos_ref, sin_ref, q_ref, k_ref, v_ref, u_ref):
    _project(h_ref, w_ref, cos_ref, sin_ref, q_ref, k_ref, v_ref, u_ref)


def _inproj0_kernel(xt_ref, xr_ref, g_ref, b_ref, w_ref, cos_ref, sin_ref,
                    h_ref, q_ref, k_ref, v_ref, u_ref):
    _ln_kernel(xt_ref, xr_ref, g_ref, b_ref, h_ref)
    _project(h_ref, w_ref, cos_ref, sin_ref, q_ref, k_ref, v_ref, u_ref)


def _rope_tables():
    half = HEAD_DIM // 2
    pos = np.zeros((SEQ + TAIL,), np.float64)
    pos[:SEQ] = N_META + np.arange(SEQ)
    for b in range(BATCH):
        pos[SEQ + b * BLOCK:SEQ + b * BLOCK + N_META] = np.arange(N_META)
    s0 = SEQ + SMP0 - MET0
    pos[s0:s0 + NS] = PAST_LEN + np.tile(np.arange(DEC_SEQ), DEC_BATCH)
    inv = ROPE_THETA ** (-np.arange(half, dtype=np.float64) / half)
    ang = pos[:, None] * inv[None, :]
    cos = np.cos(ang).astype(np.float32)
    sin = np.sin(ang).astype(np.float32)
    return np.tile(cos, (1, 4)), np.tile(np.concatenate([-sin, sin], -1), (1, 2))


def _inproj_specs(layer):
    row = lambda i: (i, 0)
    per_batch = SEQ // TM
    tab = lambda i: (jnp.where(i < 2 * per_batch, i % per_batch, i - per_batch), 0)
    in_specs = [pl.BlockSpec((1, D_MODEL, IN_COLS), lambda i: (layer, 0, 0)),
                pl.BlockSpec((TM, LANES), tab),
                pl.BlockSpec((TM, LANES), tab)]
    out_specs = [pl.BlockSpec((TM, ATTN_WIDTH), row),
                 pl.BlockSpec((TM, KV_WIDTH), row),
                 pl.BlockSpec((TM, KV_WIDTH), row),
                 pl.BlockSpec((TM, CONV_CH), row)]
    out_shape = [jax.ShapeDtypeStruct((T, ATTN_WIDTH), BF),
                 jax.ShapeDtypeStruct((T, KV_WIDTH), F32),
                 jax.ShapeDtypeStruct((T, KV_WIDTH), F32),
                 jax.ShapeDtypeStruct((T, CONV_CH), F32)]
    return in_specs, out_specs, out_shape


def _inproj_call(h, w_bf, layer, cos_t, sin_t):
    in_specs, out_specs, out_shape = _inproj_specs(layer)
    return pl.pallas_call(
        _inproj_kernel,
        grid=(T // TM,),
        in_specs=[pl.BlockSpec((TM, D_MODEL), lambda i: (i, 0))] + in_specs,
        out_specs=out_specs,
        out_shape=out_shape,
        compiler_params=_cparams(1),
        name="inproj",
    )(h, w_bf, cos_t, sin_t)


def _inproj0_call(x_tok, x_tail, g, b, w_bf, cos_t, sin_t):
    in_specs, out_specs, out_shape = _inproj_specs(0)
    n_tok = TOK // TM
    row = pl.BlockSpec((TM, D_MODEL), lambda i: (i, 0))
    return pl.pallas_call(
        _inproj0_kernel,
        grid=(T // TM,),
        in_specs=[pl.BlockSpec((TM, D_MODEL), lambda i: (jnp.minimum(i, n_tok - 1), 0)),
                  pl.BlockSpec((TM, D_MODEL), lambda i: (jnp.maximum(i - n_tok, 0), 0)),
                  _vec(D_MODEL), _vec(D_MODEL)] + in_specs,
        out_specs=[row] + out_specs,
        out_shape=[jax.ShapeDtypeStruct((T, D_MODEL), F32)] + out_shape,
        compiler_params=_cparams(1),
        name="ln_inproj",
    )(x_tok, x_tail, g.reshape(1, D_MODEL), b.reshape(1, D_MODEL), w_bf, cos_t, sin_t)


def _head_variants(x):
    lane = lax.broadcasted_iota(jnp.int32, x.shape, x.ndim - 1)
    lo = lane < HEAD_DIM
    xr = pltpu.roll(x, HEAD_DIM, x.ndim - 1)
    zero = jnp.zeros_like(x)
    return [[jnp.where(lo, x, zero).astype(BF), jnp.where(lo, zero, xr).astype(BF)],
            [jnp.where(lo, xr, zero).astype(BF), jnp.where(lo, zero, x).astype(BF)]]


def _attn_bias():
    qi = np.arange(2 * BLOCK)[:, None] % BLOCK
    col = np.arange(N_KEYS)[None, :]
    d = BLOCK + qi - col
    band = (col < 2 * BLOCK) & (d >= 0) & (d <= WINDOW)
    is_meta = col >= 2 * BLOCK
    normal = band | is_meta
    first = (band & (col >= BLOCK)) | is_meta
    meta_q = is_meta & ((col - 2 * BLOCK) <= qi)
    return np.where(np.stack([normal, first, meta_q]), 0.0, -np.inf).astype(np.float32)


def _attend(sinks_ref, bias_ref, variant, q_ref, kp_ref, kc_ref, km_refs, vp_ref, vc_ref,
            vm_refs, o_ref):
    upper = lax.broadcasted_iota(jnp.int32, (2 * BLOCK, 1), 0) < BLOCK

    def keys_of(prev_ref, cur_ref, meta_refs, h):
        prev = prev_ref[...] if h == 0 else cur_ref[(h - 1) * BLOCK:h * BLOCK, :]
        cur = cur_ref[h * BLOCK:(h + 1) * BLOCK, :]
        return jnp.concatenate([prev, cur, meta_refs[min(h, 1)][...]], axis=0)

    for h in range(ATT_NB):
        rows = slice(h * BLOCK, (h + 1) * BLOCK)
        kvar = _head_variants(keys_of(kp_ref, kc_ref, km_refs, h))
        vvar = _head_variants(keys_of(vp_ref, vc_ref, vm_refs, h))
        bias = bias_ref[variant[h]]
        for g in range(N_KV_HEADS):
            qg = jnp.concatenate([q_ref[rows, (2 * g) * LANES:(2 * g + 1) * LANES],
                                  q_ref[rows, (2 * g + 1) * LANES:(2 * g + 2) * LANES]],
                                 axis=0)
            acc = jnp.zeros((2 * BLOCK, LANES), F32)
            for e in range(2):
                s = lax.dot_general(qg, kvar[g][e], (((1,), (1,)), ((), ())),
                                    preferred_element_type=F32) + bias
                sk = jnp.where(upper, sinks_ref[4 * g + e], sinks_ref[4 * g + 2 + e])
                m = jnp.maximum(jnp.max(s, -1, keepdims=True), sk)
                p = jnp.exp(s - m)
                den = jnp.sum(p, -1, keepdims=True) + jnp.exp(sk - m)
                pn = (p * (1.0 / den)).astype(BF)
                acc = acc + jnp.dot(pn, vvar[g][e], preferred_element_type=F32)
            o_ref[rows, (2 * g) * LANES:(2 * g + 1) * LANES] = acc[:BLOCK].astype(BF)
            o_ref[rows, (2 * g + 1) * LANES:(2 * g + 2) * LANES] = acc[BLOCK:].astype(BF)


def _prompt_kernel(sinks_ref, bias_ref, q_ref, kp_ref, kc_ref, km0_ref, km1_ref,
                   vp_ref, vc_ref, vm0_ref, vm1_ref, up_ref, uc_ref, w_ref, bdw_ref, g_ref,
                   b_ref, a_ref, c_ref, ext_ref, sh_ref, y_ref):
    s_id = pl.program_id(0)
    n_tok = BATCH * ATT_STEPS
    attn_refs = (q_ref, kp_ref, kc_ref, (km0_ref, km1_ref), vp_ref, vc_ref, (vm0_ref, vm1_ref),
                 a_ref)

    @pl.when(s_id > n_tok)
    def _():
        a_ref[...] = jnp.zeros_like(a_ref)
        c_ref[...] = jnp.zeros_like(c_ref)

    @pl.when(s_id == n_tok)
    def _():
        _attend(sinks_ref, bias_ref, (2,) * ATT_NB, *attn_refs)
        c_ref[...] = jnp.zeros_like(c_ref)

    @pl.when(s_id < n_tok)
    def _():
        first = (s_id % ATT_STEPS) == 0
        _attend(sinks_ref, bias_ref, (jnp.where(first, 1, 0),) + (0,) * (ATT_NB - 1),
                *attn_refs)
        half = CONV_HALO // 2
        ext_ref[:half] = jnp.where(first, 0.0, up_ref[:half])
        ext_ref[half:CONV_HALO] = jnp.where(first, up_ref[:half], up_ref[half:])
        ext_ref[CONV_HALO:] = uc_ref[...]
        _conv_rows(ext_ref, sh_ref, y_ref, w_ref, bdw_ref, g_ref, b_ref, c_ref, ATT_Q)


def _prompt_call(sinks, bias, q, k, v, u, w_dw, b_dw, g, b):
    n_tok = BATCH * ATT_STEPS
    cur = lambda s: (jnp.minimum(s, n_tok), 0)
    fixed = lambda s: (0, 0)

    def prev(s):
        blk = jnp.where(s % ATT_STEPS == 0, ATT_NB * s, ATT_NB * s - 1)
        return (jnp.minimum(blk, ATT_NB * n_tok), 0)

    def meta(h):
        def index(s):
            b = jnp.where(s < n_tok, s // ATT_STEPS, jnp.where(s == n_tok, h, 0))
            return ((MET0 + b * BLOCK) // N_META, 0)
        return index

    def u_prev(s):
        first = (MET0 + (s // ATT_STEPS) * BLOCK) // CONV_HALO
        tok = s * (ATT_Q // CONV_HALO) - 1
        return (jnp.where(s < n_tok, jnp.where(s % ATT_STEPS == 0, first, tok), 0), 0)

    return pl.pallas_call(
        _prompt_kernel,
        grid=(T // ATT_Q,),
        in_specs=[pl.BlockSpec(memory_space=pltpu.SMEM),
                  pl.BlockSpec((3, 2 * BLOCK, N_KEYS), lambda s: (0, 0, 0)),
                  pl.BlockSpec((ATT_Q, ATTN_WIDTH), cur),
                  pl.BlockSpec((BLOCK, KV_WIDTH), prev),
                  pl.BlockSpec((ATT_Q, KV_WIDTH), cur),
                  pl.BlockSpec((N_META, KV_WIDTH), meta(0)),
                  pl.BlockSpec((N_META, KV_WIDTH), meta(1)),
                  pl.BlockSpec((BLOCK, KV_WIDTH), prev),
                  pl.BlockSpec((ATT_Q, KV_WIDTH), cur),
                  pl.BlockSpec((N_META, KV_WIDTH), meta(0)),
                  pl.BlockSpec((N_META, KV_WIDTH), meta(1)),
                  pl.BlockSpec((CONV_HALO, CONV_CH), u_prev),
                  pl.BlockSpec((ATT_Q, CONV_CH), cur),
                  pl.BlockSpec((CONV_WIDTH, CONV_CH), fixed),
                  _vec(CONV_CH), _vec(CONV_CH), _vec(CONV_CH)],
        out_specs=[pl.BlockSpec((ATT_Q, ATTN_WIDTH), lambda s: (s, 0)),
                   pl.BlockSpec((ATT_Q, CONV_CH), lambda s: (s, 0))],
        out_shape=[jax.ShapeDtypeStruct((T, ATTN_WIDTH), BF),
                   jax.ShapeDtypeStruct((T, CONV_CH), BF)],
        scratch_shapes=_conv_scratch(ATT_Q),
        compiler_params=_cparams(1),
        name="prompt_attn_conv",
    )(sinks, bias, q, k, k, k, k, v, v, v, v, u, u, w_dw, b_dw.reshape(1, CONV_CH),
      g.reshape(1, CONV_CH), b.reshape(1, CONV_CH))


SAMPLE_ROWS = N_KV_HEADS * DEC_SEQ * (N_Q_HEADS // N_KV_HEADS)


N_SMALL = N_META + DEC_SEQ


def _sattn_kernel(sk_ref, q_ref, kw_ref, ks_ref, vw_ref, vs_ref, o_ref, nk_ref, nv_ref):
    q = q_ref[...]
    nt = (((2,), (2,)), ((0,), (0,)))
    s_win = lax.dot_general(q, kw_ref[...].astype(BF), nt, preferred_element_type=F32)
    s_small = lax.dot_general(q, ks_ref[...].astype(BF), nt, preferred_element_type=F32)
    per_kv = SAMPLE_ROWS // N_KV_HEADS

    def step_of_row(n_cols):
        r = lax.broadcasted_iota(jnp.int32, (1, SAMPLE_ROWS, n_cols), 1)
        return (r % per_kv) // (N_Q_HEADS // N_KV_HEADS)

    idx = lax.broadcasted_iota(jnp.int32, (1, SAMPLE_ROWS, WINDOW), 2)
    s_win = jnp.where(idx >= step_of_row(WINDOW), s_win, -jnp.inf)
    col = lax.broadcasted_iota(jnp.int32, (1, SAMPLE_ROWS, N_SMALL), 2)
    s_small = jnp.where(col - N_META <= step_of_row(N_SMALL), s_small, -jnp.inf)
    sk = sk_ref[...][None, :, :1]
    m = jnp.maximum(jnp.maximum(jnp.max(s_win, -1, keepdims=True),
                                jnp.max(s_small, -1, keepdims=True)), sk)
    p_win = jnp.exp(s_win - m)
    p_small = jnp.exp(s_small - m)
    den = (jnp.sum(p_win, -1, keepdims=True) + jnp.sum(p_small, -1, keepdims=True)
           + jnp.exp(sk - m))
    inv = 1.0 / den
    nn = (((2,), (1,)), ((0,), (0,)))
    o_ref[...] = (
        lax.dot_general((p_win * inv).astype(BF), vw_ref[...].astype(BF), nn,
                        preferred_element_type=F32)
        + lax.dot_general((p_small * inv).astype(BF), vs_ref[...].astype(BF), nn,
                          preferred_element_type=F32))
    for src_w, src_s, dst in ((kw_ref, ks_ref, nk_ref), (vw_ref, vs_ref, nv_ref)):
        dst[:, :WINDOW - DEC_SEQ, :] = src_w[:, DEC_SEQ:, :]
        dst[:, WINDOW - DEC_SEQ:, :] = src_s[:, N_META:, :]


def _sattn_call(sink_rows, qz, k_win, k_small, v_win, v_small):
    b3 = lambda i: (i, 0, 0)
    win = pl.BlockSpec((SAMPLE_BT, WINDOW, KV_WIDTH), b3)
    small = pl.BlockSpec((SAMPLE_BT, N_SMALL, KV_WIDTH), b3)
    return pl.pallas_call(
        _sattn_kernel,
        grid=(DEC_BATCH // SAMPLE_BT,),
        in_specs=[pl.BlockSpec((SAMPLE_ROWS, LANES), lambda i: (0, 0)),
                  pl.BlockSpec((SAMPLE_BT, SAMPLE_ROWS, LANES), b3),
                  win, small, win, small],
        out_specs=[pl.BlockSpec((SAMPLE_BT, SAMPLE_ROWS, LANES), b3), win, win],
        out_shape=[jax.ShapeDtypeStruct((DEC_BATCH, SAMPLE_ROWS, LANES), F32),
                   jax.ShapeDtypeStruct((DEC_BATCH, WINDOW, KV_WIDTH), F32),
                   jax.ShapeDtypeStruct((DEC_BATCH, WINDOW, KV_WIDTH), F32)],
        compiler_params=_cparams(1),
        name="sample_attn",
    )(sink_rows, qz, k_win, k_small, v_win, v_small)


def _conv_tail(y, g, b):
    return _silu(_ln_rows(y, g, b))


def _conv_rows(ext_ref, sh_ref, y_ref, w_ref, bdw_ref, g_ref, b_ref, o_ref, n_rows):
    off = CONV_HALO - CONV_STATE
    span = n_rows + CONV_HALO - SUBLANES
    for s in range(1, SUBLANES):
        sh_ref[s - 1, :span, :] = ext_ref[s:s + span, :]
    for lg in range(CONV_CH // LANES):
        lanes = slice(lg * LANES, (lg + 1) * LANES)
        for r0 in range(0, n_rows, CONV_CHUNK):
            acc = jnp.broadcast_to(bdw_ref[:, lanes], (CONV_CHUNK, LANES))
            for j in range(CONV_WIDTH):
                a, s = divmod(off + j, SUBLANES)
                src = ext_ref if s == 0 else sh_ref.at[s - 1]
                x = src[r0 + a * SUBLANES:r0 + a * SUBLANES + CONV_CHUNK, lanes]
                acc = acc + x * w_ref[j:j + 1, lanes]
            y_ref[r0:r0 + CONV_CHUNK, lanes] = acc
    for r0 in range(0, n_rows, CONV_CHUNK):
        y = y_ref[r0:r0 + CONV_CHUNK, :]
        o_ref[r0:r0 + CONV_CHUNK, :] = _conv_tail(y, g_ref[...], b_ref[...]).astype(BF)


def _conv_scratch(n_rows):
    return [pltpu.VMEM((CONV_HALO + n_rows, CONV_CH), F32),
            pltpu.VMEM((SUBLANES - 1, CONV_HALO + n_rows - SUBLANES, CONV_CH), F32),
            pltpu.VMEM((n_rows, CONV_CH), F32)]


def _mconv_kernel(cur_ref, w_ref, bdw_ref, g_ref, b_ref, o_ref, ext_ref, sh_ref, y_ref):
    ext_ref[:CONV_HALO] = jnp.zeros((CONV_HALO, CONV_CH), F32)
    ext_ref[CONV_HALO:] = cur_ref[...]
    _conv_rows(ext_ref, sh_ref, y_ref, w_ref, bdw_ref, g_ref, b_ref, o_ref, BLOCK)


def _mconv_call(u, w_dw, b_dw, g, b):
    fixed = lambda bb: (0, 0)
    return pl.pallas_call(
        _mconv_kernel,
        grid=(BATCH,),
        in_specs=[pl.BlockSpec((BLOCK, CONV_CH), lambda bb: (MET0 // BLOCK + bb, 0)),
                  pl.BlockSpec((CONV_WIDTH, CONV_CH), fixed),
                  _vec(CONV_CH), _vec(CONV_CH), _vec(CONV_CH)],
        out_specs=pl.BlockSpec((BLOCK, CONV_CH), lambda bb: (bb, 0)),
        out_shape=jax.ShapeDtypeStruct((BATCH * BLOCK, CONV_CH), BF),
        scratch_shapes=_conv_scratch(BLOCK),
        compiler_params=_cparams(1),
        name="meta_conv",
    )(u, w_dw, b_dw.reshape(1, CONV_CH), g.reshape(1, CONV_CH), b.reshape(1, CONV_CH))


def _sconv_kernel(st_ref, u_ref, w_ref, bdw_ref, g_ref, b_ref, o_ref, new_ref, y_ref):
    l = pl.program_id(0)

    def ext_row(r):
        if r < CONV_STATE:
            return st_ref[0, r]
        return u_ref[pl.ds(r - CONV_STATE, DEC_BATCH, stride=DEC_SEQ), :]

    rows = [ext_row(r) for r in range(CONV_STATE + DEC_SEQ)]
    for r in range(CONV_STATE):
        new_ref[r] = rows[r + DEC_SEQ]
    for s in range(DEC_SEQ):
        acc = jnp.broadcast_to(bdw_ref[...], (DEC_BATCH, LANES))
        for j in range(CONV_WIDTH):
            acc = acc + rows[s + j] * w_ref[j:j + 1, :]
        y_ref[l, s] = acc

    @pl.when(l == CONV_CH // LANES - 1)
    def _():
        for s in range(DEC_SEQ):
            y = jnp.concatenate([y_ref[c, s] for c in range(CONV_CH // LANES)], axis=-1)
            o_ref[s] = _conv_tail(y, g_ref[...], b_ref[...]).astype(BF)


def _sconv_call(state_t, layer, u, w_dw, b_dw, g, b):
    grp = lambda rows: pl.BlockSpec((rows, LANES), lambda l: (0, l))
    full = lambda shape: pl.BlockSpec(shape, lambda l: (0,) * len(shape))
    return pl.pallas_call(
        _sconv_kernel,
        grid=(CONV_CH // LANES,),
        in_specs=[pl.BlockSpec((1, CONV_STATE, DEC_BATCH, LANES), lambda l: (layer, 0, 0, l)),
                  grp(NS), grp(CONV_WIDTH), grp(1),
                  full((1, CONV_CH)), full((1, CONV_CH))],
        out_specs=[full((DEC_SEQ, DEC_BATCH, CONV_CH)),
                   pl.BlockSpec((CONV_STATE, DEC_BATCH, LANES), lambda l: (0, 0, l))],
        out_shape=[jax.ShapeDtypeStruct((DEC_SEQ, DEC_BATCH, CONV_CH), BF),
                   jax.ShapeDtypeStruct((CONV_STATE, DEC_BATCH, CONV_CH), F32)],
        scratch_shapes=[pltpu.VMEM((CONV_CH // LANES, DEC_SEQ, DEC_BATCH, LANES), F32)],
        compiler_params=_cparams(1),
        name="sample_conv",
    )(state_t, u, w_dw, b_dw.reshape(1, CONV_CH), g.reshape(1, CONV_CH), b.reshape(1, CONV_CH))


def _mix(a_ref, c_ref, h_ref, wo_ref, g1_ref, b1_ref):
    mix = (jnp.dot(a_ref[...], wo_ref[0, :ATTN_WIDTH, :], preferred_element_type=F32)
           + jnp.dot(c_ref[...], wo_ref[0, ATTN_WIDTH:, :], preferred_element_type=F32))
    return _ln_rows(ALPHA * h_ref[...] + mix, g1_ref[...], b1_ref[...])


def _dense_kernel(a_ref, c_ref, h_ref, wo_ref, g1_ref, b1_ref, wg_ref, wu_ref, wd_ref,
                  g2_ref, b2_ref, o_ref):
    h1 = _mix(a_ref, c_ref, h_ref, wo_ref, g1_ref, b1_ref)
    hb = h1.astype(BF)
    gate = jnp.dot(hb, wg_ref[0], preferred_element_type=F32)
    up = jnp.dot(hb, wu_ref[0], preferred_element_type=F32)
    f = jnp.dot((_silu(gate) * up).astype(BF), wd_ref[0], preferred_element_type=F32)
    o_ref[...] = _ln_rows(ALPHA * h1 + f, g2_ref[...], b2_ref[...])


def _dense_call(a, c, h, wo, layer, g1, b1, wg, wu, wd, j, g2, b2):
    row = lambda i: (i, 0)
    once = dict(pipeline_mode=pl.Buffered(1))
    return pl.pallas_call(
        _dense_kernel,
        grid=(T // TM,),
        in_specs=[pl.BlockSpec((TM, ATTN_WIDTH), row),
                  pl.BlockSpec((TM, CONV_CH), row),
                  pl.BlockSpec((TM, D_MODEL), row),
                  pl.BlockSpec((1, D_MODEL, D_MODEL), lambda i: (layer, 0, 0), **once),
                  _vec(D_MODEL), _vec(D_MODEL),
                  pl.BlockSpec((1, D_MODEL, D_FF_DENSE), lambda i: (j, 0, 0), **once),
                  pl.BlockSpec((1, D_MODEL, D_FF_DENSE), lambda i: (j, 0, 0), **once),
                  pl.BlockSpec((1, D_FF_DENSE, D_MODEL), lambda i: (j, 0, 0), **once),
                  _vec(D_MODEL), _vec(D_MODEL)],
        out_specs=pl.BlockSpec((TM, D_MODEL), row),
        out_shape=jax.ShapeDtypeStruct((T, D_MODEL), F32),
        compiler_params=_cparams(1),
        name="mix_dense_ffn",
    )(a, c, h, wo, g1.reshape(1, -1), b1.reshape(1, -1), wg, wu, wd,
      g2.reshape(1, -1), b2.reshape(1, -1))


def _router_kernel(a_ref, c_ref, h_ref, wo_ref, g1_ref, b1_ref, rwh_ref, rwl_ref, rb_ref,
                   h1_ref, comb_ref, sel_ref):
    h1 = _mix(a_ref, c_ref, h_ref, wo_ref, g1_ref, b1_ref)
    h1_ref[...] = h1
    xh = h1.astype(BF)
    xl = (h1 - xh.astype(F32)).astype(BF)
    logits = (jnp.dot(xh, rwh_ref[...], preferred_element_type=F32)
              + jnp.dot(xl, rwh_ref[...], preferred_element_type=F32)
              + jnp.dot(xh, rwl_ref[...], preferred_element_type=F32)) + rb_ref[...]
    lane = lax.broadcasted_iota(jnp.int32, (TM, LANES), 1)
    logits = jnp.where(lane < N_EXPERTS, logits, -jnp.inf)
    v1 = jnp.max(logits, -1, keepdims=True)
    i1 = jnp.min(jnp.where(logits == v1, lane, LANES), -1, keepdims=True)
    rest = jnp.where(lane == i1, -jnp.inf, logits)
    v2 = jnp.max(rest, -1, keepdims=True)
    i2 = jnp.min(jnp.where(rest == v2, lane, LANES), -1, keepdims=True)
    e2 = jnp.exp(v2 - v1)
    den = 1.0 + e2
    comb_ref[...] = jnp.where(lane == i1, 1.0 / den, jnp.where(lane == i2, e2 / den, 0.0))
    sel_ref[...] = jnp.where((lane == i1) | (lane == i2), 1.0, 0.0).astype(BF)


def _router_call(a, c, h, wo, layer, g1, b1, rwh, rwl, rb):
    row = lambda i: (i, 0)
    fixed = lambda i: (0, 0)
    return pl.pallas_call(
        _router_kernel,
        grid=(T // TM,),
        in_specs=[pl.BlockSpec((TM, ATTN_WIDTH), row),
                  pl.BlockSpec((TM, CONV_CH), row),
                  pl.BlockSpec((TM, D_MODEL), row),
                  pl.BlockSpec((1, D_MODEL, D_MODEL), lambda i: (layer, 0, 0)),
                  _vec(D_MODEL), _vec(D_MODEL),
                  pl.BlockSpec((D_MODEL, LANES), fixed),
                  pl.BlockSpec((D_MODEL, LANES), fixed),
                  _vec(LANES)],
        out_specs=[pl.BlockSpec((TM, D_MODEL), row),
                   pl.BlockSpec((TM, LANES), row),
                   pl.BlockSpec((TM, LANES), row)],
        out_shape=[jax.ShapeDtypeStruct((T, D_MODEL), F32),
                   jax.ShapeDtypeStruct((T, LANES), F32),
                   jax.ShapeDtypeStruct((T, LANES), BF)],
        compiler_params=_cparams(1),
        name="mix_router",
    )(a, c, h, wo, g1.reshape(1, -1), b1.reshape(1, -1), rwh, rwl, rb)


def _moe_kernel(x_ref, comb_ref, sel_ref, tri_ref, eg_ref, eu_ref, ed_ref, g2_ref, b2_ref,
                o_ref, xb_ref, rankr_ref, selr_ref, combr_ref, put_ref, acc_ref):
    e = pl.program_id(1)

    @pl.when(e == 0)
    def _():
        xb_ref[...] = x_ref[...].astype(BF)
        sel = sel_ref[...]
        rank = jnp.dot(tri_ref[...], sel, preferred_element_type=F32)
        rankr_ref[...] = rank.T
        selr_ref[...] = sel.astype(F32).T
        combr_ref[...] = comb_ref[...].T
        acc_ref[...] = jnp.zeros_like(acc_ref)

    sel_row = selr_ref[pl.ds(e, 1), :]
    rank_row = rankr_ref[pl.ds(e, 1), :]
    comb_row = combr_ref[pl.ds(e, 1), :]
    count = jnp.sum(sel_row).astype(jnp.int32)

    def expert_pass(first, size):
        base = first.astype(F32)
        padded = -(-size // LANES) * LANES
        slot = lax.broadcasted_iota(jnp.int32, (padded, TM_MOE), 0).astype(F32)
        take_p = jnp.where((rank_row - base == slot) & (sel_row > 0.0), 1.0, 0.0)
        take = take_p[:size]
        weight = jnp.sum(take * comb_row, -1, keepdims=True)
        xc = jnp.dot(take.astype(BF), xb_ref[...], preferred_element_type=F32).astype(BF)
        gate = jnp.dot(xc, eg_ref[0, 0], preferred_element_type=F32)
        up = jnp.dot(xc, eu_ref[0, 0], preferred_element_type=F32)
        out = jnp.dot((_silu(gate) * up).astype(BF), ed_ref[0, 0], preferred_element_type=F32)
        put_ref[:, :padded] = take_p.T.astype(BF)
        acc_ref[...] += jnp.dot(put_ref[:, :size], (out * weight).astype(BF),
                                preferred_element_type=F32)

    one_wide = (count > 2 * MOE_CHUNK) & (count <= MOE_WIDE)

    @pl.when(one_wide)
    def _():
        expert_pass(jnp.int32(0), MOE_WIDE)

    n_big = jnp.where(one_wide, 0, (count + MOE_CHUNK - 1) // (2 * MOE_CHUNK))

    def big_pass(ci, carry):
        expert_pass(ci * (2 * MOE_CHUNK), 2 * MOE_CHUNK)
        return carry

    lax.fori_loop(0, n_big, big_pass, 0)

    @pl.when((~one_wide) & (count > n_big * (2 * MOE_CHUNK)))
    def _():
        expert_pass(n_big * (2 * MOE_CHUNK), MOE_CHUNK)

    @pl.when(e == N_EXPERTS - 1)
    def _():
        o_ref[...] = _ln_rows(ALPHA * x_ref[...] + acc_ref[...], g2_ref[...], b2_ref[...])


def _moe_call(x, comb, sel, tri, eg, eu, ed, j, g2, b2, tile0, n_tiles):
    row = lambda i, e: (tile0 + i, 0)
    fixed = lambda i, e: (0, 0)
    exp = lambda i, e: (j, e, 0, 0)
    return pl.pallas_call(
        _moe_kernel,
        grid=(n_tiles, N_EXPERTS),
        in_specs=[pl.BlockSpec((TM_MOE, D_MODEL), row),
                  pl.BlockSpec((TM_MOE, LANES), row),
                  pl.BlockSpec((TM_MOE, LANES), row),
                  pl.BlockSpec((TM_MOE, TM_MOE), fixed),
                  pl.BlockSpec((1, 1, D_MODEL, D_FF_EXPERT), exp),
                  pl.BlockSpec((1, 1, D_MODEL, D_FF_EXPERT), exp),
                  pl.BlockSpec((1, 1, D_FF_EXPERT, D_MODEL), exp),
                  _vec(D_MODEL), _vec(D_MODEL)],
        out_specs=pl.BlockSpec((TM_MOE, D_MODEL), lambda i, e: (i, 0)),
        out_shape=jax.ShapeDtypeStruct((n_tiles * TM_MOE, D_MODEL), F32),
        scratch_shapes=[pltpu.VMEM((TM_MOE, D_MODEL), BF),
                        pltpu.VMEM((LANES, TM_MOE), F32),
                        pltpu.VMEM((LANES, TM_MOE), F32),
                        pltpu.VMEM((LANES, TM_MOE), F32),
                        pltpu.VMEM((TM_MOE, -(-MOE_WIDE // LANES) * LANES), BF),
                        pltpu.VMEM((TM_MOE, D_MODEL), F32)],
        compiler_params=_cparams(2),
        name="moe_ffn",
    )(x, comb, sel, tri, eg, eu, ed, g2.reshape(1, -1), b2.reshape(1, -1))


def kernel(x_prompt, x_sample, cache_meta_k, cache_meta_v, cache_k, cache_v, state_conv,
           meta_tokens, ln_in_g, ln_in_b, w_in, w_dw, b_dw, conv_ln_g, conv_ln_b, sinks, w_out,
           ln1_g, ln1_b, ln2_g, ln2_b, ffd_w_gate, ffd_w_up, ffd_w_down,
           router_w, router_b, exp_w_gate, exp_w_up, exp_w_down):
    meta_blk = jnp.concatenate([meta_tokens, jnp.zeros((BLOCK - N_META, D_MODEL), F32)], 0)
    x_tail = jnp.concatenate([meta_blk] * BATCH + [x_sample.reshape(NS, D_MODEL),
                                                   jnp.zeros((T - PAD0, D_MODEL), F32)], 0)
    cos_np, sin_np = _rope_tables()
    cos_t, sin_t = jnp.asarray(cos_np), jnp.asarray(sin_np)
    tri = jnp.asarray(np.tril(np.ones((TM_MOE, TM_MOE), np.float32), -1), dtype=BF)
    bias = jnp.asarray(_attn_bias())
    w_in_b, w_out_b = w_in.astype(BF), w_out.astype(BF)
    ffd_b = (ffd_w_gate.astype(BF), ffd_w_up.astype(BF), ffd_w_down.astype(BF))
    exp_b = (exp_w_gate.astype(BF), exp_w_up.astype(BF), exp_w_down.astype(BF))

    state_t = state_conv.transpose(0, 2, 1, 3)

    per_kv = SAMPLE_ROWS // N_KV_HEADS
    head_of_row = (jnp.arange(SAMPLE_ROWS) // per_kv) * 4 + jnp.arange(SAMPLE_ROWS) % 4
    met_rows = lambda x, b: x[MET0 + b * BLOCK:MET0 + b * BLOCK + N_META]
    kv5 = lambda rows: jnp.stack(rows).reshape(BATCH, -1, N_KV_HEADS, HEAD_DIM)

    pmk, pmv, pk, pv, pc, sk, sv, sc = [], [], [], [], [], [], [], []
    y_tok = y_tail = None
    for i in range(DEPTH):
        if i == 0:
            h, q, k, v, u = _inproj0_call(x_prompt.reshape(TOK, D_MODEL), x_tail, ln_in_g,
                                          ln_in_b, w_in_b, cos_t, sin_t)
        else:
            q, k, v, u = _inproj_call(h, w_in_b, i, cos_t, sin_t)

        a, c = _prompt_call(sinks[i], bias, q, k, v, u, w_dw[i], b_dw[i], conv_ln_g[i],
                            conv_ln_b[i])
        c_m = _mconv_call(u, w_dw[i], b_dw[i], conv_ln_g[i], conv_ln_b[i])
        pmk.append(kv5([met_rows(k, b) for b in range(BATCH)]))
        pmv.append(kv5([met_rows(v, b) for b in range(BATCH)]))
        pk.append(kv5([k[(b + 1) * SEQ - WINDOW:(b + 1) * SEQ] for b in range(BATCH)]))
        pv.append(kv5([v[(b + 1) * SEQ - WINDOW:(b + 1) * SEQ] for b in range(BATCH)]))
        pc.append(jnp.stack([u[(b + 1) * SEQ - CONV_STATE:(b + 1) * SEQ] for b in range(BATCH)]))

        k2 = k[SMP0:PAD0].reshape(DEC_BATCH, DEC_SEQ, KV_WIDTH)
        v2 = v[SMP0:PAD0].reshape(DEC_BATCH, DEC_SEQ, KV_WIDTH)
        k_small = jnp.concatenate([cache_meta_k[i].reshape(DEC_BATCH, N_META, KV_WIDTH), k2], 1)
        v_small = jnp.concatenate([cache_meta_v[i].reshape(DEC_BATCH, N_META, KV_WIDTH), v2], 1)
        q2 = q[SMP0:PAD0].reshape(DEC_BATCH, DEC_SEQ, N_KV_HEADS, 4, HEAD_DIM)
        q2 = q2.transpose(0, 2, 1, 3, 4).reshape(DEC_BATCH, N_KV_HEADS, per_kv, HEAD_DIM)
        zq = jnp.zeros_like(q2[:, 0])
        qz = jnp.concatenate([jnp.concatenate([q2[:, 0], zq], -1),
                              jnp.concatenate([zq, q2[:, 1]], -1)], 1)
        sink_rows = jnp.broadcast_to(sinks[i][head_of_row][:, None], (SAMPLE_ROWS, LANES))
        o2, nk, nv = _sattn_call(sink_rows, qz,
                                 cache_k[i].reshape(DEC_BATCH, WINDOW, KV_WIDTH), k_small,
                                 cache_v[i].reshape(DEC_BATCH, WINDOW, KV_WIDTH), v_small)
        o2 = jnp.stack([o2[:, :per_kv, :HEAD_DIM], o2[:, per_kv:, HEAD_DIM:]], 1)
        a_s = o2.reshape(DEC_BATCH, N_KV_HEADS, DEC_SEQ, 4, HEAD_DIM).transpose(0, 2, 1, 3, 4)
        a_s = a_s.reshape(NS, ATTN_WIDTH).astype(BF)
        u2 = u[SMP0:PAD0]
        c_s, sc_i = _sconv_call(state_t, i, u2, w_dw[i], b_dw[i], conv_ln_g[i], conv_ln_b[i])
        c_s = c_s.transpose(1, 0, 2).reshape(NS, CONV_CH)
        sk.append(nk.reshape(DEC_BATCH, WINDOW, N_KV_HEADS, HEAD_DIM))
        sv.append(nv.reshape(DEC_BATCH, WINDOW, N_KV_HEADS, HEAD_DIM))
        sc.append(sc_i)

        a = lax.dynamic_update_slice(a, a_s, (SMP0, 0))
        c = lax.dynamic_update_slice(c, jnp.concatenate([c_m, c_s], 0), (MET0, 0))
        j = i // 2
        if i % 2 == 0:
            h = _dense_call(a, c, h, w_out_b, i, ln1_g[i], ln1_b[i], *ffd_b, j,
                            ln2_g[i], ln2_b[i])
        else:
            rw = jnp.pad(router_w[j], ((0, 0), (0, LANES - N_EXPERTS)))
            rwh = rw.astype(BF)
            rwl = (rw - rwh.astype(F32)).astype(BF)
            rb = jnp.pad(router_b[j], (0, LANES - N_EXPERTS)).reshape(1, LANES)
            h1, comb, sel = _router_call(a, c, h, w_out_b, i, ln1_g[i], ln1_b[i], rwh, rwl, rb)
            moe = functools.partial(_moe_call, h1, comb, sel, tri, *exp_b, j,
                                    ln2_g[i], ln2_b[i])
            if i < DEPTH - 1:
                h = moe(0, T // TM_MOE)
            else:
                y_tok = moe(0, TOK // TM_MOE)
                y_tail = moe(TOK // TM_MOE, TAIL // TM_MOE)

    y_prompt = y_tok.reshape(BATCH, SEQ, D_MODEL)
    y_sample = y_tail[SMP0 - TOK:PAD0 - TOK].reshape(DEC_BATCH, DEC_SEQ, D_MODEL)
    return (y_prompt, y_sample, jnp.stack(pmk), jnp.stack(pmv), jnp.stack(pk), jnp.stack(pv),
            jnp.stack(pc), jnp.stack(sk), jnp.stack(sv), jnp.stack(sc).transpose(0, 2, 1, 3))
```

```python
import functools

import numpy as np
import jax
import jax.numpy as jnp
from jax import lax
from jax.experimental import pallas as pl
from jax.experimental.pallas import tpu as pltpu

D_MODEL = 1024
BATCH = 2
SEQ = 8192
DEPTH = 4
DEC_BATCH = 128
DEC_SEQ = 4
PAST_LEN = 8192
N_META = 16
HEAD_DIM = 64
N_Q_HEADS = 8
N_KV_HEADS = 2
ATTN_WIDTH = N_Q_HEADS * HEAD_DIM
KV_WIDTH = N_KV_HEADS * HEAD_DIM
CONV_CH = D_MODEL - ATTN_WIDTH
Q_END = ATTN_WIDTH
K_END = Q_END + KV_WIDTH
V_END = K_END + KV_WIDTH
IN_COLS = V_END + 2 * CONV_CH
CONV_WIDTH = 31
CONV_STATE = CONV_WIDTH - 1
WINDOW = 128
BLOCK = 128
ROPE_THETA = 10000.0
ATTN_SCALE = HEAD_DIM ** -0.5
D_FF_DENSE = 2816
N_EXPERTS = 8
D_FF_EXPERT = 1024
ALPHA = (2 * DEPTH) ** 0.25
LN_EPS = 1e-5

LANES = 128
SUBLANES = 8
TOK = BATCH * SEQ
MET0 = TOK
SMP0 = MET0 + BATCH * BLOCK
NS = DEC_BATCH * DEC_SEQ
PAD0 = SMP0 + NS
TM = 512
TM_MOE = 1024
T = -(-PAD0 // TM_MOE) * TM_MOE
TAIL = T - TOK
ATT_NB = 2
ATT_Q = ATT_NB * BLOCK
ATT_STEPS = SEQ // ATT_Q
N_KEYS = 2 * BLOCK + N_META
CONV_HALO = 32
CONV_CHUNK = 64
SAMPLE_BT = 16
MOE_CHUNK = 128
MOE_WIDE = 288
VMEM_LIMIT = 56 * 1024 * 1024

BF = jnp.bfloat16
F32 = jnp.float32


def _cparams(n_axes):
    return pltpu.CompilerParams(dimension_semantics=("arbitrary",) * n_axes,
                                vmem_limit_bytes=VMEM_LIMIT)


def _ln_rows(x, g, b):
    mu = jnp.mean(x, -1, keepdims=True)
    xc = x - mu
    var = jnp.mean(xc * xc, -1, keepdims=True)
    return xc * lax.rsqrt(var + LN_EPS) * g + b


def _silu(x):
    return x * jax.nn.sigmoid(x)


def _vec(n):
    return pl.BlockSpec((1, n), lambda *_: (0, 0))


def _ln_kernel(xt_ref, xr_ref, g_ref, b_ref, o_ref):
    i = pl.program_id(0)

    @pl.when(i < TOK // TM)
    def _():
        o_ref[...] = _ln_rows(xt_ref[...], g_ref[...], b_ref[...])

    @pl.when(i >= TOK // TM)
    def _():
        o_ref[...] = _ln_rows(xr_ref[...], g_ref[...], b_ref[...])


def _project(h_ref, w_ref, cos_ref, sin_ref, q_ref, k_ref, v_ref, u_ref):
    hb = h_ref[...].astype(BF)
    cos = cos_ref[...]
    sin = sin_ref[...]
    lane = lax.broadcasted_iota(jnp.int32, (TM, LANES), 1)
    first_half = (lane % HEAD_DIM) < (HEAD_DIM // 2)

    def rope(x):
        rot = jnp.where(first_half, pltpu.roll(x, LANES - HEAD_DIM // 2, 1),
                        pltpu.roll(x, HEAD_DIM // 2, 1))
        return x * cos + rot * sin

    qk = jnp.dot(hb, w_ref[0, :, :K_END], preferred_element_type=F32)
    for c in range(Q_END // LANES):
        q_ref[:, c * LANES:(c + 1) * LANES] = (
            rope(qk[:, c * LANES:(c + 1) * LANES]) * ATTN_SCALE).astype(BF)
    k_ref[...] = rope(qk[:, Q_END:K_END])
    v_ref[...] = jnp.dot(hb, w_ref[0, :, K_END:V_END], preferred_element_type=F32)
    val = jnp.dot(hb, w_ref[0, :, V_END:V_END + CONV_CH], preferred_element_type=F32)
    gate = jnp.dot(hb, w_ref[0, :, V_END + CONV_CH:], preferred_element_type=F32)
    u_ref[...] = val * jax.nn.sigmoid(gate)


def _inproj_kernel(h_ref, w_ref, cos_ref, sin_ref, q_ref, k_ref, v_ref, u_ref):
    _project(h_ref, w_ref, cos_ref, sin_ref, q_ref, k_ref, v_ref, u_ref)


def _inproj0_kernel(xt_ref, xr_ref, g_ref, b_ref, w_ref, cos_ref, sin_ref,
                    h_ref, q_ref, k_ref, v_ref, u_ref):
    _ln_kernel(xt_ref, xr_ref, g_ref, b_ref, h_ref)
    _project(h_ref, w_ref, cos_ref, sin_ref, q_ref, k_ref, v_ref, u_ref)


def _rope_tables():
    half = HEAD_DIM // 2
    pos = np.zeros((SEQ + TAIL,), np.float64)
    pos[:SEQ] = N_META + np.arange(SEQ)
    for b in range(BATCH):
        pos[SEQ + b * BLOCK:SEQ + b * BLOCK + N_META] = np.arange(N_META)
    s0 = SEQ + SMP0 - MET0
    pos[s0:s0 + NS] = PAST_LEN + np.tile(np.arange(DEC_SEQ), DEC_BATCH)
    inv = ROPE_THETA ** (-np.arange(half, dtype=np.float64) / half)
    ang = pos[:, None] * inv[None, :]
    cos = np.cos(ang).astype(np.float32)
    sin = np.sin(ang).astype(np.float32)
    return np.tile(cos, (1, 4)), np.tile(np.concatenate([-sin, sin], -1), (1, 2))


def _inproj_specs(layer):
    row = lambda i: (i, 0)
    per_batch = SEQ // TM
    tab = lambda i: (jnp.where(i < 2 * per_batch, i % per_batch, i - per_batch), 0)
    in_specs = [pl.BlockSpec((1, D_MODEL, IN_COLS), lambda i: (layer, 0, 0)),
                pl.BlockSpec((TM, LANES), tab),
                pl.BlockSpec((TM, LANES), tab)]
    out_specs = [pl.BlockSpec((TM, ATTN_WIDTH), row),
                 pl.BlockSpec((TM, KV_WIDTH), row),
                 pl.BlockSpec((TM, KV_WIDTH), row),
                 pl.BlockSpec((TM, CONV_CH), row)]
    out_shape = [jax.ShapeDtypeStruct((T, ATTN_WIDTH), BF),
                 jax.ShapeDtypeStruct((T, KV_WIDTH), F32),
                 jax.ShapeDtypeStruct((T, KV_WIDTH), F32),
                 jax.ShapeDtypeStruct((T, CONV_CH), F32)]
    return in_specs, out_specs, out_shape


def _inproj_call(h, w_bf, layer, cos_t, sin_t):
    in_specs, out_specs, out_shape = _inproj_specs(layer)
    return pl.pallas_call(
        _inproj_kernel,
        grid=(T // TM,),
        in_specs=[pl.BlockSpec((TM, D_MODEL), lambda i: (i, 0))] + in_specs,
        out_specs=out_specs,
        out_shape=out_shape,
        compiler_params=_cparams(1),
        name="inproj",
    )(h, w_bf, cos_t, sin_t)


def _inproj0_call(x_tok, x_tail, g, b, w_bf, cos_t, sin_t):
    in_specs, out_specs, out_shape = _inproj_specs(0)
    n_tok = TOK // TM
    row = pl.BlockSpec((TM, D_MODEL), lambda i: (i, 0))
    return pl.pallas_call(
        _inproj0_kernel,
        grid=(T // TM,),
        in_specs=[pl.BlockSpec((TM, D_MODEL), lambda i: (jnp.minimum(i, n_tok - 1), 0)),
                  pl.BlockSpec((TM, D_MODEL), lambda i: (jnp.maximum(i - n_tok, 0), 0)),
                  _vec(D_MODEL), _vec(D_MODEL)] + in_specs,
        out_specs=[row] + out_specs,
        out_shape=[jax.ShapeDtypeStruct((T, D_MODEL), F32)] + out_shape,
        compiler_params=_cparams(1),
        name="ln_inproj",
    )(x_tok, x_tail, g.reshape(1, D_MODEL), b.reshape(1, D_MODEL), w_bf, cos_t, sin_t)


def _head_variants(x):
    lane = lax.broadcasted_iota(jnp.int32, x.shape, x.ndim - 1)
    lo = lane < HEAD_DIM
    xr = pltpu.roll(x, HEAD_DIM, x.ndim - 1)
    zero = jnp.zeros_like(x)
    return [[jnp.where(lo, x, zero).astype(BF), jnp.where(lo, zero, xr).astype(BF)],
            [jnp.where(lo, xr, zero).astype(BF), jnp.where(lo, zero, x).astype(BF)]]


def _attn_bias():
    qi = np.arange(2 * BLOCK)[:, None] % BLOCK
    col = np.arange(N_KEYS)[None, :]
    d = BLOCK + qi - col
    band = (col < 2 * BLOCK) & (d >= 0) & (d <= WINDOW)
    is_meta = col >= 2 * BLOCK
    normal = band | is_meta
    first = (band & (col >= BLOCK)) | is_meta
    meta_q = is_meta & ((col - 2 * BLOCK) <= qi)
    return np.where(np.stack([normal, first, meta_q]), 0.0, -np.inf).astype(np.float32)


def _attend(sinks_ref, bias_ref, variant, q_ref, kp_ref, kc_ref, km_refs, vp_ref, vc_ref,
            vm_refs, o_ref):
    upper = lax.broadcasted_iota(jnp.int32, (2 * BLOCK, 1), 0) < BLOCK

    def keys_of(prev_ref, cur_ref, meta_refs, h):
        prev = prev_ref[...] if h == 0 else cur_ref[(h - 1) * BLOCK:h * BLOCK, :]
        cur = cur_ref[h * BLOCK:(h + 1) * BLOCK, :]
        return jnp.concatenate([prev, cur, meta_refs[min(h, 1)][...]], axis=0)

    for h in range(ATT_NB):
        rows = slice(h * BLOCK, (h + 1) * BLOCK)
        kvar = _head_variants(keys_of(kp_ref, kc_ref, km_refs, h))
        vvar = _head_variants(keys_of(vp_ref, vc_ref, vm_refs, h))
        bias = bias_ref[variant[h]]
        for g in range(N_KV_HEADS):
            qg = jnp.concatenate([q_ref[rows, (2 * g) * LANES:(2 * g + 1) * LANES],
                                  q_ref[rows, (2 * g + 1) * LANES:(2 * g + 2) * LANES]],
                                 axis=0)
            acc = jnp.zeros((2 * BLOCK, LANES), F32)
            for e in range(2):
                s = lax.dot_general(qg, kvar[g][e], (((1,), (1,)), ((), ())),
                                    preferred_element_type=F32) + bias
                sk = jnp.where(upper, sinks_ref[4 * g + e], sinks_ref[4 * g + 2 + e])
                m = jnp.maximum(jnp.max(s, -1, keepdims=True), sk)
                p = jnp.exp(s - m)
                den = jnp.sum(p, -1, keepdims=True) + jnp.exp(sk - m)
                pn = (p * (1.0 / den)).astype(BF)
                acc = acc + jnp.dot(pn, vvar[g][e], preferred_element_type=F32)
            o_ref[rows, (2 * g) * LANES:(2 * g + 1) * LANES] = acc[:BLOCK].astype(BF)
            o_ref[rows, (2 * g + 1) * LANES:(2 * g + 2) * LANES] = acc[BLOCK:].astype(BF)


def _prompt_kernel(sinks_ref, bias_ref, q_ref, kp_ref, kc_ref, km0_ref, km1_ref,
                   vp_ref, vc_ref, vm0_ref, vm1_ref, up_ref, uc_ref, w_ref, bdw_ref, g_ref,
                   b_ref, a_ref, c_ref, ext_ref, sh_ref, y_ref):
    s_id = pl.program_id(0)
    n_tok = BATCH * ATT_STEPS
    attn_refs = (q_ref, kp_ref, kc_ref, (km0_ref, km1_ref), vp_ref, vc_ref, (vm0_ref, vm1_ref),
                 a_ref)

    @pl.when(s_id > n_tok)
    def _():
        a_ref[...] = jnp.zeros_like(a_ref)
        c_ref[...] = jnp.zeros_like(c_ref)

    @pl.when(s_id == n_tok)
    def _():
        _attend(sinks_ref, bias_ref, (2,) * ATT_NB, *attn_refs)
        c_ref[...] = jnp.zeros_like(c_ref)

    @pl.when(s_id < n_tok)
    def _():
        first = (s_id % ATT_STEPS) == 0
        _attend(sinks_ref, bias_ref, (jnp.where(first, 1, 0),) + (0,) * (ATT_NB - 1),
                *attn_refs)
        half = CONV_HALO // 2
        ext_ref[:half] = jnp.where(first, 0.0, up_ref[:half])
        ext_ref[half:CONV_HALO] = jnp.where(first, up_ref[:half], up_ref[half:])
        ext_ref[CONV_HALO:] = uc_ref[...]
        _conv_rows(ext_ref, sh_ref, y_ref, w_ref, bdw_ref, g_ref, b_ref, c_ref, ATT_Q)


def _prompt_call(sinks, bias, q, k, v, u, w_dw, b_dw, g, b):
    n_tok = BATCH * ATT_STEPS
    cur = lambda s: (jnp.minimum(s, n_tok), 0)
    fixed = lambda s: (0, 0)

    def prev(s):
        blk = jnp.where(s % ATT_STEPS == 0, ATT_NB * s, ATT_NB * s - 1)
        return (jnp.minimum(blk, ATT_NB * n_tok), 0)

    def meta(h):
        def index(s):
            b = jnp.where(s < n_tok, s // ATT_STEPS, jnp.where(s == n_tok, h, 0))
            return ((MET0 + b * BLOCK) // N_META, 0)
        return index

    def u_prev(s):
        first = (MET0 + (s // ATT_STEPS) * BLOCK) // CONV_HALO
        tok = s * (ATT_Q // CONV_HALO) - 1
        return (jnp.where(s < n_tok, jnp.where(s % ATT_STEPS == 0, first, tok), 0), 0)

    return pl.pallas_call(
        _prompt_kernel,
        grid=(T // ATT_Q,),
        in_specs=[pl.BlockSpec(memory_space=pltpu.SMEM),
                  pl.BlockSpec((3, 2 * BLOCK, N_KEYS), lambda s: (0, 0, 0)),
                  pl.BlockSpec((ATT_Q, ATTN_WIDTH), cur),
                  pl.BlockSpec((BLOCK, KV_WIDTH), prev),
                  pl.BlockSpec((ATT_Q, KV_WIDTH), cur),
                  pl.BlockSpec((N_META, KV_WIDTH), meta(0)),
                  pl.BlockSpec((N_META, KV_WIDTH), meta(1)),
                  pl.BlockSpec((BLOCK, KV_WIDTH), prev),
                  pl.BlockSpec((ATT_Q, KV_WIDTH), cur),
                  pl.BlockSpec((N_META, KV_WIDTH), meta(0)),
                  pl.BlockSpec((N_META, KV_WIDTH), meta(1)),
                  pl.BlockSpec((CONV_HALO, CONV_CH), u_prev),
                  pl.BlockSpec((ATT_Q, CONV_CH), cur),
                  pl.BlockSpec((CONV_WIDTH, CONV_CH), fixed),
                  _vec(CONV_CH), _vec(CONV_CH), _vec(CONV_CH)],
        out_specs=[pl.BlockSpec((ATT_Q, ATTN_WIDTH), lambda s: (s, 0)),
                   pl.BlockSpec((ATT_Q, CONV_CH), lambda s: (s, 0))],
        out_shape=[jax.ShapeDtypeStruct((T, ATTN_WIDTH), BF),
                   jax.ShapeDtypeStruct((T, CONV_CH), BF)],
        scratch_shapes=_conv_scratch(ATT_Q),
        compiler_params=_cparams(1),
        name="prompt_attn_conv",
    )(sinks, bias, q, k, k, k, k, v, v, v, v, u, u, w_dw, b_dw.reshape(1, CONV_CH),
      g.reshape(1, CONV_CH), b.reshape(1, CONV_CH))


SAMPLE_ROWS = N_KV_HEADS * DEC_SEQ * (N_Q_HEADS // N_KV_HEADS)


N_SMALL = N_META + DEC_SEQ


def _sattn_kernel(sk_ref, q_ref, kw_ref, ks_ref, vw_ref, vs_ref, o_ref, nk_ref, nv_ref):
    q = q_ref[...]
    nt = (((2,), (2,)), ((0,), (0,)))
    s_win = lax.dot_general(q, kw_ref[...].astype(BF), nt, preferred_element_type=F32)
    s_small = lax.dot_general(q, ks_ref[...].astype(BF), nt, preferred_element_type=F32)
    per_kv = SAMPLE_ROWS // N_KV_HEADS

    def step_of_row(n_cols):
        r = lax.broadcasted_iota(jnp.int32, (1, SAMPLE_ROWS, n_cols), 1)
        return (r % per_kv) // (N_Q_HEADS // N_KV_HEADS)

    idx = lax.broadcasted_iota(jnp.int32, (1, SAMPLE_ROWS, WINDOW), 2)
    s_win = jnp.where(idx >= step_of_row(WINDOW), s_win, -jnp.inf)
    col = lax.broadcasted_iota(jnp.int32, (1, SAMPLE_ROWS, N_SMALL), 2)
    s_small = jnp.where(col - N_META <= step_of_row(N_SMALL), s_small, -jnp.inf)
    sk = sk_ref[...][None, :, :1]
    m = jnp.maximum(jnp.maximum(jnp.max(s_win, -1, keepdims=True),
                                jnp.max(s_small, -1, keepdims=True)), sk)
    p_win = jnp.exp(s_win - m)
    p_small = jnp.exp(s_small - m)
    den = (jnp.sum(p_win, -1, keepdims=True) + jnp.sum(p_small, -1, keepdims=True)
           + jnp.exp(sk - m))
    inv = 1.0 / den
    nn = (((2,), (1,)), ((0,), (0,)))
    o_ref[...] = (
        lax.dot_general((p_win * inv).astype(BF), vw_ref[...].astype(BF), nn,
                        preferred_element_type=F32)
        + lax.dot_general((p_small * inv).astype(BF), vs_ref[...].astype(BF), nn,
                          preferred_element_type=F32))
    for src_w, src_s, dst in ((kw_ref, ks_ref, nk_ref), (vw_ref, vs_ref, nv_ref)):
        dst[:, :WINDOW - DEC_SEQ, :] = src_w[:, DEC_SEQ:, :]
        dst[:, WINDOW - DEC_SEQ:, :] = src_s[:, N_META:, :]


def _sattn_call(sink_rows, qz, k_win, k_small, v_win, v_small):
    b3 = lambda i: (i, 0, 0)
    win = pl.BlockSpec((SAMPLE_BT, WINDOW, KV_WIDTH), b3)
    small = pl.BlockSpec((SAMPLE_BT, N_SMALL, KV_WIDTH), b3)
    return pl.pallas_call(
        _sattn_kernel,
        grid=(DEC_BATCH // SAMPLE_BT,),
        in_specs=[pl.BlockSpec((SAMPLE_ROWS, LANES), lambda i: (0, 0)),
                  pl.BlockSpec((SAMPLE_BT, SAMPLE_ROWS, LANES), b3),
                  win, small, win, small],
        out_specs=[pl.BlockSpec((SAMPLE_BT, SAMPLE_ROWS, LANES), b3), win, win],
        out_shape=[jax.ShapeDtypeStruct((DEC_BATCH, SAMPLE_ROWS, LANES), F32),
                   jax.ShapeDtypeStruct((DEC_BATCH, WINDOW, KV_WIDTH), F32),
                   jax.ShapeDtypeStruct((DEC_BATCH, WINDOW, KV_WIDTH), F32)],
        compiler_params=_cparams(1),
        name="sample_attn",
    )(sink_rows, qz, k_win, k_small, v_win, v_small)


def _conv_tail(y, g, b):
    return _silu(_ln_rows(y, g, b))


def _conv_rows(ext_ref, sh_ref, y_ref, w_ref, bdw_ref, g_ref, b_ref, o_ref, n_rows):
    off = CONV_HALO - CONV_STATE
    span = n_rows + CONV_HALO - SUBLANES
    for s in range(1, SUBLANES):
        sh_ref[s - 1, :span, :] = ext_ref[s:s + span, :]
    for lg in range(CONV_CH // LANES):
        lanes = slice(lg * LANES, (lg + 1) * LANES)
        for r0 in range(0, n_rows, CONV_CHUNK):
            acc = jnp.broadcast_to(bdw_ref[:, lanes], (CONV_CHUNK, LANES))
            for j in range(CONV_WIDTH):
                a, s = divmod(off + j, SUBLANES)
                src = ext_ref if s == 0 else sh_ref.at[s - 1]
                x = src[r0 + a * SUBLANES:r0 + a * SUBLANES + CONV_CHUNK, lanes]
                acc = acc + x * w_ref[j:j + 1, lanes]
            y_ref[r0:r0 + CONV_CHUNK, lanes] = acc
    for r0 in range(0, n_rows, CONV_CHUNK):
        y = y_ref[r0:r0 + CONV_CHUNK, :]
        o_ref[r0:r0 + CONV_CHUNK, :] = _conv_tail(y, g_ref[...], b_ref[...]).astype(BF)


def _conv_scratch(n_rows):
    return [pltpu.VMEM((CONV_HALO + n_rows, CONV_CH), F32),
            pltpu.VMEM((SUBLANES - 1, CONV_HALO + n_rows - SUBLANES, CONV_CH), F32),
            pltpu.VMEM((n_rows, CONV_CH), F32)]


def _mconv_kernel(cur_ref, w_ref, bdw_ref, g_ref, b_ref, o_ref, ext_ref, sh_ref, y_ref):
    ext_ref[:CONV_HALO] = jnp.zeros((CONV_HALO, CONV_CH), F32)
    ext_ref[CONV_HALO:] = cur_ref[...]
    _conv_rows(ext_ref, sh_ref, y_ref, w_ref, bdw_ref, g_ref, b_ref, o_ref, BLOCK)


def _mconv_call(u, w_dw, b_dw, g, b):
    fixed = lambda bb: (0, 0)
    return pl.pallas_call(
        _mconv_kernel,
        grid=(BATCH,),
        in_specs=[pl.BlockSpec((BLOCK, CONV_CH), lambda bb: (MET0 // BLOCK + bb, 0)),
                  pl.BlockSpec((CONV_WIDTH, CONV_CH), fixed),
                  _vec(CONV_CH), _vec(CONV_CH), _vec(CONV_CH)],
        out_specs=pl.BlockSpec((BLOCK, CONV_CH), lambda bb: (bb, 0)),
        out_shape=jax.ShapeDtypeStruct((BATCH * BLOCK, CONV_CH), BF),
        scratch_shapes=_conv_scratch(BLOCK),
        compiler_params=_cparams(1),
        name="meta_conv",
    )(u, w_dw, b_dw.reshape(1, CONV_CH), g.reshape(1, CONV_CH), b.reshape(1, CONV_CH))


def _sconv_kernel(st_ref, u_ref, w_ref, bdw_ref, g_ref, b_ref, o_ref, new_ref, y_ref):
    l = pl.program_id(0)

    def ext_row(r):
        if r < CONV_STATE:
            return st_ref[0, r]
        return u_ref[pl.ds(r - CONV_STATE, DEC_BATCH, stride=DEC_SEQ), :]

    rows = [ext_row(r) for r in range(CONV_STATE + DEC_SEQ)]
    for r in range(CONV_STATE):
        new_ref[r] = rows[r + DEC_SEQ]
    for s in range(DEC_SEQ):
        acc = jnp.broadcast_to(bdw_ref[...], (DEC_BATCH, LANES))
        for j in range(CONV_WIDTH):
            acc = acc + rows[s + j] * w_ref[j:j + 1, :]
        y_ref[l, s] = acc

    @pl.when(l == CONV_CH // LANES - 1)
    def _():
        for s in range(DEC_SEQ):
            y = jnp.concatenate([y_ref[c, s] for c in range(CONV_CH // LANES)], axis=-1)
            o_ref[s] = _conv_tail(y, g_ref[...], b_ref[...]).astype(BF)


def _sconv_call(state_t, layer, u, w_dw, b_dw, g, b):
    grp = lambda rows: pl.BlockSpec((rows, LANES), lambda l: (0, l))
    full = lambda shape: pl.BlockSpec(shape, lambda l: (0,) * len(shape))
    return pl.pallas_call(
        _sconv_kernel,
        grid=(CONV_CH // LANES,),
        in_specs=[pl.BlockSpec((1, CONV_STATE, DEC_BATCH, LANES), lambda l: (layer, 0, 0, l)),
                  grp(NS), grp(CONV_WIDTH), grp(1),
                  full((1, CONV_CH)), full((1, CONV_CH))],
        out_specs=[full((DEC_SEQ, DEC_BATCH, CONV_CH)),
                   pl.BlockSpec((CONV_STATE, DEC_BATCH, LANES), lambda l: (0, 0, l))],
        out_shape=[jax.ShapeDtypeStruct((DEC_SEQ, DEC_BATCH, CONV_CH), BF),
                   jax.ShapeDtypeStruct((CONV_STATE, DEC_BATCH, CONV_CH), F32)],
        scratch_shapes=[pltpu.VMEM((CONV_CH // LANES, DEC_SEQ, DEC_BATCH, LANES), F32)],
        compiler_params=_cparams(1),
        name="sample_conv",
    )(state_t, u, w_dw, b_dw.reshape(1, CONV_CH), g.reshape(1, CONV_CH), b.reshape(1, CONV_CH))


def _mix(a_ref, c_ref, h_ref, wo_ref, g1_ref, b1_ref):
    mix = (jnp.dot(a_ref[...], wo_ref[0, :ATTN_WIDTH, :], preferred_element_type=F32)
           + jnp.dot(c_ref[...], wo_ref[0, ATTN_WIDTH:, :], preferred_element_type=F32))
    return _ln_rows(ALPHA * h_ref[...] + mix, g1_ref[...], b1_ref[...])


def _dense_kernel(a_ref, c_ref, h_ref, wo_ref, g1_ref, b1_ref, wg_ref, wu_ref, wd_ref,
                  g2_ref, b2_ref, o_ref):
    h1 = _mix(a_ref, c_ref, h_ref, wo_ref, g1_ref, b1_ref)
    hb = h1.astype(BF)
    gate = jnp.dot(hb, wg_ref[0], preferred_element_type=F32)
    up = jnp.dot(hb, wu_ref[0], preferred_element_type=F32)
    f = jnp.dot((_silu(gate) * up).astype(BF), wd_ref[0], preferred_element_type=F32)
    o_ref[...] = _ln_rows(ALPHA * h1 + f, g2_ref[...], b2_ref[...])


def _dense_call(a, c, h, wo, layer, g1, b1, wg, wu, wd, j, g2, b2):
    row = lambda i: (i, 0)
    once = dict(pipeline_mode=pl.Buffered(1))
    return pl.pallas_call(
        _dense_kernel,
        grid=(T // TM,),
        in_specs=[pl.BlockSpec((TM, ATTN_WIDTH), row),
                  pl.BlockSpec((TM, CONV_CH), row),
                  pl.BlockSpec((TM, D_MODEL), row),
                  pl.BlockSpec((1, D_MODEL, D_MODEL), lambda i: (layer, 0, 0), **once),
                  _vec(D_MODEL), _vec(D_MODEL),
                  pl.BlockSpec((1, D_MODEL, D_FF_DENSE), lambda i: (j, 0, 0), **once),
                  pl.BlockSpec((1, D_MODEL, D_FF_DENSE), lambda i: (j, 0, 0), **once),
                  pl.BlockSpec((1, D_FF_DENSE, D_MODEL), lambda i: (j, 0, 0), **once),
                  _vec(D_MODEL), _vec(D_MODEL)],
        out_specs=pl.BlockSpec((TM, D_MODEL), row),
        out_shape=jax.ShapeDtypeStruct((T, D_MODEL), F32),
        compiler_params=_cparams(1),
        name="mix_dense_ffn",
    )(a, c, h, wo, g1.reshape(1, -1), b1.reshape(1, -1), wg, wu, wd,
      g2.reshape(1, -1), b2.reshape(1, -1))


def _router_kernel(a_ref, c_ref, h_ref, wo_ref, g1_ref, b1_ref, rwh_ref, rwl_ref, rb_ref,
                   h1_ref, comb_ref, sel_ref):
    h1 = _mix(a_ref, c_ref, h_ref, wo_ref, g1_ref, b1_ref)
    h1_ref[...] = h1
    xh = h1.astype(BF)
    xl = (h1 - xh.astype(F32)).astype(BF)
    logits = (jnp.dot(xh, rwh_ref[...], preferred_element_type=F32)
              + jnp.dot(xl, rwh_ref[...], preferred_element_type=F32)
              + jnp.dot(xh, rwl_ref[...], preferred_element_type=F32)) + rb_ref[...]
    lane = lax.broadcasted_iota(jnp.int32, (TM, LANES), 1)
    logits = jnp.where(lane < N_EXPERTS, logits, -jnp.inf)
    v1 = jnp.max(logits, -1, keepdims=True)
    i1 = jnp.min(jnp.where(logits == v1, lane, LANES), -1, keepdims=True)
    rest = jnp.where(lane == i1, -jnp.inf, logits)
    v2 = jnp.max(rest, -1, keepdims=True)
    i2 = jnp.min(jnp.where(rest == v2, lane, LANES), -1, keepdims=True)
    e2 = jnp.exp(v2 - v1)
    den = 1.0 + e2
    comb_ref[...] = jnp.where(lane == i1, 1.0 / den, jnp.where(lane == i2, e2 / den, 0.0))
    sel_ref[...] = jnp.where((lane == i1) | (lane == i2), 1.0, 0.0).astype(BF)


def _router_call(a, c, h, wo, layer, g1, b1, rwh, rwl, rb):
    row = lambda i: (i, 0)
    fixed = lambda i: (0, 0)
    return pl.pallas_call(
        _router_kernel,
        grid=(T // TM,),
        in_specs=[pl.BlockSpec((TM, ATTN_WIDTH), row),
                  pl.BlockSpec((TM, CONV_CH), row),
                  pl.BlockSpec((TM, D_MODEL), row),
                  pl.BlockSpec((1, D_MODEL, D_MODEL), lambda i: (layer, 0, 0)),
                  _vec(D_MODEL), _vec(D_MODEL),
                  pl.BlockSpec((D_MODEL, LANES), fixed),
                  pl.BlockSpec((D_MODEL, LANES), fixed),
                  _vec(LANES)],
        out_specs=[pl.BlockSpec((TM, D_MODEL), row),
                   pl.BlockSpec((TM, LANES), row),
                   pl.BlockSpec((TM, LANES), row)],
        out_shape=[jax.ShapeDtypeStruct((T, D_MODEL), F32),
                   jax.ShapeDtypeStruct((T, LANES), F32),
                   jax.ShapeDtypeStruct((T, LANES), BF)],
        compiler_params=_cparams(1),
        name="mix_router",
    )(a, c, h, wo, g1.reshape(1, -1), b1.reshape(1, -1), rwh, rwl, rb)


def _moe_kernel(x_ref, comb_ref, sel_ref, tri_ref, eg_ref, eu_ref, ed_ref, g2_ref, b2_ref,
                o_ref, xb_ref, rankr_ref, selr_ref, combr_ref, put_ref, acc_ref):
    e = pl.program_id(1)

    @pl.when(e == 0)
    def _():
        xb_ref[...] = x_ref[...].astype(BF)
        sel = sel_ref[...]
        rank = jnp.dot(tri_ref[...], sel, preferred_element_type=F32)
        rankr_ref[...] = rank.T
        selr_ref[...] = sel.astype(F32).T
        combr_ref[...] = comb_ref[...].T
        acc_ref[...] = jnp.zeros_like(acc_ref)

    sel_row = selr_ref[pl.ds(e, 1), :]
    rank_row = rankr_ref[pl.ds(e, 1), :]
    comb_row = combr_ref[pl.ds(e, 1), :]
    count = jnp.sum(sel_row).astype(jnp.int32)

    def expert_pass(first, size):
        base = first.astype(F32)
        padded = -(-size // LANES) * LANES
        slot = lax.broadcasted_iota(jnp.int32, (padded, TM_MOE), 0).astype(F32)
        take_p = jnp.where((rank_row - base == slot) & (sel_row > 0.0), 1.0, 0.0)
        take = take_p[:size]
        weight = jnp.sum(take * comb_row, -1, keepdims=True)
        xc = jnp.dot(take.astype(BF), xb_ref[...], preferred_element_type=F32).astype(BF)
        gate = jnp.dot(xc, eg_ref[0, 0], preferred_element_type=F32)
        up = jnp.dot(xc, eu_ref[0, 0], preferred_element_type=F32)
        out = jnp.dot((_silu(gate) * up).astype(BF), ed_ref[0, 0], preferred_element_type=F32)
        put_ref[:, :padded] = take_p.T.astype(BF)
        acc_ref[...] += jnp.dot(put_ref[:, :size], (out * weight).astype(BF),
                                preferred_element_type=F32)

    one_wide = (count > 2 * MOE_CHUNK) & (count <= MOE_WIDE)

    @pl.when(one_wide)
    def _():
        expert_pass(jnp.int32(0), MOE_WIDE)

    n_big = jnp.where(one_wide, 0, (count + MOE_CHUNK - 1) // (2 * MOE_CHUNK))

    def big_pass(ci, carry):
        expert_pass(ci * (2 * MOE_CHUNK), 2 * MOE_CHUNK)
        return carry

    lax.fori_loop(0, n_big, big_pass, 0)

    @pl.when((~one_wide) & (count > n_big * (2 * MOE_CHUNK)))
    def _():
        expert_pass(n_big * (2 * MOE_CHUNK), MOE_CHUNK)

    @pl.when(e == N_EXPERTS - 1)
    def _():
        o_ref[...] = _ln_rows(ALPHA * x_ref[...] + acc_ref[...], g2_ref[...], b2_ref[...])


def _moe_call(x, comb, sel, tri, eg, eu, ed, j, g2, b2, tile0, n_tiles):
    row = lambda i, e: (tile0 + i, 0)
    fixed = lambda i, e: (0, 0)
    exp = lambda i, e: (j, e, 0, 0)
    return pl.pallas_call(
        _moe_kernel,
        grid=(n_tiles, N_EXPERTS),
        in_specs=[pl.BlockSpec((TM_MOE, D_MODEL), row),
                  pl.BlockSpec((TM_MOE, LANES), row),
                  pl.BlockSpec((TM_MOE, LANES), row),
                  pl.BlockSpec((TM_MOE, TM_MOE), fixed),
                  pl.BlockSpec((1, 1, D_MODEL, D_FF_EXPERT), exp),
                  pl.BlockSpec((1, 1, D_MODEL, D_FF_EXPERT), exp),
                  pl.BlockSpec((1, 1, D_FF_EXPERT, D_MODEL), exp),
                  _vec(D_MODEL), _vec(D_MODEL)],
        out_specs=pl.BlockSpec((TM_MOE, D_MODEL), lambda i, e: (i, 0)),
        out_shape=jax.ShapeDtypeStruct((n_tiles * TM_MOE, D_MODEL), F32),
        scratch_shapes=[pltpu.VMEM((TM_MOE, D_MODEL), BF),
                        pltpu.VMEM((LANES, TM_MOE), F32),
                        pltpu.VMEM((LANES, TM_MOE), F32),
                        pltpu.VMEM((LANES, TM_MOE), F32),
                        pltpu.VMEM((TM_MOE, -(-MOE_WIDE // LANES) * LANES), BF),
                        pltpu.VMEM((TM_MOE, D_MODEL), F32)],
        compiler_params=_cparams(2),
        name="moe_ffn",
    )(x, comb, sel, tri, eg, eu, ed, g2.reshape(1, -1), b2.reshape(1, -1))


def kernel(x_prompt, x_sample, cache_meta_k, cache_meta_v, cache_k, cache_v, state_conv,
           meta_tokens, ln_in_g, ln_in_b, w_in, w_dw, b_dw, conv_ln_g, conv_ln_b, sinks, w_out,
           ln1_g, ln1_b, ln2_g, ln2_b, ffd_w_gate, ffd_w_up, ffd_w_down,
           router_w, router_b, exp_w_gate, exp_w_up, exp_w_down):
    meta_blk = jnp.concatenate([meta_tokens, jnp.zeros((BLOCK - N_META, D_MODEL), F32)], 0)
    x_tail = jnp.concatenate([meta_blk] * BATCH + [x_sample.reshape(NS, D_MODEL),
                                                   jnp.zeros((T - PAD0, D_MODEL), F32)], 0)
    cos_np, sin_np = _rope_tables()
    cos_t, sin_t = jnp.asarray(cos_np), jnp.asarray(sin_np)
    tri = jnp.asarray(np.tril(np.ones((TM_MOE, TM_MOE), np.float32), -1), dtype=BF)
    bias = jnp.asarray(_attn_bias())
    w_in_b, w_out_b = w_in.astype(BF), w_out.astype(BF)
    ffd_b = (ffd_w_gate.astype(BF), ffd_w_up.astype(BF), ffd_w_down.astype(BF))
    exp_b = (exp_w_gate.astype(BF), exp_w_up.astype(BF), exp_w_down.astype(BF))

    state_t = state_conv.transpose(0, 2, 1, 3)

    per_kv = SAMPLE_ROWS // N_KV_HEADS
    head_of_row = (jnp.arange(SAMPLE_ROWS) // per_kv) * 4 + jnp.arange(SAMPLE_ROWS) % 4
    met_rows = lambda x, b: x[MET0 + b * BLOCK:MET0 + b * BLOCK + N_META]
    kv5 = lambda rows: jnp.stack(rows).reshape(BATCH, -1, N_KV_HEADS, HEAD_DIM)

    pmk, pmv, pk, pv, pc, sk, sv, sc = [], [], [], [], [], [], [], []
    y_tok = y_tail = None
    for i in range(DEPTH):
        if i == 0:
            h, q, k, v, u = _inproj0_call(x_prompt.reshape(TOK, D_MODEL), x_tail, ln_in_g,
                                          ln_in_b, w_in_b, cos_t, sin_t)
        else:
            q, k, v, u = _inproj_call(h, w_in_b, i, cos_t, sin_t)

        a, c = _prompt_call(sinks[i], bias, q, k, v, u, w_dw[i], b_dw[i], conv_ln_g[i],
                            conv_ln_b[i])
        c_m = _mconv_call(u, w_dw[i], b_dw[i], conv_ln_g[i], conv_ln_b[i])
        pmk.append(kv5([met_rows(k, b) for b in range(BATCH)]))
        pmv.append(kv5([met_rows(v, b) for b in range(BATCH)]))
        pk.append(kv5([k[(b + 1) * SEQ - WINDOW:(b + 1) * SEQ] for b in range(BATCH)]))
        pv.append(kv5([v[(b + 1) * SEQ - WINDOW:(b + 1) * SEQ] for b in range(BATCH)]))
        pc.append(jnp.stack([u[(b + 1) * SEQ - CONV_STATE:(b + 1) * SEQ] for b in range(BATCH)]))

        k2 = k[SMP0:PAD0].reshape(DEC_BATCH, DEC_SEQ, KV_WIDTH)
        v2 = v[SMP0:PAD0].reshape(DEC_BATCH, DEC_SEQ, KV_WIDTH)
        k_small = jnp.concatenate([cache_meta_k[i].reshape(DEC_BATCH, N_META, KV_WIDTH), k2], 1)
        v_small = jnp.concatenate([cache_meta_v[i].reshape(DEC_BATCH, N_META, KV_WIDTH), v2], 1)
        q2 = q[SMP0:PAD0].reshape(DEC_BATCH, DEC_SEQ, N_KV_HEADS, 4, HEAD_DIM)
        q2 = q2.transpose(0, 2, 1, 3, 4).reshape(DEC_BATCH, N_KV_HEADS, per_kv, HEAD_DIM)
        zq = jnp.zeros_like(q2[:, 0])
        qz = jnp.concatenate([jnp.concatenate([q2[:, 0], zq], -1),
                              jnp.concatenate([zq, q2[:, 1]], -1)], 1)
        sink_rows = jnp.broadcast_to(sinks[i][head_of_row][:, None], (SAMPLE_ROWS, LANES))
        o2, nk, nv = _sattn_call(sink_rows, qz,
                                 cache_k[i].reshape(DEC_BATCH, WINDOW, KV_WIDTH), k_small,
                                 cache_v[i].reshape(DEC_BATCH, WINDOW, KV_WIDTH), v_small)
        o2 = jnp.stack([o2[:, :per_kv, :HEAD_DIM], o2[:, per_kv:, HEAD_DIM:]], 1)
        a_s = o2.reshape(DEC_BATCH, N_KV_HEADS, DEC_SEQ, 4, HEAD_DIM).transpose(0, 2, 1, 3, 4)
        a_s = a_s.reshape(NS, ATTN_WIDTH).astype(BF)
        u2 = u[SMP0:PAD0]
        c_s, sc_i = _sconv_call(state_t, i, u2, w_dw[i], b_dw[i], conv_ln_g[i], conv_ln_b[i])
        c_s = c_s.transpose(1, 0, 2).reshape(NS, CONV_CH)
        sk.append(nk.reshape(DEC_BATCH, WINDOW, N_KV_HEADS, HEAD_DIM))
        sv.append(nv.reshape(DEC_BATCH, WINDOW, N_KV_HEADS, HEAD_DIM))
        sc.append(sc_i)

        a = lax.dynamic_update_slice(a, a_s, (SMP0, 0))
        c = lax.dynamic_update_slice(c, jnp.concatenate([c_m, c_s], 0), (MET0, 0))
        j = i // 2
        if i % 2 == 0:
            h = _dense_call(a, c, h, w_out_b, i, ln1_g[i], ln1_b[i], *ffd_b, j,
                            ln2_g[i], ln2_b[i])
        else:
            rw = jnp.pad(router_w[j], ((0, 0), (0, LANES - N_EXPERTS)))
            rwh = rw.astype(BF)
            rwl = (rw - rwh.astype(F32)).astype(BF)
            rb = jnp.pad(router_b[j], (0, LANES - N_EXPERTS)).reshape(1, LANES)
            h1, comb, sel = _router_call(a, c, h, w_out_b, i, ln1_g[i], ln1_b[i], rwh, rwl, rb)
            moe = functools.partial(_moe_call, h1, comb, sel, tri, *exp_b, j,
                                    ln2_g[i], ln2_b[i])
            if i < DEPTH - 1:
                h = moe(0, T // TM_MOE)
            else:
                y_tok = moe(0, TOK // TM_MOE)
                y_tail = moe(TOK // TM_MOE, TAIL // TM_MOE)

    y_prompt = y_tok.reshape(BATCH, SEQ, D_MODEL)
    y_sample = y_tail[SMP0 - TOK:PAD0 - TOK].reshape(DEC_BATCH, DEC_SEQ, D_MODEL)
    return (y_prompt, y_sample, jnp.stack(pmk), jnp.stack(pmv), jnp.stack(pk), jnp.stack(pv),
            jnp.stack(pc), jnp.stack(sk), jnp.stack(sv), jnp.stack(sc).transpose(0, 2, 1, 3))
```

```python
import functools

import numpy as np
import jax
import jax.numpy as jnp
from jax import lax
from jax.experimental import pallas as pl
from jax.experimental.pallas import tpu as pltpu

D_MODEL = 1024
BATCH = 2
SEQ = 8192
DEPTH = 4
DEC_BATCH = 128
DEC_SEQ = 4
PAST_LEN = 8192
N_META = 16
HEAD_DIM = 64
N_Q_HEADS = 8
N_KV_HEADS = 2
ATTN_WIDTH = N_Q_HEADS * HEAD_DIM
KV_WIDTH = N_KV_HEADS * HEAD_DIM
CONV_CH = D_MODEL - ATTN_WIDTH
Q_END = ATTN_WIDTH
K_END = Q_END + KV_WIDTH
V_END = K_END + KV_WIDTH
IN_COLS = V_END + 2 * CONV_CH
CONV_WIDTH = 31
CONV_STATE = CONV_WIDTH - 1
WINDOW = 128
BLOCK = 128
ROPE_THETA = 10000.0
ATTN_SCALE = HEAD_DIM ** -0.5
D_FF_DENSE = 2816
N_EXPERTS = 8
D_FF_EXPERT = 1024
ALPHA = (2 * DEPTH) ** 0.25
LN_EPS = 1e-5

LANES = 128
SUBLANES = 8
TOK = BATCH * SEQ
MET0 = TOK
SMP0 = MET0 + BATCH * BLOCK
NS = DEC_BATCH * DEC_SEQ
PAD0 = SMP0 + NS
TM = 512
TM_MOE = 1024
T = -(-PAD0 // TM_MOE) * TM_MOE
TAIL = T - TOK
ATT_NB = 2
ATT_Q = ATT_NB * BLOCK
ATT_STEPS = SEQ // ATT_Q
N_KEYS = 2 * BLOCK + N_META
CONV_HALO = 32
CONV_CHUNK = 64
SAMPLE_BT = 16
MOE_CHUNK = 128
MOE_WIDE = 320
VMEM_LIMIT = 56 * 1024 * 1024

BF = jnp.bfloat16
F32 = jnp.float32


def _cparams(n_axes):
    return pltpu.CompilerParams(dimension_semantics=("arbitrary",) * n_axes,
                                vmem_limit_bytes=VMEM_LIMIT)


def _ln_rows(x, g, b):
    mu = jnp.mean(x, -1, keepdims=True)
    xc = x - mu
    var = jnp.mean(xc * xc, -1, keepdims=True)
    return xc * lax.rsqrt(var + LN_EPS) * g + b


def _silu(x):
    return x * jax.nn.sigmoid(x)


def _vec(n):
    return pl.BlockSpec((1, n), lambda *_: (0, 0))


def _ln_kernel(xt_ref, xr_ref, g_ref, b_ref, o_ref):
    i = pl.program_id(0)

    @pl.when(i < TOK // TM)
    def _():
        o_ref[...] = _ln_rows(xt_ref[...], g_ref[...], b_ref[...])

    @pl.when(i >= TOK // TM)
    def _():
        o_ref[...] = _ln_rows(xr_ref[...], g_ref[...], b_ref[...])


def _project(h_ref, w_ref, cos_ref, sin_ref, q_ref, k_ref, v_ref, u_ref):
    hb = h_ref[...].astype(BF)
    cos = cos_ref[...]
    sin = sin_ref[...]
    lane = lax.broadcasted_iota(jnp.int32, (TM, LANES), 1)
    first_half = (lane % HEAD_DIM) < (HEAD_DIM // 2)

    def rope(x):
        rot = jnp.where(first_half, pltpu.roll(x, LANES - HEAD_DIM // 2, 1),
                        pltpu.roll(x, HEAD_DIM // 2, 1))
        return x * cos + rot * sin

    qkv = jnp.dot(hb, w_ref[0, :, :V_END], preferred_element_type=F32)
    for c in range(Q_END // LANES):
        q_ref[:, c * LANES:(c + 1) * LANES] = (
            rope(qkv[:, c * LANES:(c + 1) * LANES]) * ATTN_SCALE).astype(BF)
    k_ref[...] = rope(qkv[:, Q_END:K_END])
    v_ref[...] = qkv[:, K_END:V_END]
    val = jnp.dot(hb, w_ref[0, :, V_END:V_END + CONV_CH], preferred_element_type=F32)
    gate = jnp.dot(hb, w_ref[0, :, V_END + CONV_CH:], preferred_element_type=F32)
    u_ref[...] = val * jax.nn.sigmoid(gate)


def _inproj_kernel(h_ref, w_ref, cos_ref, sin_ref, q_ref, k_ref, v_ref, u_ref):
    _project(h_ref, w_ref, cos_ref, sin_ref, q_ref, k_ref, v_ref, u_ref)


def _inproj0_kernel(xt_ref, xr_ref, g_ref, b_ref, w_ref, cos_ref, sin_ref,
                    h_ref, q_ref, k_ref, v_ref, u_ref):
    _ln_kernel(xt_ref, xr_ref, g_ref, b_ref, h_ref)
    _project(h_ref, w_ref, cos_ref, sin_ref, q_ref, k_ref, v_ref, u_ref)


def _rope_tables():
    half = HEAD_DIM // 2
    pos = np.zeros((SEQ + TAIL,), np.float64)
    pos[:SEQ] = N_META + np.arange(SEQ)
    for b in range(BATCH):
        pos[SEQ + b * BLOCK:SEQ + b * BLOCK + N_META] = np.arange(N_META)
    s0 = SEQ + SMP0 - MET0
    pos[s0:s0 + NS] = PAST_LEN + np.tile(np.arange(DEC_SEQ), DEC_BATCH)
    inv = ROPE_THETA ** (-np.arange(half, dtype=np.float64) / half)
    ang = pos[:, None] * inv[None, :]
    cos = np.cos(ang).astype(np.float32)
    sin = np.sin(ang).astype(np.float32)
    return np.tile(cos, (1, 4)), np.tile(np.concatenate([-sin, sin], -1), (1, 2))


def _inproj_specs(layer):
    row = lambda i: (i, 0)
    per_batch = SEQ // TM
    tab = lambda i: (jnp.where(i < 2 * per_batch, i % per_batch, i - per_batch), 0)
    in_specs = [pl.BlockSpec((1, D_MODEL, IN_COLS), lambda i: (layer, 0, 0)),
                pl.BlockSpec((TM, LANES), tab),
                pl.BlockSpec((TM, LANES), tab)]
    out_specs = [pl.BlockSpec((TM, ATTN_WIDTH), row),
                 pl.BlockSpec((TM, KV_WIDTH), row),
                 pl.BlockSpec((TM, KV_WIDTH), row),
                 pl.BlockSpec((TM, CONV_CH), row)]
    out_shape = [jax.ShapeDtypeStruct((T, ATTN_WIDTH), BF),
                 jax.ShapeDtypeStruct((T, KV_WIDTH), F32),
                 jax.ShapeDtypeStruct((T, KV_WIDTH), F32),
                 jax.ShapeDtypeStruct((T, CONV_CH), F32)]
    return in_specs, out_specs, out_shape


def _inproj_call(h, w_bf, layer, cos_t, sin_t):
    in_specs, out_specs, out_shape = _inproj_specs(layer)
    return pl.pallas_call(
        _inproj_kernel,
        grid=(T // TM,),
        in_specs=[pl.BlockSpec((TM, D_MODEL), lambda i: (i, 0))] + in_specs,
        out_specs=out_specs,
        out_shape=out_shape,
        compiler_params=_cparams(1),
        name="inproj",
    )(h, w_bf, cos_t, sin_t)


def _inproj0_call(x_tok, x_tail, g, b, w_bf, cos_t, sin_t):
    in_specs, out_specs, out_shape = _inproj_specs(0)
    n_tok = TOK // TM
    row = pl.BlockSpec((TM, D_MODEL), lambda i: (i, 0))
    return pl.pallas_call(
        _inproj0_kernel,
        grid=(T // TM,),
        in_specs=[pl.BlockSpec((TM, D_MODEL), lambda i: (jnp.minimum(i, n_tok - 1), 0)),
                  pl.BlockSpec((TM, D_MODEL), lambda i: (jnp.maximum(i - n_tok, 0), 0)),
                  _vec(D_MODEL), _vec(D_MODEL)] + in_specs,
        out_specs=[row] + out_specs,
        out_shape=[jax.ShapeDtypeStruct((T, D_MODEL), F32)] + out_shape,
        compiler_params=_cparams(1),
        name="ln_inproj",
    )(x_tok, x_tail, g.reshape(1, D_MODEL), b.reshape(1, D_MODEL), w_bf, cos_t, sin_t)


def _head_variants(x):
    lane = lax.broadcasted_iota(jnp.int32, x.shape, x.ndim - 1)
    lo = lane < HEAD_DIM
    xr = pltpu.roll(x, HEAD_DIM, x.ndim - 1)
    zero = jnp.zeros_like(x)
    return [[jnp.where(lo, x, zero).astype(BF), jnp.where(lo, zero, xr).astype(BF)],
            [jnp.where(lo, xr, zero).astype(BF), jnp.where(lo, zero, x).astype(BF)]]


def _attn_bias():
    qi = np.arange(2 * BLOCK)[:, None] % BLOCK
    col = np.arange(N_KEYS)[None, :]
    d = BLOCK + qi - col
    band = (col < 2 * BLOCK) & (d >= 0) & (d <= WINDOW)
    is_meta = col >= 2 * BLOCK
    normal = band | is_meta
    first = (band & (col >= BLOCK)) | is_meta
    meta_q = is_meta & ((col - 2 * BLOCK) <= qi)
    return np.where(np.stack([normal, first, meta_q]), 0.0, -np.inf).astype(np.float32)


def _attend(sinks_ref, bias_ref, variant, q_ref, kp_ref, kc_ref, km_refs, vp_ref, vc_ref,
            vm_refs, o_ref):
    upper = lax.broadcasted_iota(jnp.int32, (2 * BLOCK, 1), 0) < BLOCK

    def keys_of(prev_ref, cur_ref, meta_refs, h):
        prev = prev_ref[...] if h == 0 else cur_ref[(h - 1) * BLOCK:h * BLOCK, :]
        cur = cur_ref[h * BLOCK:(h + 1) * BLOCK, :]
        return jnp.concatenate([prev, cur, meta_refs[min(h, 1)][...]], axis=0)

    for h in range(ATT_NB):
        rows = slice(h * BLOCK, (h + 1) * BLOCK)
        kvar = _head_variants(keys_of(kp_ref, kc_ref, km_refs, h))
        vvar = _head_variants(keys_of(vp_ref, vc_ref, vm_refs, h))
        bias = bias_ref[variant[h]]
        for g in range(N_KV_HEADS):
            qg = jnp.concatenate([q_ref[rows, (2 * g) * LANES:(2 * g + 1) * LANES],
                                  q_ref[rows, (2 * g + 1) * LANES:(2 * g + 2) * LANES]],
                                 axis=0)
            acc = jnp.zeros((2 * BLOCK, LANES), F32)
            for e in range(2):
                s = lax.dot_general(qg, kvar[g][e], (((1,), (1,)), ((), ())),
                                    preferred_element_type=F32) + bias
                sk = jnp.where(upper, sinks_ref[4 * g + e], sinks_ref[4 * g + 2 + e])
                m = jnp.maximum(jnp.max(s, -1, keepdims=True), sk)
                p = jnp.exp(s - m)
                den = jnp.sum(p, -1, keepdims=True) + jnp.exp(sk - m)
                pn = (p * (1.0 / den)).astype(BF)
                acc = acc + jnp.dot(pn, vvar[g][e], preferred_element_type=F32)
            o_ref[rows, (2 * g) * LANES:(2 * g + 1) * LANES] = acc[:BLOCK].astype(BF)
            o_ref[rows, (2 * g + 1) * LANES:(2 * g + 2) * LANES] = acc[BLOCK:].astype(BF)


def _prompt_kernel(sinks_ref, bias_ref, q_ref, kp_ref, kc_ref, km0_ref, km1_ref,
                   vp_ref, vc_ref, vm0_ref, vm1_ref, up_ref, uc_ref, w_ref, bdw_ref, g_ref,
                   b_ref, a_ref, c_ref, ext_ref, sh_ref, y_ref):
    s_id = pl.program_id(0)
    n_tok = BATCH * ATT_STEPS
    attn_refs = (q_ref, kp_ref, kc_ref, (km0_ref, km1_ref), vp_ref, vc_ref, (vm0_ref, vm1_ref),
                 a_ref)

    @pl.when(s_id > n_tok)
    def _():
        a_ref[...] = jnp.zeros_like(a_ref)
        c_ref[...] = jnp.zeros_like(c_ref)

    @pl.when(s_id == n_tok)
    def _():
        _attend(sinks_ref, bias_ref, (2,) * ATT_NB, *attn_refs)
        c_ref[...] = jnp.zeros_like(c_ref)

    @pl.when(s_id < n_tok)
    def _():
        first = (s_id % ATT_STEPS) == 0
        _attend(sinks_ref, bias_ref, (jnp.where(first, 1, 0),) + (0,) * (ATT_NB - 1),
                *attn_refs)
        half = CONV_HALO // 2
        ext_ref[:half] = jnp.where(first, 0.0, up_ref[:half])
        ext_ref[half:CONV_HALO] = jnp.where(first, up_ref[:half], up_ref[half:])
        ext_ref[CONV_HALO:] = uc_ref[...]
        _conv_rows(ext_ref, sh_ref, y_ref, w_ref, bdw_ref, g_ref, b_ref, c_ref, ATT_Q)


def _prompt_call(sinks, bias, q, k, v, u, w_dw, b_dw, g, b):
    n_tok = BATCH * ATT_STEPS
    cur = lambda s: (jnp.minimum(s, n_tok), 0)
    fixed = lambda s: (0, 0)

    def prev(s):
        blk = jnp.where(s % ATT_STEPS == 0, ATT_NB * s, ATT_NB * s - 1)
        return (jnp.minimum(blk, ATT_NB * n_tok), 0)

    def meta(h):
        def index(s):
            b = jnp.where(s < n_tok, s // ATT_STEPS, jnp.where(s == n_tok, h, 0))
            return ((MET0 + b * BLOCK) // N_META, 0)
        return index

    def u_prev(s):
        first = (MET0 + (s // ATT_STEPS) * BLOCK) // CONV_HALO
        tok = s * (ATT_Q // CONV_HALO) - 1
        return (jnp.where(s < n_tok, jnp.where(s % ATT_STEPS == 0, first, tok), 0), 0)

    return pl.pallas_call(
        _prompt_kernel,
        grid=(T // ATT_Q,),
        in_specs=[pl.BlockSpec(memory_space=pltpu.SMEM),
                  pl.BlockSpec((3, 2 * BLOCK, N_KEYS), lambda s: (0, 0, 0)),
                  pl.BlockSpec((ATT_Q, ATTN_WIDTH), cur),
                  pl.BlockSpec((BLOCK, KV_WIDTH), prev),
                  pl.BlockSpec((ATT_Q, KV_WIDTH), cur),
                  pl.BlockSpec((N_META, KV_WIDTH), meta(0)),
                  pl.BlockSpec((N_META, KV_WIDTH), meta(1)),
                  pl.BlockSpec((BLOCK, KV_WIDTH), prev),
                  pl.BlockSpec((ATT_Q, KV_WIDTH), cur),
                  pl.BlockSpec((N_META, KV_WIDTH), meta(0)),
                  pl.BlockSpec((N_META, KV_WIDTH), meta(1)),
                  pl.BlockSpec((CONV_HALO, CONV_CH), u_prev),
                  pl.BlockSpec((ATT_Q, CONV_CH), cur),
                  pl.BlockSpec((CONV_WIDTH, CONV_CH), fixed),
                  _vec(CONV_CH), _vec(CONV_CH), _vec(CONV_CH)],
        out_specs=[pl.BlockSpec((ATT_Q, ATTN_WIDTH), lambda s: (s, 0)),
                   pl.BlockSpec((ATT_Q, CONV_CH), lambda s: (s, 0))],
        out_shape=[jax.ShapeDtypeStruct((T, ATTN_WIDTH), BF),
                   jax.ShapeDtypeStruct((T, CONV_CH), BF)],
        scratch_shapes=_conv_scratch(ATT_Q),
        compiler_params=_cparams(1),
        name="prompt_attn_conv",
    )(sinks, bias, q, k, k, k, k, v, v, v, v, u, u, w_dw, b_dw.reshape(1, CONV_CH),
      g.reshape(1, CONV_CH), b.reshape(1, CONV_CH))


SAMPLE_ROWS = N_KV_HEADS * DEC_SEQ * (N_Q_HEADS // N_KV_HEADS)


N_SMALL = N_META + DEC_SEQ


def _sattn_kernel(sk_ref, q_ref, kw_ref, ks_ref, vw_ref, vs_ref, o_ref, nk_ref, nv_ref):
    q = q_ref[...]
    nt = (((2,), (2,)), ((0,), (0,)))
    s_win = lax.dot_general(q, kw_ref[...].astype(BF), nt, preferred_element_type=F32)
    s_small = lax.dot_general(q, ks_ref[...].astype(BF), nt, preferred_element_type=F32)
    per_kv = SAMPLE_ROWS // N_KV_HEADS

    def step_of_row(n_cols):
        r = lax.broadcasted_iota(jnp.int32, (1, SAMPLE_ROWS, n_cols), 1)
        return (r % per_kv) // (N_Q_HEADS // N_KV_HEADS)

    idx = lax.broadcasted_iota(jnp.int32, (1, SAMPLE_ROWS, WINDOW), 2)
    s_win = jnp.where(idx >= step_of_row(WINDOW), s_win, -jnp.inf)
    col = lax.broadcasted_iota(jnp.int32, (1, SAMPLE_ROWS, N_SMALL), 2)
    s_small = jnp.where(col - N_META <= step_of_row(N_SMALL), s_small, -jnp.inf)
    sk = sk_ref[...][None, :, :1]
    m = jnp.maximum(jnp.maximum(jnp.max(s_win, -1, keepdims=True),
                                jnp.max(s_small, -1, keepdims=True)), sk)
    p_win = jnp.exp(s_win - m)
    p_small = jnp.exp(s_small - m)
    den = (jnp.sum(p_win, -1, keepdims=True) + jnp.sum(p_small, -1, keepdims=True)
           + jnp.exp(sk - m))
    inv = 1.0 / den
    nn = (((2,), (1,)), ((0,), (0,)))
    o_ref[...] = (
        lax.dot_general((p_win * inv).astype(BF), vw_ref[...].astype(BF), nn,
                        preferred_element_type=F32)
        + lax.dot_general((p_small * inv).astype(BF), vs_ref[...].astype(BF), nn,
                          preferred_element_type=F32))
    for src_w, src_s, dst in ((kw_ref, ks_ref, nk_ref), (vw_ref, vs_ref, nv_ref)):
        dst[:, :WINDOW - DEC_SEQ, :] = src_w[:, DEC_SEQ:, :]
        dst[:, WINDOW - DEC_SEQ:, :] = src_s[:, N_META:, :]


def _sattn_call(sink_rows, qz, k_win, k_small, v_win, v_small):
    b3 = lambda i: (i, 0, 0)
    win = pl.BlockSpec((SAMPLE_BT, WINDOW, KV_WIDTH), b3)
    small = pl.BlockSpec((SAMPLE_BT, N_SMALL, KV_WIDTH), b3)
    return pl.pallas_call(
        _sattn_kernel,
        grid=(DEC_BATCH // SAMPLE_BT,),
        in_specs=[pl.BlockSpec((SAMPLE_ROWS, LANES), lambda i: (0, 0)),
                  pl.BlockSpec((SAMPLE_BT, SAMPLE_ROWS, LANES), b3),
                  win, small, win, small],
        out_specs=[pl.BlockSpec((SAMPLE_BT, SAMPLE_ROWS, LANES), b3), win, win],
        out_shape=[jax.ShapeDtypeStruct((DEC_BATCH, SAMPLE_ROWS, LANES), F32),
                   jax.ShapeDtypeStruct((DEC_BATCH, WINDOW, KV_WIDTH), F32),
                   jax.ShapeDtypeStruct((DEC_BATCH, WINDOW, KV_WIDTH), F32)],
        compiler_params=_cparams(1),
        name="sample_attn",
    )(sink_rows, qz, k_win, k_small, v_win, v_small)


def _conv_tail(y, g, b):
    return _silu(_ln_rows(y, g, b))


def _conv_rows(ext_ref, sh_ref, y_ref, w_ref, bdw_ref, g_ref, b_ref, o_ref, n_rows):
    off = CONV_HALO - CONV_STATE
    span = n_rows + CONV_HALO - SUBLANES
    for s in range(1, SUBLANES):
        sh_ref[s - 1, :span, :] = ext_ref[s:s + span, :]
    for lg in range(CONV_CH // LANES):
        lanes = slice(lg * LANES, (lg + 1) * LANES)
        for r0 in range(0, n_rows, CONV_CHUNK):
            acc = jnp.broadcast_to(bdw_ref[:, lanes], (CONV_CHUNK, LANES))
            for j in range(CONV_WIDTH):
                a, s = divmod(off + j, SUBLANES)
                src = ext_ref if s == 0 else sh_ref.at[s - 1]
                x = src[r0 + a * SUBLANES:r0 + a * SUBLANES + CONV_CHUNK, lanes]
                acc = acc + x * w_ref[j:j + 1, lanes]
            y_ref[r0:r0 + CONV_CHUNK, lanes] = acc
    for r0 in range(0, n_rows, CONV_CHUNK):
        y = y_ref[r0:r0 + CONV_CHUNK, :]
        o_ref[r0:r0 + CONV_CHUNK, :] = _conv_tail(y, g_ref[...], b_ref[...]).astype(BF)


def _conv_scratch(n_rows):
    return [pltpu.VMEM((CONV_HALO + n_rows, CONV_CH), F32),
            pltpu.VMEM((SUBLANES - 1, CONV_HALO + n_rows - SUBLANES, CONV_CH), F32),
            pltpu.VMEM((n_rows, CONV_CH), F32)]


def _mconv_kernel(cur_ref, w_ref, bdw_ref, g_ref, b_ref, o_ref, ext_ref, sh_ref, y_ref):
    ext_ref[:CONV_HALO] = jnp.zeros((CONV_HALO, CONV_CH), F32)
    ext_ref[CONV_HALO:] = cur_ref[...]
    _conv_rows(ext_ref, sh_ref, y_ref, w_ref, bdw_ref, g_ref, b_ref, o_ref, BLOCK)


def _mconv_call(u, w_dw, b_dw, g, b):
    fixed = lambda bb: (0, 0)
    return pl.pallas_call(
        _mconv_kernel,
        grid=(BATCH,),
        in_specs=[pl.BlockSpec((BLOCK, CONV_CH), lambda bb: (MET0 // BLOCK + bb, 0)),
                  pl.BlockSpec((CONV_WIDTH, CONV_CH), fixed),
                  _vec(CONV_CH), _vec(CONV_CH), _vec(CONV_CH)],
        out_specs=pl.BlockSpec((BLOCK, CONV_CH), lambda bb: (bb, 0)),
        out_shape=jax.ShapeDtypeStruct((BATCH * BLOCK, CONV_CH), BF),
        scratch_shapes=_conv_scratch(BLOCK),
        compiler_params=_cparams(1),
        name="meta_conv",
    )(u, w_dw, b_dw.reshape(1, CONV_CH), g.reshape(1, CONV_CH), b.reshape(1, CONV_CH))


def _sconv_kernel(st_ref, u_ref, w_ref, bdw_ref, g_ref, b_ref, o_ref, new_ref, y_ref):
    l = pl.program_id(0)

    def ext_row(r):
        if r < CONV_STATE:
            return st_ref[0, r]
        return u_ref[pl.ds(r - CONV_STATE, DEC_BATCH, stride=DEC_SEQ), :]

    rows = [ext_row(r) for r in range(CONV_STATE + DEC_SEQ)]
    for r in range(CONV_STATE):
        new_ref[r] = rows[r + DEC_SEQ]
    for s in range(DEC_SEQ):
        acc = jnp.broadcast_to(bdw_ref[...], (DEC_BATCH, LANES))
        for j in range(CONV_WIDTH):
            acc = acc + rows[s + j] * w_ref[j:j + 1, :]
        y_ref[l, s] = acc

    @pl.when(l == CONV_CH // LANES - 1)
    def _():
        for s in range(DEC_SEQ):
            y = jnp.concatenate([y_ref[c, s] for c in range(CONV_CH // LANES)], axis=-1)
            o_ref[s] = _conv_tail(y, g_ref[...], b_ref[...]).astype(BF)


def _sconv_call(state_t, layer, u, w_dw, b_dw, g, b):
    grp = lambda rows: pl.BlockSpec((rows, LANES), lambda l: (0, l))
    full = lambda shape: pl.BlockSpec(shape, lambda l: (0,) * len(shape))
    return pl.pallas_call(
        _sconv_kernel,
        grid=(CONV_CH // LANES,),
        in_specs=[pl.BlockSpec((1, CONV_STATE, DEC_BATCH, LANES), lambda l: (layer, 0, 0, l)),
                  grp(NS), grp(CONV_WIDTH), grp(1),
                  full((1, CONV_CH)), full((1, CONV_CH))],
        out_specs=[full((DEC_SEQ, DEC_BATCH, CONV_CH)),
                   pl.BlockSpec((CONV_STATE, DEC_BATCH, LANES), lambda l: (0, 0, l))],
        out_shape=[jax.ShapeDtypeStruct((DEC_SEQ, DEC_BATCH, CONV_CH), BF),
                   jax.ShapeDtypeStruct((CONV_STATE, DEC_BATCH, CONV_CH), F32)],
        scratch_shapes=[pltpu.VMEM((CONV_CH // LANES, DEC_SEQ, DEC_BATCH, LANES), F32)],
        compiler_params=_cparams(1),
        name="sample_conv",
    )(state_t, u, w_dw, b_dw.reshape(1, CONV_CH), g.reshape(1, CONV_CH), b.reshape(1, CONV_CH))


def _mix(a_ref, c_ref, h_ref, wo_ref, g1_ref, b1_ref):
    mix = (jnp.dot(a_ref[...], wo_ref[0, :ATTN_WIDTH, :], preferred_element_type=F32)
           + jnp.dot(c_ref[...], wo_ref[0, ATTN_WIDTH:, :], preferred_element_type=F32))
    return _ln_rows(ALPHA * h_ref[...] + mix, g1_ref[...], b1_ref[...])


def _dense_kernel(a_ref, c_ref, h_ref, wo_ref, g1_ref, b1_ref, wg_ref, wu_ref, wd_ref,
                  g2_ref, b2_ref, o_ref):
    h1 = _mix(a_ref, c_ref, h_ref, wo_ref, g1_ref, b1_ref)
    hb = h1.astype(BF)
    gate = jnp.dot(hb, wg_ref[0], preferred_element_type=F32)
    up = jnp.dot(hb, wu_ref[0], preferred_element_type=F32)
    f = jnp.dot((_silu(gate) * up).astype(BF), wd_ref[0], preferred_element_type=F32)
    o_ref[...] = _ln_rows(ALPHA * h1 + f, g2_ref[...], b2_ref[...])


def _dense_call(a, c, h, wo, layer, g1, b1, wg, wu, wd, j, g2, b2):
    row = lambda i: (i, 0)
    once = dict(pipeline_mode=pl.Buffered(1))
    return pl.pallas_call(
        _dense_kernel,
        grid=(T // TM,),
        in_specs=[pl.BlockSpec((TM, ATTN_WIDTH), row),
                  pl.BlockSpec((TM, CONV_CH), row),
                  pl.BlockSpec((TM, D_MODEL), row),
                  pl.BlockSpec((1, D_MODEL, D_MODEL), lambda i: (layer, 0, 0), **once),
                  _vec(D_MODEL), _vec(D_MODEL),
                  pl.BlockSpec((1, D_MODEL, D_FF_DENSE), lambda i: (j, 0, 0), **once),
                  pl.BlockSpec((1, D_MODEL, D_FF_DENSE), lambda i: (j, 0, 0), **once),
                  pl.BlockSpec((1, D_FF_DENSE, D_MODEL), lambda i: (j, 0, 0), **once),
                  _vec(D_MODEL), _vec(D_MODEL)],
        out_specs=pl.BlockSpec((TM, D_MODEL), row),
        out_shape=jax.ShapeDtypeStruct((T, D_MODEL), F32),
        compiler_params=_cparams(1),
        name="mix_dense_ffn",
    )(a, c, h, wo, g1.reshape(1, -1), b1.reshape(1, -1), wg, wu, wd,
      g2.reshape(1, -1), b2.reshape(1, -1))


def _router_kernel(a_ref, c_ref, h_ref, wo_ref, g1_ref, b1_ref, rwh_ref, rwl_ref, rb_ref,
                   h1_ref, comb_ref, sel_ref):
    h1 = _mix(a_ref, c_ref, h_ref, wo_ref, g1_ref, b1_ref)
    h1_ref[...] = h1
    xh = h1.astype(BF)
    xl = (h1 - xh.astype(F32)).astype(BF)
    logits = (jnp.dot(xh, rwh_ref[...], preferred_element_type=F32)
              + jnp.dot(xl, rwh_ref[...], preferred_element_type=F32)
              + jnp.dot(xh, rwl_ref[...], preferred_element_type=F32)) + rb_ref[...]
    lane = lax.broadcasted_iota(jnp.int32, (TM, LANES), 1)
    logits = jnp.where(lane < N_EXPERTS, logits, -jnp.inf)
    v1 = jnp.max(logits, -1, keepdims=True)
    i1 = jnp.min(jnp.where(logits == v1, lane, LANES), -1, keepdims=True)
    rest = jnp.where(lane == i1, -jnp.inf, logits)
    v2 = jnp.max(rest, -1, keepdims=True)
    i2 = jnp.min(jnp.where(rest == v2, lane, LANES), -1, keepdims=True)
    e2 = jnp.exp(v2 - v1)
    den = 1.0 + e2
    comb_ref[...] = jnp.where(lane == i1, 1.0 / den, jnp.where(lane == i2, e2 / den, 0.0))
    sel_ref[...] = jnp.where((lane == i1) | (lane == i2), 1.0, 0.0).astype(BF)


def _router_call(a, c, h, wo, layer, g1, b1, rwh, rwl, rb):
    row = lambda i: (i, 0)
    fixed = lambda i: (0, 0)
    return pl.pallas_call(
        _router_kernel,
        grid=(T // TM,),
        in_specs=[pl.BlockSpec((TM, ATTN_WIDTH), row),
                  pl.BlockSpec((TM, CONV_CH), row),
                  pl.BlockSpec((TM, D_MODEL), row),
                  pl.BlockSpec((1, D_MODEL, D_MODEL), lambda i: (layer, 0, 0)),
                  _vec(D_MODEL), _vec(D_MODEL),
                  pl.BlockSpec((D_MODEL, LANES), fixed),
                  pl.BlockSpec((D_MODEL, LANES), fixed),
                  _vec(LANES)],
        out_specs=[pl.BlockSpec((TM, D_MODEL), row),
                   pl.BlockSpec((TM, LANES), row),
                   pl.BlockSpec((TM, LANES), row)],
        out_shape=[jax.ShapeDtypeStruct((T, D_MODEL), F32),
                   jax.ShapeDtypeStruct((T, LANES), F32),
                   jax.ShapeDtypeStruct((T, LANES), BF)],
        compiler_params=_cparams(1),
        name="mix_router",
    )(a, c, h, wo, g1.reshape(1, -1), b1.reshape(1, -1), rwh, rwl, rb)


def _moe_kernel(x_ref, comb_ref, sel_ref, tri_ref, eg_ref, eu_ref, ed_ref, g2_ref, b2_ref,
                o_ref, xb_ref, rankr_ref, selr_ref, combr_ref, put_ref, acc_ref):
    e = pl.program_id(1)

    @pl.when(e == 0)
    def _():
        xb_ref[...] = x_ref[...].astype(BF)
        sel = sel_ref[...]
        rank = jnp.dot(tri_ref[...], sel, preferred_element_type=F32)
        rankr_ref[...] = rank.T
        selr_ref[...] = sel.astype(F32).T
        combr_ref[...] = comb_ref[...].T
        acc_ref[...] = jnp.zeros_like(acc_ref)

    sel_row = selr_ref[pl.ds(e, 1), :]
    rank_row = rankr_ref[pl.ds(e, 1), :]
    comb_row = combr_ref[pl.ds(e, 1), :]
    count = jnp.sum(sel_row).astype(jnp.int32)

    def expert_pass(first, size):
        base = first.astype(F32)
        padded = -(-size // LANES) * LANES
        slot = lax.broadcasted_iota(jnp.int32, (padded, TM_MOE), 0).astype(F32)
        take_p = jnp.where((rank_row - base == slot) & (sel_row > 0.0), 1.0, 0.0)
        take = take_p[:size]
        weight = jnp.sum(take * comb_row, -1, keepdims=True)
        xc = jnp.dot(take.astype(BF), xb_ref[...], preferred_element_type=F32).astype(BF)
        gate = jnp.dot(xc, eg_ref[0, 0], preferred_element_type=F32)
        up = jnp.dot(xc, eu_ref[0, 0], preferred_element_type=F32)
        out = jnp.dot((_silu(gate) * up).astype(BF), ed_ref[0, 0], preferred_element_type=F32)
        put_ref[:, :padded] = take_p.T.astype(BF)
        acc_ref[...] += jnp.dot(put_ref[:, :size], (out * weight).astype(BF),
                                preferred_element_type=F32)

    one_wide = (count > 2 * MOE_CHUNK) & (count <= MOE_WIDE)

    @pl.when(one_wide)
    def _():
        expert_pass(jnp.int32(0), MOE_WIDE)

    n_big = jnp.where(one_wide, 0, (count + MOE_CHUNK - 1) // (2 * MOE_CHUNK))

    def big_pass(ci, carry):
        expert_pass(ci * (2 * MOE_CHUNK), 2 * MOE_CHUNK)
        return carry

    lax.fori_loop(0, n_big, big_pass, 0)

    @pl.when((~one_wide) & (count > n_big * (2 * MOE_CHUNK)))
    def _():
        expert_pass(n_big * (2 * MOE_CHUNK), MOE_CHUNK)

    @pl.when(e == N_EXPERTS - 1)
    def _():
        o_ref[...] = _ln_rows(ALPHA * x_ref[...] + acc_ref[...], g2_ref[...], b2_ref[...])


def _moe_call(x, comb, sel, tri, eg, eu, ed, j, g2, b2, tile0, n_tiles):
    row = lambda i, e: (tile0 + i, 0)
    fixed = lambda i, e: (0, 0)
    exp = lambda i, e: (j, e, 0, 0)
    return pl.pallas_call(
        _moe_kernel,
        grid=(n_tiles, N_EXPERTS),
        in_specs=[pl.BlockSpec((TM_MOE, D_MODEL), row),
                  pl.BlockSpec((TM_MOE, LANES), row),
                  pl.BlockSpec((TM_MOE, LANES), row),
                  pl.BlockSpec((TM_MOE, TM_MOE), fixed),
                  pl.BlockSpec((1, 1, D_MODEL, D_FF_EXPERT), exp),
                  pl.BlockSpec((1, 1, D_MODEL, D_FF_EXPERT), exp),
                  pl.BlockSpec((1, 1, D_FF_EXPERT, D_MODEL), exp),
                  _vec(D_MODEL), _vec(D_MODEL)],
        out_specs=pl.BlockSpec((TM_MOE, D_MODEL), lambda i, e: (i, 0)),
        out_shape=jax.ShapeDtypeStruct((n_tiles * TM_MOE, D_MODEL), F32),
        scratch_shapes=[pltpu.VMEM((TM_MOE, D_MODEL), BF),
                        pltpu.VMEM((LANES, TM_MOE), F32),
                        pltpu.VMEM((LANES, TM_MOE), F32),
                        pltpu.VMEM((LANES, TM_MOE), F32),
                        pltpu.VMEM((TM_MOE, -(-MOE_WIDE // LANES) * LANES), BF),
                        pltpu.VMEM((TM_MOE, D_MODEL), F32)],
        compiler_params=_cparams(2),
        name="moe_ffn",
    )(x, comb, sel, tri, eg, eu, ed, g2.reshape(1, -1), b2.reshape(1, -1))


def kernel(x_prompt, x_sample, cache_meta_k, cache_meta_v, cache_k, cache_v, state_conv,
           meta_tokens, ln_in_g, ln_in_b, w_in, w_dw, b_dw, conv_ln_g, conv_ln_b, sinks, w_out,
           ln1_g, ln1_b, ln2_g, ln2_b, ffd_w_gate, ffd_w_up, ffd_w_down,
           router_w, router_b, exp_w_gate, exp_w_up, exp_w_down):
    meta_blk = jnp.concatenate([meta_tokens, jnp.zeros((BLOCK - N_META, D_MODEL), F32)], 0)
    x_tail = jnp.concatenate([meta_blk] * BATCH + [x_sample.reshape(NS, D_MODEL),
                                                   jnp.zeros((T - PAD0, D_MODEL), F32)], 0)
    cos_np, sin_np = _rope_tables()
    cos_t, sin_t = jnp.asarray(cos_np), jnp.asarray(sin_np)
    tri = jnp.asarray(np.tril(np.ones((TM_MOE, TM_MOE), np.float32), -1), dtype=BF)
    bias = jnp.asarray(_attn_bias())
    w_in_b, w_out_b = w_in.astype(BF), w_out.astype(BF)
    ffd_b = (ffd_w_gate.astype(BF), ffd_w_up.astype(BF), ffd_w_down.astype(BF))
    exp_b = (exp_w_gate.astype(BF), exp_w_up.astype(BF), exp_w_down.astype(BF))

    state_t = state_conv.transpose(0, 2, 1, 3)

    per_kv = SAMPLE_ROWS // N_KV_HEADS
    head_of_row = (jnp.arange(SAMPLE_ROWS) // per_kv) * 4 + jnp.arange(SAMPLE_ROWS) % 4
    met_rows = lambda x, b: x[MET0 + b * BLOCK:MET0 + b * BLOCK + N_META]
    kv5 = lambda rows: jnp.stack(rows).reshape(BATCH, -1, N_KV_HEADS, HEAD_DIM)

    pmk, pmv, pk, pv, pc, sk, sv, sc = [], [], [], [], [], [], [], []
    y_tok = y_tail = None
    for i in range(DEPTH):
        if i == 0:
            h, q, k, v, u = _inproj0_call(x_prompt.reshape(TOK, D_MODEL), x_tail, ln_in_g,
                                          ln_in_b, w_in_b, cos_t, sin_t)
        else:
            q, k, v, u = _inproj_call(h, w_in_b, i, cos_t, sin_t)

        a, c = _prompt_call(sinks[i], bias, q, k, v, u, w_dw[i], b_dw[i], conv_ln_g[i],
                            conv_ln_b[i])
        c_m = _mconv_call(u, w_dw[i], b_dw[i], conv_ln_g[i], conv_ln_b[i])
        pmk.append(kv5([met_rows(k, b) for b in range(BATCH)]))
        pmv.append(kv5([met_rows(v, b) for b in range(BATCH)]))
        pk.append(kv5([k[(b + 1) * SEQ - WINDOW:(b + 1) * SEQ] for b in range(BATCH)]))
        pv.append(kv5([v[(b + 1) * SEQ - WINDOW:(b + 1) * SEQ] for b in range(BATCH)]))
        pc.append(jnp.stack([u[(b + 1) * SEQ - CONV_STATE:(b + 1) * SEQ] for b in range(BATCH)]))

        k2 = k[SMP0:PAD0].reshape(DEC_BATCH, DEC_SEQ, KV_WIDTH)
        v2 = v[SMP0:PAD0].reshape(DEC_BATCH, DEC_SEQ, KV_WIDTH)
        k_small = jnp.concatenate([cache_meta_k[i].reshape(DEC_BATCH, N_META, KV_WIDTH), k2], 1)
        v_small = jnp.concatenate([cache_meta_v[i].reshape(DEC_BATCH, N_META, KV_WIDTH), v2], 1)
        q2 = q[SMP0:PAD0].reshape(DEC_BATCH, DEC_SEQ, N_KV_HEADS, 4, HEAD_DIM)
        q2 = q2.transpose(0, 2, 1, 3, 4).reshape(DEC_BATCH, N_KV_HEADS, per_kv, HEAD_DIM)
        zq = jnp.zeros_like(q2[:, 0])
        qz = jnp.concatenate([jnp.concatenate([q2[:, 0], zq], -1),
                              jnp.concatenate([zq, q2[:, 1]], -1)], 1)
        sink_rows = jnp.broadcast_to(sinks[i][head_of_row][:, None], (SAMPLE_ROWS, LANES))
        o2, nk, nv = _sattn_call(sink_rows, qz,
                                 cache_k[i].reshape(DEC_BATCH, WINDOW, KV_WIDTH), k_small,
                                 cache_v[i].reshape(DEC_BATCH, WINDOW, KV_WIDTH), v_small)
        o2 = jnp.stack([o2[:, :per_kv, :HEAD_DIM], o2[:, per_kv:, HEAD_DIM:]], 1)
        a_s = o2.reshape(DEC_BATCH, N_KV_HEADS, DEC_SEQ, 4, HEAD_DIM).transpose(0, 2, 1, 3, 4)
        a_s = a_s.reshape(NS, ATTN_WIDTH).astype(BF)
        u2 = u[SMP0:PAD0]
        c_s, sc_i = _sconv_call(state_t, i, u2, w_dw[i], b_dw[i], conv_ln_g[i], conv_ln_b[i])
        c_s = c_s.transpose(1, 0, 2).reshape(NS, CONV_CH)
        sk.append(nk.reshape(DEC_BATCH, WINDOW, N_KV_HEADS, HEAD_DIM))
        sv.append(nv.reshape(DEC_BATCH, WINDOW, N_KV_HEADS, HEAD_DIM))
        sc.append(sc_i)

        a = lax.dynamic_update_slice(a, a_s, (SMP0, 0))
        c = lax.dynamic_update_slice(c, jnp.concatenate([c_m, c_s], 0), (MET0, 0))
        j = i // 2
        if i % 2 == 0:
            h = _dense_call(a, c, h, w_out_b, i, ln1_g[i], ln1_b[i], *ffd_b, j,
                            ln2_g[i], ln2_b[i])
        else:
            rw = jnp.pad(router_w[j], ((0, 0), (0, LANES - N_EXPERTS)))
            rwh = rw.astype(BF)
            rwl = (rw - rwh.astype(F32)).astype(BF)
            rb = jnp.pad(router_b[j], (0, LANES - N_EXPERTS)).reshape(1, LANES)
            h1, comb, sel = _router_call(a, c, h, w_out_b, i, ln1_g[i], ln1_b[i], rwh, rwl, rb)
            moe = functools.partial(_moe_call, h1, comb, sel, tri, *exp_b, j,
                                    ln2_g[i], ln2_b[i])
            if i < DEPTH - 1:
                h = moe(0, T // TM_MOE)
            else:
                y_tok = moe(0, TOK // TM_MOE)
                y_tail = moe(TOK // TM_MOE, TAIL // TM_MOE)

    y_prompt = y_tok.reshape(BATCH, SEQ, D_MODEL)
    y_sample = y_tail[SMP0 - TOK:PAD0 - TOK].reshape(DEC_BATCH, DEC_SEQ, D_MODEL)
    return (y_prompt, y_sample, jnp.stack(pmk), jnp.stack(pmv), jnp.stack(pk), jnp.stack(pv),
            jnp.stack(pc), jnp.stack(sk), jnp.stack(sv), jnp.stack(sc).transpose(0, 2, 1, 3))
```
